```python
import jax, jax.numpy as jnp
from jax import lax
import numpy as np

D_MODEL = 1024
BATCH = 4
SEQ = 8192
DEPTH = 2

GRID_W = 64
CTX_LEN = 256
N_MIXERS = 2
RMS_EPS = 1e-6
GLA_HEADS = 4
GLA_DK = D_MODEL // 2
GLA_DV = D_MODEL
GLA_HEAD_K = GLA_DK // GLA_HEADS
GLA_HEAD_V = GLA_DV // GLA_HEADS
GLA_GATE_RANK = 16
GLA_GATE_NORM = 16.0
GLA_CHUNK = 64
GLA_IN = 2 * GLA_DK + 2 * GLA_DV + 2 * GLA_GATE_RANK
LRU_WIDTH = 1280
LRU_BLOCKS = 5
LRU_BLOCK_W = LRU_WIDTH // LRU_BLOCKS
LRU_C = 8.0
CONV_W = 4

kernel_name = "hybrid_gla_rglru_diffusion_trunk"


def rms_norm(x, g):
    xf = x.astype(jnp.float32)
    y = xf * lax.rsqrt(jnp.mean(xf * xf, axis=-1, keepdims=True) + RMS_EPS)
    return (y * g.astype(jnp.float32)).astype(x.dtype)


def modulate(h, shift, scale):
    return h * (1 + scale) + shift


def to_col_major(z, rows):
    b, t, d = z.shape
    return z.reshape(b, rows, GRID_W, d).transpose(0, 2, 1, 3).reshape(b, t, d)


def from_col_major(z, rows):
    b, t, d = z.shape
    return z.reshape(b, GRID_W, rows, d).transpose(0, 2, 1, 3).reshape(b, t, d)


def gla_scan(q, k, v, log_a, s0):
    b, t, h, _ = q.shape
    dv = v.shape[-1]
    n = t // GLA_CHUNK

    def chunks(z):
        return z.reshape(b, n, GLA_CHUNK, h, z.shape[-1]).transpose(1, 0, 3, 2, 4)

    causal = jnp.tril(jnp.ones((GLA_CHUNK, GLA_CHUNK), dtype=bool))[:, :, None]

    def step(state, inp):
        qc, kc, vc, gc = (z.astype(jnp.float32) for z in inp)
        cum = jnp.cumsum(gc, axis=2)
        o_inter = jnp.einsum('bhck,bhkv->bhcv', qc * jnp.exp(cum), state)
        diff = cum[:, :, :, None, :] - cum[:, :, None, :, :]
        decay = jnp.exp(jnp.where(causal, diff, -jnp.inf))
        scores = jnp.einsum('bhijk,bhjk->bhij', qc[:, :, :, None, :] * decay, kc)
        o_intra = jnp.einsum('bhij,bhjv->bhiv', scores, vc)
        last = cum[:, :, -1:, :]
        state = state * jnp.exp(last[:, :, 0, :, None]) + jnp.einsum(
            'bhck,bhcv->bhkv', kc * jnp.exp(last - cum), vc)
        return state, o_inter + o_intra

    s_final, out = lax.scan(step, s0, (chunks(q), chunks(k), chunks(v), chunks(log_a)))
    out = out.transpose(1, 0, 3, 2, 4).reshape(b, t, h, dv)
    return out.astype(q.dtype), s_final


def gla_mixer(h, w_in, wg_f, bg_f, wg_b, bg_b, norm_w, w_out, s0_f, s0_b, need_out):
    bsz, t, _ = h.shape
    proj = h @ w_in
    splits = [GLA_DK, 2 * GLA_DK, 2 * GLA_DK + GLA_DV, 2 * GLA_DK + 2 * GLA_DV,
              2 * GLA_DK + 2 * GLA_DV + GLA_GATE_RANK]
    q, k, v, g, lr_f, lr_b = jnp.split(proj, splits, axis=-1)
    q = q.reshape(bsz, t, GLA_HEADS, GLA_HEAD_K) * (GLA_HEAD_K ** -0.5)
    k = k.reshape(bsz, t, GLA_HEADS, GLA_HEAD_K)
    v = v.reshape(bsz, t, GLA_HEADS, GLA_HEAD_V)

    def log_gate(lr, w, bias):
        z = (lr @ w + bias).astype(jnp.float32)
        return (jax.nn.log_sigmoid(z) / GLA_GATE_NORM).reshape(bsz, t, GLA_HEADS, GLA_HEAD_K)

    o_f, s_f = gla_scan(q, k, v, log_gate(lr_f, wg_f, bg_f), s0_f)
    o_b_rev, s_b = gla_scan(jnp.flip(q, 1), jnp.flip(k, 1), jnp.flip(v, 1),
                            jnp.flip(log_gate(lr_b, wg_b, bg_b), 1), s0_b)
    if not need_out:
        return None, s_f, s_b
    o = rms_norm(o_f + jnp.flip(o_b_rev, 1), norm_w).reshape(bsz, t, GLA_DV)
    y = (o * jax.nn.silu(g)) @ w_out
    return y, s_f, s_b


def centred_dwconv(z, w, bias):
    left = CONV_W // 2
    y = lax.conv_general_dilated(z, w[:, None, :].astype(z.dtype), window_strides=(1,),
                                 padding=[(left, CONV_W - 1 - left)],
                                 dimension_numbers=('NWC', 'WIO', 'NWC'),
                                 feature_group_count=z.shape[-1])
    return y + bias


def rglru_scan(z, w_a, b_a, w_x, b_x, lam, h0):
    bsz, t, _ = z.shape
    zb = z.reshape(bsz, t, LRU_BLOCKS, LRU_BLOCK_W)
    r = jax.nn.sigmoid((jnp.einsum('btnd,nde->btne', zb, w_a).reshape(bsz, t, LRU_WIDTH) + b_a).astype(jnp.float32))
    i = jax.nn.sigmoid((jnp.einsum('btnd,nde->btne', zb, w_x).reshape(bsz, t, LRU_WIDTH) + b_x).astype(jnp.float32))
    log_a = -LRU_C * r * jax.nn.softplus(-lam.astype(jnp.float32))
    a = jnp.exp(log_a)
    u = jnp.sqrt(-jnp.expm1(2.0 * log_a)) * (i * z.astype(jnp.float32))
    u = u.at[:, 0].add(a[:, 0] * h0)

    def combine(lhs, rhs):
        a_l, u_l = lhs
        a_r, u_r = rhs
        return a_l * a_r, a_r * u_l + u_r

    _, hs = lax.associative_scan(combine, (a, u), axis=1)
    return hs, hs[:, -1]


def rglru_mixer(h, w_in, conv_w, conv_b, p_f, p_b, w_out, h0_f, h0_b, need_out):
    proj = h @ w_in
    z, g = jnp.split(proj, [LRU_WIDTH], axis=-1)
    z = centred_dwconv(z, conv_w, conv_b)
    hf, s_f = rglru_scan(z, *p_f, h0_f)
    hb_rev, s_b = rglru_scan(jnp.flip(z, 1), *p_b, h0_b)
    if not need_out:
        return None, s_f, s_b
    y = ((hf + jnp.flip(hb_rev, 1)).astype(h.dtype) * jax.nn.silu(g)) @ w_out
    return y, s_f, s_b


def setup_inputs(seed: int = 0) -> dict:
    key = jax.random.key(seed)
    ks = iter(jax.random.split(key, 40))
    n_gla = (DEPTH + 1) // 2
    n_lru = DEPTH // 2
    D = D_MODEL

    def nrm(shape, std):
        return jax.random.normal(next(ks), shape, jnp.float32) * std

    u = jax.random.uniform(next(ks), (2, n_lru, LRU_WIDTH), jnp.float32, minval=0.9, maxval=0.999)
    a0 = u ** (1.0 / LRU_C)
    lam = jnp.log(a0) - jnp.log1p(-a0)
    bw = LRU_BLOCK_W ** -0.5
    return {
        "x": nrm((BATCH, SEQ, D), 1.0),
        "c": nrm((BATCH, D), 1.0),
        "ctx": nrm((BATCH, CTX_LEN, D), 1.0),
        "c_ctx": nrm((D,), 1.0),
        "ada_w": nrm((DEPTH, D, 3 * D), 0.5 * D ** -0.5),
        "ada_b": nrm((DEPTH, 3 * D), 0.02),
        "norm_pre": 1.0 + nrm((DEPTH, D), 0.05),
        "norm_post": 1.0 + nrm((DEPTH, D), 0.05),
        "gla_w_in": nrm((n_gla, D, GLA_IN), D ** -0.5),
        "gla_wg_f": nrm((n_gla, GLA_GATE_RANK, GLA_DK), GLA_GATE_RANK ** -0.5),
        "gla_bg_f": nrm((n_gla, GLA_DK), 0.5),
        "gla_wg_b": nrm((n_gla, GLA_GATE_RANK, GLA_DK), GLA_GATE_RANK ** -0.5),
        "gla_bg_b": nrm((n_gla, GLA_DK), 0.5),
        "gla_norm": 1.0 + nrm((n_gla, GLA_HEAD_V), 0.05),
        "gla_w_out": nrm((n_gla, GLA_DV, D), GLA_DV ** -0.5),
        "lru_w_in": nrm((n_lru, D, 2 * LRU_WIDTH), D ** -0.5),
        "lru_conv_w": nrm((n_lru, CONV_W, LRU_WIDTH), CONV_W ** -0.5),
        "lru_conv_b": nrm((n_lru, LRU_WIDTH), 0.02),
        "lru_wa_f": nrm((n_lru, LRU_BLOCKS, LRU_BLOCK_W, LRU_BLOCK_W), bw),
        "lru_ba_f": nrm((n_lru, LRU_WIDTH), 0.1),
        "lru_wx_f": nrm((n_lru, LRU_BLOCKS, LRU_BLOCK_W, LRU_BLOCK_W), bw),
        "lru_bx_f": nrm((n_lru, LRU_WIDTH), 0.1),
        "lru_lam_f": lam[0],
        "lru_wa_b": nrm((n_lru, LRU_BLOCKS, LRU_BLOCK_W, LRU_BLOCK_W), bw),
        "lru_ba_b": nrm((n_lru, LRU_WIDTH), 0.1),
        "lru_wx_b": nrm((n_lru, LRU_BLOCKS, LRU_BLOCK_W, LRU_BLOCK_W), bw),
        "lru_bx_b": nrm((n_lru, LRU_WIDTH), 0.1),
        "lru_lam_b": lam[1],
        "lru_w_out": nrm((n_lru, LRU_WIDTH, D), LRU_WIDTH ** -0.5),
    }


def reference(x, c, ctx, c_ctx, ada_w, ada_b, norm_pre, norm_post,
              gla_w_in, gla_wg_f, gla_bg_f, gla_wg_b, gla_bg_b, gla_norm, gla_w_out,
              lru_w_in, lru_conv_w, lru_conv_b, lru_wa_f, lru_ba_f, lru_wx_f, lru_bx_f, lru_lam_f,
              lru_wa_b, lru_ba_b, lru_wx_b, lru_bx_b, lru_lam_b, lru_w_out):
    bsz, seq, _ = x.shape
    rows = seq // GRID_W
    sc = jax.nn.silu(c)
    sc_ctx = jax.nn.silu(c_ctx)
    for i in range(DEPTH):
        last = i == DEPTH - 1
        j = i // N_MIXERS
        mod = sc @ ada_w[i] + ada_b[i]
        mod_c = sc_ctx @ ada_w[i] + ada_b[i]
        sh, scl, gt = jnp.split(mod[:, None, :], 3, axis=-1)
        sh_c, scl_c, gt_c = jnp.split(mod_c, 3, axis=-1)
        h = modulate(rms_norm(x, norm_pre[i]), sh, scl)
        h_c = modulate(rms_norm(ctx, norm_pre[i]), sh_c, scl_c)
        if i % N_MIXERS == 0:
            params = (gla_w_in[j], gla_wg_f[j], gla_bg_f[j], gla_wg_b[j], gla_bg_b[j], gla_norm[j], gla_w_out[j])
            s0 = jnp.zeros((ctx.shape[0], GLA_HEADS, GLA_HEAD_K, GLA_HEAD_V), jnp.float32)
            y_c, s_f, s_b = gla_mixer(h_c, *params, s0, s0, not last)
            y, _, _ = gla_mixer(h, *params, s_f, s_b, True)
        else:
            p_f = (lru_wa_f[j], lru_ba_f[j], lru_wx_f[j], lru_bx_f[j], lru_lam_f[j])
            p_b = (lru_wa_b[j], lru_ba_b[j], lru_wx_b[j], lru_bx_b[j], lru_lam_b[j])
            h0 = jnp.zeros((ctx.shape[0], LRU_WIDTH), jnp.float32)
            y_c, s_f, s_b = rglru_mixer(h_c, lru_w_in[j], lru_conv_w[j], lru_conv_b[j], p_f, p_b,
                                        lru_w_out[j], h0, h0, not last)
            y, _, _ = rglru_mixer(to_col_major(h, rows), lru_w_in[j], lru_conv_w[j], lru_conv_b[j],
                                  p_f, p_b, lru_w_out[j], s_f, s_b, True)
            y = from_col_major(y, rows)
        x = x + gt * rms_norm(y, norm_post[i])
        if not last:
            ctx = ctx + gt_c * rms_norm(y_c, norm_post[i])
    return x
```

```python
import functools

import jax
import jax.numpy as jnp
from jax import lax
from jax.experimental import pallas as pl
from jax.experimental.pallas import tpu as pltpu

F32 = jnp.float32
BF16 = jnp.bfloat16

RMS_EPS = 1e-6
GRID_W = 64
GLA_HEADS = 4
GLA_HEAD_K = 128
GLA_HEAD_V = 256
GLA_DK = GLA_HEADS * GLA_HEAD_K
GLA_DV = GLA_HEADS * GLA_HEAD_V
GLA_GATE_RANK = 16
GLA_GATE_NORM = 16.0
GLA_CHUNK = 128
LRU_BLOCKS = 5
LRU_BLOCK_W = 256
LRU_WIDTH = LRU_BLOCKS * LRU_BLOCK_W
LRU_C = 8.0
CONV_W = 4
CONV_LEFT = CONV_W // 2
N_SEG = 8

VMEM_LIMIT_BYTES = 56 * 1024 * 1024


def _cparams(*sem):
    return pltpu.CompilerParams(dimension_semantics=sem, vmem_limit_bytes=VMEM_LIMIT_BYTES)


def _silu(x):
    return x * (0.5 * (1.0 + jnp.tanh(0.5 * x)))


def _sigmoid(x):
    return 0.5 * (1.0 + jnp.tanh(0.5 * x))


def _log_sigmoid(x):
    return jnp.minimum(x, 0.0) - jnp.log1p(jnp.exp(-jnp.abs(x)))


def _softplus(x):
    return jnp.maximum(x, 0.0) + jnp.log1p(jnp.exp(-jnp.abs(x)))


def _split_bf16(x):
    hi = x.astype(BF16)
    lo = (x - hi.astype(F32)).astype(BF16)
    return hi, lo


def _dot(a, b):
    return jnp.dot(a, b, preferred_element_type=F32)


def _dot_x3(a, b):
    a_hi, a_lo = _split_bf16(a)
    b_hi, b_lo = _split_bf16(b)
    return _dot(a_hi, b_hi) + (_dot(a_hi, b_lo) + _dot(a_lo, b_hi))


def _pre_norm_modulate(x, gpre, sh, scl):
    ms = jnp.mean(x * x, axis=-1, keepdims=True)
    h = x * lax.rsqrt(ms + RMS_EPS) * gpre
    return h * (1.0 + scl) + sh


def _post_norm_residual(x, y, gpost, gt):
    ms = jnp.mean(y * y, axis=-1, keepdims=True)
    return x + gt * (y * lax.rsqrt(ms + RMS_EPS) * gpost)


def _ada_kernel(c_ref, w_ref, b_ref, o_ref):
    sc = _silu(c_ref[...])
    o_ref[...] = _dot_x3(sc, w_ref[...]) + b_ref[...]


def _ada_modulation(cvec, ada_w, ada_b):
    depth, d, n3 = ada_w.shape
    tn = 1024
    return pl.pallas_call(
        _ada_kernel,
        grid=(depth, n3 // tn),
        in_specs=[pl.BlockSpec((8, d), lambda i, j: (0, 0)),
                  pl.BlockSpec((None, d, tn), lambda i, j: (i, 0, j)),
                  pl.BlockSpec((None, 1, tn), lambda i, j: (i, 0, j))],
        out_specs=pl.BlockSpec((None, 8, tn), lambda i, j: (i, 0, j)),
        out_shape=jax.ShapeDtypeStruct((depth, 8, n3), F32),
        compiler_params=_cparams("parallel", "parallel"),
        name="ada_modulation",
    )(cvec, ada_w, ada_b.reshape(depth, 1, n3))


def _gla_in_kernel(x_ref, sh_ref, scl_ref, gpre_ref, w_ref, wg_ref, bg_ref,
                   q_ref, k_ref, v_ref, sg_ref, cumf_ref, cumb_ref):
    tm = x_ref.shape[0]
    h = _pre_norm_modulate(x_ref[...], gpre_ref[...], sh_ref[...], scl_ref[...])
    proj = _dot(h.astype(BF16), w_ref[...])
    q_ref[...] = (proj[:, :GLA_DK] * (GLA_HEAD_K ** -0.5)).astype(BF16)
    k_ref[...] = proj[:, GLA_DK:2 * GLA_DK].astype(BF16)
    v_ref[...] = proj[:, 2 * GLA_DK:2 * GLA_DK + GLA_DV].astype(BF16)
    sg_ref[...] = _silu(proj[:, 2 * GLA_DK + GLA_DV:2 * GLA_DK + 2 * GLA_DV]).astype(BF16)
    lr = proj[:, 2 * GLA_DK + 2 * GLA_DV:]
    z = _dot(lr.astype(BF16), wg_ref[...]) + bg_ref[...]
    log_a = _log_sigmoid(z) * (1.0 / GLA_GATE_NORM)
    c = GLA_CHUNK
    row = lax.broadcasted_iota(jnp.int32, (c, c), 0)
    col = lax.broadcasted_iota(jnp.int32, (c, c), 1)
    lower = (row >= col).astype(BF16)
    upper = (row <= col).astype(BF16)
    for i in range(tm // c):
        la = log_a[i * c:(i + 1) * c]
        hi, lo = _split_bf16(la)
        cumf_ref[i * c:(i + 1) * c, :] = _dot(lower, hi[:, :GLA_DK]) + _dot(lower, lo[:, :GLA_DK])
        cumb_ref[i * c:(i + 1) * c, :] = _dot(upper, hi[:, GLA_DK:]) + _dot(upper, lo[:, GLA_DK:])


def _gla_in(x, sh, scl, gpre, w, wg, bg, tm):
    bsz, t, d = x.shape
    n = w.shape[1]
    row = lambda wd: pl.BlockSpec((None, tm, wd), lambda b, i: (b, i, 0))
    vec = lambda wd: pl.BlockSpec((None, 1, wd), lambda b, i: (b, 0, 0))
    full = lambda a: pl.BlockSpec(a.shape, lambda b, i: (0,) * a.ndim)
    return pl.pallas_call(
        _gla_in_kernel,
        grid=(bsz, t // tm),
        in_specs=[row(d), vec(d), vec(d), full(gpre), full(w), full(wg), full(bg)],
        out_specs=[row(GLA_DK), row(GLA_DK), row(GLA_DV), row(GLA_DV), row(GLA_DK), row(GLA_DK)],
        out_shape=[jax.ShapeDtypeStruct((bsz, t, GLA_DK), BF16),
                   jax.ShapeDtypeStruct((bsz, t, GLA_DK), BF16),
                   jax.ShapeDtypeStruct((bsz, t, GLA_DV), BF16),
                   jax.ShapeDtypeStruct((bsz, t, GLA_DV), BF16),
                   jax.ShapeDtypeStruct((bsz, t, GLA_DK), F32),
                   jax.ShapeDtypeStruct((bsz, t, GLA_DK), F32)],
        compiler_params=_cparams("parallel", "parallel"),
        name="gla_in",
    )(x, sh, scl, gpre, w, wg, bg)


def _gla_scan_kernel(q_ref, k_ref, v_ref, cum_ref, s0_ref, o_ref, sfin_ref, s_ref, *, reverse):
    n = pl.program_id(1)
    tb = q_ref.shape[0]
    c = GLA_CHUNK

    @pl.when(n == 0)
    def _():
        s_ref[...] = s0_ref[...]

    row = lax.broadcasted_iota(jnp.int32, (c, c), 0)
    col = lax.broadcasted_iota(jnp.int32, (c, c), 1)
    keep = (row <= col) if reverse else (row >= col)
    chunks = range(tb // c)
    for i in (reversed(chunks) if reverse else chunks):
        rows = slice(i * c, (i + 1) * c)
        for hd in range(GLA_HEADS):
            kc = slice(hd * GLA_HEAD_K, (hd + 1) * GLA_HEAD_K)
            vc = slice(hd * GLA_HEAD_V, (hd + 1) * GLA_HEAD_V)
            cum = cum_ref[rows, kc]
            last = cum[0:1] if reverse else cum[c - 1:c]
            q = q_ref[rows, kc].astype(F32)
            k = k_ref[rows, kc].astype(F32)
            v = v_ref[rows, vc]
            qe = (q * jnp.exp(cum)).astype(BF16)
            ke = (k * jnp.exp(-cum)).astype(BF16)
            kl = (k * jnp.exp(last - cum)).astype(BF16)
            st = s_ref[hd]
            scores = lax.dot_general(qe, ke, (((1,), (1,)), ((), ())), preferred_element_type=F32)
            scores = jnp.where(keep, scores, 0.0).astype(BF16)
            o = lax.dot_general(qe, st.astype(BF16), (((1,), (1,)), ((), ())),
                                preferred_element_type=F32)
            o = o + _dot(scores, v)
            o_ref[rows, vc] = o.astype(BF16)
            upd = lax.dot_general(v, kl, (((0,), (0,)), ((), ())), preferred_element_type=F32)
            s_ref[hd] = st * jnp.exp(last) + upd

    @pl.when(n == pl.num_programs(1) - 1)
    def _():
        sfin_ref[...] = s_ref[...]


def _gla_scan(q, k, v, cum, s0, tb, reverse):
    bsz, t, _ = q.shape
    nblk = t // tb
    blk = (lambda b, n: (b, nblk - 1 - n, 0)) if reverse else (lambda b, n: (b, n, 0))
    row = lambda wd: pl.BlockSpec((None, tb, wd), blk)
    st = pl.BlockSpec((None, GLA_HEADS, GLA_HEAD_V, GLA_HEAD_K), lambda b, n: (b, 0, 0, 0))
    return pl.pallas_call(
        functools.partial(_gla_scan_kernel, reverse=reverse),
        grid=(bsz, nblk),
        in_specs=[row(GLA_DK), row(GLA_DK), row(GLA_DV), row(GLA_DK), st],
        out_specs=[row(GLA_DV), st],
        out_shape=[jax.ShapeDtypeStruct((bsz, t, GLA_DV), BF16),
                   jax.ShapeDtypeStruct(s0.shape, F32)],
        scratch_shapes=[pltpu.VMEM((GLA_HEADS, GLA_HEAD_V, GLA_HEAD_K), F32)],
        compiler_params=_cparams("parallel", "arbitrary"),
        name="gla_scan_bwd" if reverse else "gla_scan_fwd",
    )(q, k, v, cum, s0)


def _gla_out_kernel(of_ref, ob_ref, sg_ref, x_ref, gt_ref, gn_ref, gpost_ref, w_ref, o_ref):
    o = of_ref[...].astype(F32) + ob_ref[...].astype(F32)
    gn = gn_ref[...]
    parts = []
    for hd in range(GLA_HEADS):
        oh = o[:, hd * GLA_HEAD_V:(hd + 1) * GLA_HEAD_V]
        ms = jnp.mean(oh * oh, axis=-1, keepdims=True)
        parts.append(oh * lax.rsqrt(ms + RMS_EPS) * gn)
    on = jnp.concatenate(parts, axis=-1) * sg_ref[...].astype(F32)
    y = _dot(on.astype(BF16), w_ref[...])
    o_ref[...] = _post_norm_residual(x_ref[...], y, gpost_ref[...], gt_ref[...])


def _gla_out(o_f, o_b, sg, x, gt, gn, gpost, w, tm):
    bsz, t, d = x.shape
    row = lambda wd: pl.BlockSpec((None, tm, wd), lambda b, i: (b, i, 0))
    vec = lambda wd: pl.BlockSpec((None, 1, wd), lambda b, i: (b, 0, 0))
    full = lambda a: pl.BlockSpec(a.shape, lambda b, i: (0,) * a.ndim)
    return pl.pallas_call(
        _gla_out_kernel,
        grid=(bsz, t // tm),
        in_specs=[row(GLA_DV), row(GLA_DV), row(GLA_DV), row(d), vec(d), full(gn), full(gpost), full(w)],
        out_specs=row(d),
        out_shape=jax.ShapeDtypeStruct((bsz, t, d), F32),
        compiler_params=_cparams("parallel", "parallel"),
        name="gla_out",
    )(o_f, o_b, sg, x, gt, gn, gpost, w)


def _gla_layer(x, sh, scl, gt, gpre, gpost, w_in, wg, bg, gn, w_out, s0_f, s0_b):
    t = x.shape[1]
    tm = min(512, t)
    q, k, v, sg, cum_f, cum_b = _gla_in(x, sh, scl, gpre, w_in, wg, bg, tm)
    o_f, s_f = _gla_scan(q, k, v, cum_f, s0_f, tm, reverse=False)
    o_b, s_b = _gla_scan(q, k, v, cum_b, s0_b, tm, reverse=True)
    x_new = _gla_out(o_f, o_b, sg, x, gt, gn, gpost, w_out, tm)
    return x_new, s_f, s_b


def _lru_in_kernel(x_ref, sh_ref, scl_ref, gpre_ref, w_ref, z_ref, sg_ref):
    r, s, d = x_ref.shape
    x = x_ref[...].reshape(r * s, d)
    h = _pre_norm_modulate(x, gpre_ref[...], sh_ref[...], scl_ref[...])
    proj = _dot(h.astype(BF16), w_ref[...])
    z_ref[...] = proj[:, :LRU_WIDTH]
    sg_ref[...] = _silu(proj[:, LRU_WIDTH:]).astype(BF16)


def _lru_in(x5, sh, scl, gpre, w):
    bsz, r, s, njd = x5.shape
    d = w.shape[0]
    nj = njd // d
    tm = r * s
    xin = pl.BlockSpec((None, r, s, d), lambda b, j: (b, 0, 0, j))
    row = lambda wd: pl.BlockSpec((None, tm, wd), lambda b, j: (b, j, 0))
    vec = lambda wd: pl.BlockSpec((None, 1, wd), lambda b, j: (b, 0, 0))
    full = lambda a: pl.BlockSpec(a.shape, lambda b, j: (0,) * a.ndim)
    return pl.pallas_call(
        _lru_in_kernel,
        grid=(bsz, nj),
        in_specs=[xin, vec(d), vec(d), full(gpre), full(w)],
        out_specs=[row(LRU_WIDTH), row(LRU_WIDTH)],
        out_shape=[jax.ShapeDtypeStruct((bsz, tm * nj, LRU_WIDTH), F32),
                   jax.ShapeDtypeStruct((bsz, tm * nj, LRU_WIDTH), BF16)],
        compiler_params=_cparams("parallel", "parallel"),
        name="lru_in",
    )(x5, sh, scl, gpre, w)


def _lru_scan_kernel(z_ref, zp_ref, zn_ref, cw_ref, cb_ref, wa_ref, ba_ref, wx_ref, bx_ref, lam_ref,
                     hloc_ref, ploc_ref, hfin_ref, pfin_ref,
                     zext_s, a_s, u_s, h_s, p_s, hst_s, pst_s, *, reverse):
    n = pl.program_id(1)
    nblk = pl.num_programs(1)
    blk = (nblk - 1 - n) if reverse else n
    rows = z_ref.shape[0]
    tb = rows // N_SEG
    halo_l = CONV_LEFT * N_SEG
    halo_r = (CONV_W - 1 - CONV_LEFT) * N_SEG
    seg = lax.broadcasted_iota(jnp.int32, (N_SEG, LRU_WIDTH), 0)

    for g in range(CONV_LEFT):
        zp = zp_ref[g * N_SEG:(g + 1) * N_SEG, :]
        wrapped = jnp.where(seg == 0, 0.0, pltpu.roll(zp, 1, 0))
        zext_s[g * N_SEG:(g + 1) * N_SEG, :] = jnp.where(blk == 0, wrapped, zp)
    zext_s[halo_l:halo_l + rows, :] = z_ref[...]
    for g in range(CONV_W - 1 - CONV_LEFT):
        zn = zn_ref[g * N_SEG:(g + 1) * N_SEG, :]
        wrapped = jnp.where(seg == N_SEG - 1, 0.0, pltpu.roll(zn, N_SEG - 1, 0))
        zext_s[halo_l + rows + g * N_SEG:halo_l + rows + (g + 1) * N_SEG, :] = jnp.where(
            blk == nblk - 1, wrapped, zn)

    zc = cb_ref[...]
    for j in range(CONV_W):
        zc = zc + cw_ref[j:j + 1, :] * zext_s[j * N_SEG:j * N_SEG + rows, :]
    zcb = zc.astype(BF16)
    for nb in range(LRU_BLOCKS):
        cs = slice(nb * LRU_BLOCK_W, (nb + 1) * LRU_BLOCK_W)
        r = _sigmoid(_dot(zcb[:, cs], wa_ref[nb]) + ba_ref[:, cs])
        i = _sigmoid(_dot(zcb[:, cs], wx_ref[nb]) + bx_ref[:, cs])
        log_a = (-LRU_C) * r * _softplus(-lam_ref[:, cs])
        a = jnp.exp(log_a)
        a_s[:, cs] = a
        u_s[:, cs] = jnp.sqrt(1.0 - a * a) * (i * zc[:, cs])

    @pl.when(n == 0)
    def _():
        hst_s[...] = jnp.zeros_like(hst_s)
        pst_s[...] = jnp.ones_like(pst_s)

    def step(s, carry):
        h, p = carry
        idx = (tb - 1 - s) if reverse else s
        r0 = pl.multiple_of(idx * N_SEG, N_SEG)
        a = a_s[pl.ds(r0, N_SEG), :]
        h = a * h + u_s[pl.ds(r0, N_SEG), :]
        p = a * p
        h_s[pl.ds(r0, N_SEG), :] = h
        p_s[pl.ds(r0, N_SEG), :] = p
        return h, p

    h, p = lax.fori_loop(0, tb, step, (hst_s[...], pst_s[...]), unroll=4)
    hst_s[...] = h
    pst_s[...] = p
    hloc_ref[...] = h_s[...].astype(BF16)
    ploc_ref[...] = p_s[...].astype(BF16)
    hfin_ref[...] = h
    pfin_ref[...] = p


def _lru_scan(z, conv_w, conv_b, wa, ba, wx, bx, lam, tb, reverse):
    bsz, t, w = z.shape
    rows = tb * N_SEG
    nblk = t // rows
    halo_l = CONV_LEFT * N_SEG
    halo_r = (CONV_W - 1 - CONV_LEFT) * N_SEG
    pos = (lambda n: nblk - 1 - n) if reverse else (lambda n: n)
    cur = pl.BlockSpec((None, rows, w), lambda b, n: (b, pos(n), 0))
    prev = pl.BlockSpec((None, halo_l, w),
                        lambda b, n: (b, (pos(n) * (rows // halo_l) + t // halo_l - 1) % (t // halo_l), 0))
    nxt = pl.BlockSpec((None, halo_r, w),
                       lambda b, n: (b, ((pos(n) + 1) * (rows // halo_r)) % (t // halo_r), 0))
    full = lambda a: pl.BlockSpec(a.shape, lambda b, n: (0,) * a.ndim)
    fin = pl.BlockSpec((None, N_SEG, w), lambda b, n: (b, 0, 0))
    return pl.pallas_call(
        functools.partial(_lru_scan_kernel, reverse=reverse),
        grid=(bsz, nblk),
        in_specs=[cur, prev, nxt, full(conv_w), full(conv_b), full(wa), full(ba), full(wx), full(bx), full(lam)],
        out_specs=[cur, cur, fin, fin],
        out_shape=[jax.ShapeDtypeStruct((bsz, t, w), BF16),
                   jax.ShapeDtypeStruct((bsz, t, w), BF16),
                   jax.ShapeDtypeStruct((bsz, N_SEG, w), F32),
                   jax.ShapeDtypeStruct((bsz, N_SEG, w), F32)],
        scratch_shapes=[pltpu.VMEM((rows + halo_l + halo_r, w), F32),
                        pltpu.VMEM((rows, w), F32), pltpu.VMEM((rows, w), F32),
                        pltpu.VMEM((rows, w), F32), pltpu.VMEM((rows, w), F32),
                        pltpu.VMEM((N_SEG, w), F32), pltpu.VMEM((N_SEG, w), F32)],
        compiler_params=_cparams("parallel", "arbitrary"),
        name="lru_scan_bwd" if reverse else "lru_scan_fwd",
    )(z, z, z, conv_w, conv_b, wa, ba, wx, bx, lam)


def _lru_carry_kernel(h0_ref, hfin_ref, pfin_ref, carry_ref, final_ref, *, reverse):
    c = h0_ref[...]
    order = range(N_SEG - 1, -1, -1) if reverse else range(N_SEG)
    for r in order:
        carry_ref[r:r + 1, :] = c
        c = hfin_ref[r:r + 1, :] + pfin_ref[r:r + 1, :] * c
    final_ref[...] = c


def _lru_carry(h0, hfin, pfin, reverse):
    bsz, s, w = hfin.shape
    vec = pl.BlockSpec((None, 1, w), lambda b: (b, 0, 0))
    segs = pl.BlockSpec((None, s, w), lambda b: (b, 0, 0))
    return pl.pallas_call(
        functools.partial(_lru_carry_kernel, reverse=reverse),
        grid=(bsz,),
        in_specs=[vec, segs, segs],
        out_specs=[segs, vec],
        out_shape=[jax.ShapeDtypeStruct((bsz, s, w), F32), jax.ShapeDtypeStruct((bsz, 1, w), F32)],
        compiler_params=_cparams("parallel"),
        name="lru_carry_bwd" if reverse else "lru_carry_fwd",
    )(h0, hfin, pfin)


def _lru_out_kernel(hf_ref, pf_ref, cf_ref, hb_ref, pb_ref, cb_ref, sg_ref, x_ref, gt_ref, gpost_ref, w_ref, o_ref):
    r, s, d = x_ref.shape
    w = hf_ref.shape[-1]

    def stitched(h_ref, p_ref, c_ref):
        h = h_ref[...].astype(F32).reshape(r, s, w)
        p = p_ref[...].astype(F32).reshape(r, s, w)
        return h + p * c_ref[...][None]

    h = (stitched(hf_ref, pf_ref, cf_ref) + stitched(hb_ref, pb_ref, cb_ref)).reshape(r * s, w)
    y = _dot((h * sg_ref[...].astype(F32)).astype(BF16), w_ref[...])
    x = x_ref[...].reshape(r * s, d)
    o_ref[...] = _post_norm_residual(x, y, gpost_ref[...], gt_ref[...]).reshape(r, s, d)


def _lru_out(hf, pf, cf, hb, pb, cb, sg, x5, gt, gpost, w, rb):
    bsz, r, s, njd = x5.shape
    d = w.shape[1]
    nj = njd // d
    nr = r // rb
    tm = rb * s
    width = hf.shape[-1]
    xio = pl.BlockSpec((None, rb, s, d), lambda b, j, i: (b, i, 0, j))
    row = pl.BlockSpec((None, tm, width), lambda b, j, i: (b, j * nr + i, 0))
    segs = pl.BlockSpec((None, s, width), lambda b, j, i: (b, 0, 0))
    vec = pl.BlockSpec((None, 1, d), lambda b, j, i: (b, 0, 0))
    full = lambda a: pl.BlockSpec(a.shape, lambda b, j, i: (0,) * a.ndim)
    return pl.pallas_call(
        _lru_out_kernel,
        grid=(bsz, nj, nr),
        in_specs=[row, row, segs, row, row, segs, row, xio, vec, full(gpost), full(w)],
        out_specs=xio,
        out_shape=jax.ShapeDtypeStruct(x5.shape, F32),
        compiler_params=_cparams("parallel", "parallel", "parallel"),
        name="lru_out",
    )(hf, pf, cf, hb, pb, cb, sg, x5, gt, gpost, w)


def _lru_scans(x5, sh, scl, gpre, w_in, conv_w, conv_b, p_f, p_b, h0_f, h0_b):
    r = x5.shape[1]
    z, sg = _lru_in(x5, sh, scl, gpre, w_in)
    tb = min(r, 64)
    res = []
    for params, h0, reverse in ((p_f, h0_f, False), (p_b, h0_b, True)):
        hloc, ploc, hfin, pfin = _lru_scan(z, conv_w, conv_b, *params, tb, reverse)
        carry, final = _lru_carry(h0, hfin, pfin, reverse)
        res.append((hloc, ploc, carry, final))
    return sg, res


def kernel(x, c, ctx, c_ctx, ada_w, ada_b, norm_pre, norm_post, gla_w_in, gla_wg_f, gla_bg_f, gla_wg_b, gla_bg_b, gla_norm, gla_w_out, lru_w_in, lru_conv_w, lru_conv_b, lru_wa_f, lru_ba_f, lru_wx_f, lru_bx_f, lru_lam_f, lru_wa_b, lru_ba_b, lru_wx_b, lru_bx_b, lru_lam_b, lru_w_out):
    bsz, seq, d = x.shape
    ctx_len = ctx.shape[1]
    rows = seq // GRID_W

    cvec = jnp.concatenate([c, c_ctx[None], jnp.zeros((8 - bsz - 1, d), F32)], axis=0)
    mod = _ada_modulation(cvec, ada_w, ada_b)

    def mods(i):
        lat = [mod[i, :bsz, None, j * d:(j + 1) * d] for j in range(3)]
        con = [jnp.broadcast_to(mod[i, bsz, None, None, j * d:(j + 1) * d], (bsz, 1, d)) for j in range(3)]
        return lat, con

    (sh, scl, gt), (sh_c, scl_c, gt_c) = mods(0)
    gpre, gpost = norm_pre[0][None], norm_post[0][None]
    lr_pad = 128 - 2 * GLA_GATE_RANK
    w_in = jnp.pad(gla_w_in[0], ((0, 0), (0, lr_pad))).astype(BF16)
    wg = jnp.zeros((128, 2 * GLA_DK), F32)
    wg = wg.at[:GLA_GATE_RANK, :GLA_DK].set(gla_wg_f[0])
    wg = wg.at[GLA_GATE_RANK:2 * GLA_GATE_RANK, GLA_DK:].set(gla_wg_b[0]).astype(BF16)
    bg = jnp.concatenate([gla_bg_f[0], gla_bg_b[0]])[None]
    gla_args = (gpre, gpost, w_in, wg, bg, gla_norm[0][None], gla_w_out[0].astype(BF16))
    s0 = jnp.zeros((bsz, GLA_HEADS, GLA_HEAD_V, GLA_HEAD_K), F32)
    ctx, s_f, s_b = _gla_layer(ctx, sh_c, scl_c, gt_c, *gla_args, s0, s0)
    x, _, _ = _gla_layer(x, sh, scl, gt, *gla_args, s_f, s_b)

    (sh, scl, gt), (sh_c, scl_c, _) = mods(1)
    gpre, gpost = norm_pre[1][None], norm_post[1][None]
    vec = lambda a: a[None]
    p_f = (lru_wa_f[0].astype(BF16), vec(lru_ba_f[0]), lru_wx_f[0].astype(BF16), vec(lru_bx_f[0]), vec(lru_lam_f[0]))
    p_b = (lru_wa_b[0].astype(BF16), vec(lru_ba_b[0]), lru_wx_b[0].astype(BF16), vec(lru_bx_b[0]), vec(lru_lam_b[0]))
    scan_args = (gpre, lru_w_in[0].astype(BF16), lru_conv_w[0], vec(lru_conv_b[0]), p_f, p_b)
    h0 = jnp.zeros((bsz, 1, LRU_WIDTH), F32)
    cstep = ctx_len // N_SEG
    ctx5 = ctx.reshape(bsz, N_SEG, cstep, d).transpose(0, 2, 1, 3).reshape(bsz, cstep, N_SEG, d)
    _, ((_, _, _, s_f), (_, _, _, s_b)) = _lru_scans(ctx5, sh_c, scl_c, *scan_args, h0, h0)
    x5 = x.reshape(bsz, rows, N_SEG, (GRID_W // N_SEG) * d)
    sg, ((hf, pf, cf, _), (hb, pb, cb, _)) = _lru_scans(x5, sh, scl, *scan_args, s_f, s_b)
    out5 = _lru_out(hf, pf, cf, hb, pb, cb, sg, x5, gt, gpost, lru_w_out[0].astype(BF16), rb=64)
    return out5.reshape(bsz, seq, d)
```

```python
import functools

import jax
import jax.numpy as jnp
from jax import lax
from jax.experimental import pallas as pl
from jax.experimental.pallas import tpu as pltpu

F32 = jnp.float32
BF16 = jnp.bfloat16

RMS_EPS = 1e-6
GRID_W = 64
GLA_HEADS = 4
GLA_HEAD_K = 128
GLA_HEAD_V = 256
GLA_DK = GLA_HEADS * GLA_HEAD_K
GLA_DV = GLA_HEADS * GLA_HEAD_V
GLA_GATE_RANK = 16
GLA_GATE_NORM = 16.0
GLA_CHUNK = 128
LRU_BLOCKS = 5
LRU_BLOCK_W = 256
LRU_WIDTH = LRU_BLOCKS * LRU_BLOCK_W
LRU_C = 8.0
CONV_W = 4
CONV_LEFT = CONV_W // 2
N_SEG = 8

VMEM_LIMIT_BYTES = 56 * 1024 * 1024


def _cparams(*sem):
    return pltpu.CompilerParams(dimension_semantics=sem, vmem_limit_bytes=VMEM_LIMIT_BYTES)


def _silu(x):
    hx = 0.5 * x
    return hx * jnp.tanh(hx) + hx


def _sqrt(x):
    return x * lax.rsqrt(jnp.maximum(x, 1e-30))


def _log_sigmoid(x):
    return jnp.minimum(x, 0.0) - jnp.log1p(jnp.exp(-jnp.abs(x)))


def _softplus(x):
    return jnp.maximum(x, 0.0) + jnp.log1p(jnp.exp(-jnp.abs(x)))


def _split_bf16(x):
    hi = x.astype(BF16)
    lo = (x - hi.astype(F32)).astype(BF16)
    return hi, lo


def _dot(a, b):
    return jnp.dot(a, b, preferred_element_type=F32)


def _dot_x3(a, b):
    a_hi, a_lo = _split_bf16(a)
    b_hi, b_lo = _split_bf16(b)
    return _dot(a_hi, b_hi) + (_dot(a_hi, b_lo) + _dot(a_lo, b_hi))


def _pre_norm_modulate(x, gpre, sh, scl):
    ms = jnp.mean(x * x, axis=-1, keepdims=True)
    h = x * lax.rsqrt(ms + RMS_EPS) * gpre
    return h * (1.0 + scl) + sh


def _post_norm_residual(x, y, gpost, gt):
    ms = jnp.mean(y * y, axis=-1, keepdims=True)
    return x + gt * (y * lax.rsqrt(ms + RMS_EPS) * gpost)


def _ada_kernel(c_ref, w_ref, b_ref, o_ref):
    sc = _silu(c_ref[...])
    o_ref[...] = _dot_x3(sc, w_ref[...]) + b_ref[...]


def _ada_modulation(cvec, ada_w, ada_b):
    depth, d, n3 = ada_w.shape
    tn = 1024
    return pl.pallas_call(
        _ada_kernel,
        grid=(depth, n3 // tn),
        in_specs=[pl.BlockSpec((8, d), lambda i, j: (0, 0)),
                  pl.BlockSpec((None, d, tn), lambda i, j: (i, 0, j)),
                  pl.BlockSpec((None, 1, tn), lambda i, j: (i, 0, j))],
        out_specs=pl.BlockSpec((None, 8, tn), lambda i, j: (i, 0, j)),
        out_shape=jax.ShapeDtypeStruct((depth, 8, n3), F32),
        compiler_params=_cparams("parallel", "parallel"),
        name="ada_modulation",
    )(cvec, ada_w, ada_b.reshape(depth, 1, n3))


def _gla_in_kernel(x_ref, sh_ref, scl_ref, gpre_ref, w_ref, wg_ref, bg_ref,
                   q_ref, k_ref, v_ref, sg_ref, cumf_ref, cumb_ref):
    tm = x_ref.shape[0]
    h = _pre_norm_modulate(x_ref[...], gpre_ref[...], sh_ref[...], scl_ref[...])
    proj = _dot(h.astype(BF16), w_ref[...])
    q_ref[...] = (proj[:, :GLA_DK] * (GLA_HEAD_K ** -0.5)).astype(BF16)
    k_ref[...] = proj[:, GLA_DK:2 * GLA_DK].astype(BF16)
    v_ref[...] = proj[:, 2 * GLA_DK:2 * GLA_DK + GLA_DV].astype(BF16)
    sg_ref[...] = _silu(proj[:, 2 * GLA_DK + GLA_DV:2 * GLA_DK + 2 * GLA_DV]).astype(BF16)
    lr = proj[:, 2 * GLA_DK + 2 * GLA_DV:]
    z = _dot(lr.astype(BF16), wg_ref[...]) + bg_ref[...]
    log_a = _log_sigmoid(z) * (1.0 / GLA_GATE_NORM)
    c = GLA_CHUNK
    row = lax.broadcasted_iota(jnp.int32, (c, 2 * c), 0)
    col = lax.broadcasted_iota(jnp.int32, (c, 2 * c), 1) % c
    lower = (row >= col).astype(BF16)
    upper = (row <= col).astype(BF16)
    for i in range(tm // c):
        hi, lo = _split_bf16(log_a[i * c:(i + 1) * c])
        hilo = jnp.concatenate([hi, lo], axis=0)
        cumf_ref[i * c:(i + 1) * c, :] = _dot(lower, hilo[:, :GLA_DK])
        cumb_ref[i * c:(i + 1) * c, :] = _dot(upper, hilo[:, GLA_DK:])


def _gla_in(x, sh, scl, gpre, w, wg, bg, tm):
    bsz, t, d = x.shape
    row = lambda wd: pl.BlockSpec((None, tm, wd), lambda b, i: (b, i, 0))
    vec = lambda wd: pl.BlockSpec((None, 1, wd), lambda b, i: (b, 0, 0))
    full = lambda a: pl.BlockSpec(a.shape, lambda b, i: (0,) * a.ndim)
    return pl.pallas_call(
        _gla_in_kernel,
        grid=(bsz, t // tm),
        in_specs=[row(d), vec(d), vec(d), full(gpre), full(w), full(wg), full(bg)],
        out_specs=[row(GLA_DK), row(GLA_DK), row(GLA_DV), row(GLA_DV), row(GLA_DK), row(GLA_DK)],
        out_shape=[jax.ShapeDtypeStruct((bsz, t, GLA_DK), BF16),
                   jax.ShapeDtypeStruct((bsz, t, GLA_DK), BF16),
                   jax.ShapeDtypeStruct((bsz, t, GLA_DV), BF16),
                   jax.ShapeDtypeStruct((bsz, t, GLA_DV), BF16),
                   jax.ShapeDtypeStruct((bsz, t, GLA_DK), F32),
                   jax.ShapeDtypeStruct((bsz, t, GLA_DK), F32)],
        compiler_params=_cparams("parallel", "parallel"),
        name="gla_in",
    )(x, sh, scl, gpre, w, wg, bg)


def _gla_scan_kernel(q_ref, k_ref, v_ref, cum_ref, s0_ref, o_ref, sfin_ref, s_ref, *, reverse):
    n = pl.program_id(1)
    tb = q_ref.shape[0]
    c = GLA_CHUNK

    @pl.when(n == 0)
    def _():
        s_ref[...] = s0_ref[...]

    row = lax.broadcasted_iota(jnp.int32, (c, c), 0)
    col = lax.broadcasted_iota(jnp.int32, (c, c), 1)
    keep = (row <= col) if reverse else (row >= col)
    chunks = range(tb // c)
    for i in (reversed(chunks) if reverse else chunks):
        rows = slice(i * c, (i + 1) * c)
        for hd in range(GLA_HEADS):
            kc = slice(hd * GLA_HEAD_K, (hd + 1) * GLA_HEAD_K)
            vc = slice(hd * GLA_HEAD_V, (hd + 1) * GLA_HEAD_V)
            cum = cum_ref[rows, kc]
            last = cum[0:1] if reverse else cum[c - 1:c]
            q = q_ref[rows, kc].astype(F32)
            k = k_ref[rows, kc].astype(F32)
            v = v_ref[rows, vc]
            qe = (q * jnp.exp(cum)).astype(BF16)
            ke = (k * jnp.exp(-cum)).astype(BF16)
            kl = (k * jnp.exp(last - cum)).astype(BF16)
            st = s_ref[hd]
            scores = lax.dot_general(qe, ke, (((1,), (1,)), ((), ())), preferred_element_type=F32)
            scores = jnp.where(keep, scores, 0.0).astype(BF16)
            o = lax.dot_general(qe, st.astype(BF16), (((1,), (1,)), ((), ())),
                                preferred_element_type=F32)
            o = o + _dot(scores, v)
            o_ref[rows, vc] = o.astype(BF16)
            upd = lax.dot_general(v, kl, (((0,), (0,)), ((), ())), preferred_element_type=F32)
            s_ref[hd] = st * jnp.exp(last) + upd

    @pl.when(n == pl.num_programs(1) - 1)
    def _():
        sfin_ref[...] = s_ref[...]


def _gla_scan(q, k, v, cum, s0, tb, reverse):
    bsz, t, _ = q.shape
    nblk = t // tb
    blk = (lambda b, n: (b, nblk - 1 - n, 0)) if reverse else (lambda b, n: (b, n, 0))
    row = lambda wd: pl.BlockSpec((None, tb, wd), blk)
    st = pl.BlockSpec((None, GLA_HEADS, GLA_HEAD_V, GLA_HEAD_K), lambda b, n: (b, 0, 0, 0))
    return pl.pallas_call(
        functools.partial(_gla_scan_kernel, reverse=reverse),
        grid=(bsz, nblk),
        in_specs=[row(GLA_DK), row(GLA_DK), row(GLA_DV), row(GLA_DK), st],
        out_specs=[row(GLA_DV), st],
        out_shape=[jax.ShapeDtypeStruct((bsz, t, GLA_DV), BF16),
                   jax.ShapeDtypeStruct(s0.shape, F32)],
        scratch_shapes=[pltpu.VMEM((GLA_HEADS, GLA_HEAD_V, GLA_HEAD_K), F32)],
        compiler_params=_cparams("parallel", "arbitrary"),
        name="gla_scan_bwd" if reverse else "gla_scan_fwd",
    )(q, k, v, cum, s0)


def _gla_out_kernel(of_ref, ob_ref, sg_ref, x_ref, gt_ref, gn_ref, gpost_ref, w_ref, o_ref):
    o = of_ref[...].astype(F32) + ob_ref[...].astype(F32)
    gn = gn_ref[...]
    parts = []
    for hd in range(GLA_HEADS):
        oh = o[:, hd * GLA_HEAD_V:(hd + 1) * GLA_HEAD_V]
        ms = jnp.mean(oh * oh, axis=-1, keepdims=True)
        parts.append(oh * lax.rsqrt(ms + RMS_EPS) * gn)
    on = jnp.concatenate(parts, axis=-1) * sg_ref[...].astype(F32)
    y = _dot(on.astype(BF16), w_ref[...])
    o_ref[...] = _post_norm_residual(x_ref[...], y, gpost_ref[...], gt_ref[...])


def _gla_out(o_f, o_b, sg, x, gt, gn, gpost, w, tm):
    bsz, t, d = x.shape
    row = lambda wd: pl.BlockSpec((None, tm, wd), lambda b, i: (b, i, 0))
    vec = lambda wd: pl.BlockSpec((None, 1, wd), lambda b, i: (b, 0, 0))
    full = lambda a: pl.BlockSpec(a.shape, lambda b, i: (0,) * a.ndim)
    return pl.pallas_call(
        _gla_out_kernel,
        grid=(bsz, t // tm),
        in_specs=[row(GLA_DV), row(GLA_DV), row(GLA_DV), row(d), vec(d), full(gn), full(gpost), full(w)],
        out_specs=row(d),
        out_shape=jax.ShapeDtypeStruct((bsz, t, d), F32),
        compiler_params=_cparams("parallel", "parallel"),
        name="gla_out",
    )(o_f, o_b, sg, x, gt, gn, gpost, w)


def _gla_layer(x, sh, scl, gt, gpre, gpost, w_in, wg, bg, gn, w_out, s0_f, s0_b):
    t = x.shape[1]
    tm = min(512, t)
    q, k, v, sg, cum_f, cum_b = _gla_in(x, sh, scl, gpre, w_in, wg, bg, tm)
    o_f, s_f = _gla_scan(q, k, v, cum_f, s0_f, tm, reverse=False)
    o_b, s_b = _gla_scan(q, k, v, cum_b, s0_b, tm, reverse=True)
    x_new = _gla_out(o_f, o_b, sg, x, gt, gn, gpost, w_out, tm)
    return x_new, s_f, s_b


def _lru_scan_kernel(x_ref, xp_ref, xn_ref, sh_ref, scl_ref, gpre_ref, w_ref, cw_ref, cb_ref,
                     waf_ref, baf_ref, wxf_ref, bxf_ref, lamf_ref,
                     wab_ref, bab_ref, wxb_ref, bxb_ref, lamb_ref, h0_ref,
                     hsum_ref, pb_ref, sg_ref, hbfin_ref, pbfin_ref, ffin_ref,
                     zext_s, af_s, uf_s, ab_s, ub_s, hf_s, pf_s, hb_s, pbk_s, carry_s, cin_s):
    g = pl.program_id(1)
    ng = pl.num_programs(1)
    r, s, d = x_ref.shape
    rows = r * s
    w = LRU_WIDTH
    halo_l = CONV_LEFT * s
    n_right = CONV_W - 1 - CONV_LEFT

    @pl.when(g == 0)
    def _():
        cin_s[...] = h0_ref[...]

    sh, scl, gpre = sh_ref[...], scl_ref[...], gpre_ref[...]
    hb16 = _pre_norm_modulate(x_ref[...].reshape(rows, d), gpre, sh, scl).astype(BF16)
    zext_s[halo_l:halo_l + rows, :] = _dot(hb16, w_ref[:, :w])
    sg_ref[...] = _silu(_dot(hb16, w_ref[:, w:])).astype(BF16)

    xh = jnp.concatenate([xp_ref[...].reshape(halo_l, d), xn_ref[...].reshape(n_right * s, d)], axis=0)
    zh = _dot(_pre_norm_modulate(xh, gpre, sh, scl).astype(BF16), w_ref[:, :w])
    sub = lax.broadcasted_iota(jnp.int32, (s, w), 0)
    for j in range(CONV_LEFT):
        src = halo_l + (r - CONV_LEFT + j) * s
        inner = pltpu.roll(zext_s[src:src + s, :], 1, 0)
        edge = jnp.where(g == 0, 0.0, pltpu.roll(zh[j * s:(j + 1) * s], 1, 0))
        zext_s[j * s:(j + 1) * s, :] = jnp.where(sub == 0, edge, inner)
    for j in range(n_right):
        src = halo_l + j * s
        inner = pltpu.roll(zext_s[src:src + s, :], s - 1, 0)
        edge = jnp.where(g == ng - 1, 0.0, pltpu.roll(zh[halo_l + j * s:halo_l + (j + 1) * s], s - 1, 0))
        dst = halo_l + rows + j * s
        zext_s[dst:dst + s, :] = jnp.where(sub == s - 1, edge, inner)

    for nb in range(LRU_BLOCKS):
        cs = slice(nb * LRU_BLOCK_W, (nb + 1) * LRU_BLOCK_W)
        zc = cb_ref[:, cs]
        for j in range(CONV_W):
            zc = zc + cw_ref[j:j + 1, cs] * zext_s[j * s:j * s + rows, cs]
        zcb = zc.astype(BF16)
        hzc = 0.5 * zc
        for wa_ref, ba_ref, wx_ref, bx_ref, lam_ref, a_s, u_s in (
                (waf_ref, baf_ref, wxf_ref, bxf_ref, lamf_ref, af_s, uf_s),
                (wab_ref, bab_ref, wxb_ref, bxb_ref, lamb_ref, ab_s, ub_s)):
            tr = jnp.tanh(_dot(zcb, wa_ref[nb]) + ba_ref[:, cs])
            ti = jnp.tanh(_dot(zcb, wx_ref[nb]) + bx_ref[:, cs])
            c1 = (-0.5 * LRU_C) * _softplus(-lam_ref[:, cs])
            a = jnp.exp(c1 * tr + c1)
            a_s[...] = a
            u_s[...] = _sqrt(1.0 - a * a) * (hzc * ti + hzc)

        def step(t, carry):
            hf, pf, hb, pb = carry
            rf = pl.multiple_of(t * s, s)
            rb = pl.multiple_of((r - 1 - t) * s, s)
            af = af_s[pl.ds(rf, s), :]
            ab = ab_s[pl.ds(rb, s), :]
            hf = af * hf + uf_s[pl.ds(rf, s), :]
            pf = af * pf
            hb = ab * hb + ub_s[pl.ds(rb, s), :]
            pb = ab * pb
            hf_s[pl.ds(rf, s), :] = hf
            pf_s[pl.ds(rf, s), :] = pf
            hb_s[pl.ds(rb, s), :] = hb
            pbk_s[pl.ds(rb, s), :] = pb
            return hf, pf, hb, pb

        zero = jnp.zeros((s, LRU_BLOCK_W), F32)
        one = jnp.ones((s, LRU_BLOCK_W), F32)
        hf, pf, hb, pb = lax.fori_loop(0, r, step, (zero, one, zero, one), unroll=8)

        cin = cin_s[:, cs]
        for c in range(s):
            carry_s[c:c + 1, :] = cin
            cin = hf[c:c + 1] + pf[c:c + 1] * cin
        cin_s[:, cs] = cin
        carry = carry_s[...]
        hsum = (hf_s[...].reshape(r, s, LRU_BLOCK_W) + pf_s[...].reshape(r, s, LRU_BLOCK_W) * carry[None]
                + hb_s[...].reshape(r, s, LRU_BLOCK_W))
        hsum_ref[:, cs] = hsum.reshape(rows, LRU_BLOCK_W).astype(BF16)
        pb_ref[:, cs] = pbk_s[...].astype(BF16)
        hbfin_ref[:, cs] = hb
        pbfin_ref[:, cs] = pb

    ffin_ref[...] = cin_s[...]


def _lru_scan(x5, sh, scl, gpre, w_in, conv_w, conv_b, p_f, p_b, h0_f):
    bsz, r, ng, s, d = x5.shape
    rows = r * s
    w = LRU_WIDTH
    n_right = CONV_W - 1 - CONV_LEFT
    once = pl.Buffered(1)
    cur = pl.BlockSpec((None, r, None, s, d), lambda b, g: (b, 0, g, 0, 0))
    prev = pl.BlockSpec((None, CONV_LEFT, None, s, d),
                        lambda b, g: (b, r // CONV_LEFT - 1, jnp.maximum(g - 1, 0), 0, 0))
    nxt = pl.BlockSpec((None, n_right, None, s, d), lambda b, g: (b, 0, jnp.minimum(g + 1, ng - 1), 0, 0))
    vec = lambda wd: pl.BlockSpec((None, 1, wd), lambda b, g: (b, 0, 0))
    full = lambda a: pl.BlockSpec(a.shape, lambda b, g: (0,) * a.ndim, pipeline_mode=once)
    row = pl.BlockSpec((None, rows, w), lambda b, g: (b, g, 0))
    fin = pl.BlockSpec((None, None, s, w), lambda b, g: (b, g, 0, 0))
    blk = lambda: pltpu.VMEM((rows, LRU_BLOCK_W), F32)
    return pl.pallas_call(
        _lru_scan_kernel,
        grid=(bsz, ng),
        in_specs=[cur, prev, nxt, vec(d), vec(d), full(gpre), full(w_in), full(conv_w), full(conv_b)]
                 + [full(a) for a in p_f] + [full(a) for a in p_b] + [vec(w)],
        out_specs=[row, row, row, fin, fin, vec(w)],
        out_shape=[jax.ShapeDtypeStruct((bsz, ng * rows, w), BF16),
                   jax.ShapeDtypeStruct((bsz, ng * rows, w), BF16),
                   jax.ShapeDtypeStruct((bsz, ng * rows, w), BF16),
                   jax.ShapeDtypeStruct((bsz, ng, s, w), F32),
                   jax.ShapeDtypeStruct((bsz, ng, s, w), F32),
                   jax.ShapeDtypeStruct((bsz, 1, w), F32)],
        scratch_shapes=[pltpu.VMEM((rows + (CONV_W - 1) * s, w), F32)] + [blk() for _ in range(8)]
                       + [pltpu.VMEM((s, LRU_BLOCK_W), F32), pltpu.VMEM((1, w), F32)],
        compiler_params=_cparams("parallel", "arbitrary"),
        name="lru_scan",
    )(x5, x5, x5, sh, scl, gpre, w_in, conv_w, conv_b, *p_f, *p_b, h0_f)


def _lru_carry_kernel(h0_ref, hfin_ref, pfin_ref, carry_ref, final_ref):
    c = h0_ref[...]
    for k in range(hfin_ref.shape[0] - 1, -1, -1):
        carry_ref[k:k + 1, :] = c
        c = hfin_ref[k:k + 1, :] + pfin_ref[k:k + 1, :] * c
    final_ref[...] = c


def _lru_carry(h0, hfin, pfin):
    bsz, ng, s, w = hfin.shape
    vec = pl.BlockSpec((None, 1, w), lambda b: (b, 0, 0))
    runs = pl.BlockSpec((None, ng * s, w), lambda b: (b, 0, 0))
    carry, final = pl.pallas_call(
        _lru_carry_kernel,
        grid=(bsz,),
        in_specs=[vec, runs, runs],
        out_specs=[runs, vec],
        out_shape=[jax.ShapeDtypeStruct((bsz, ng * s, w), F32), jax.ShapeDtypeStruct((bsz, 1, w), F32)],
        compiler_params=_cparams("parallel"),
        name="lru_carry",
    )(h0, hfin.reshape(bsz, ng * s, w), pfin.reshape(bsz, ng * s, w))
    return carry.reshape(bsz, ng, s, w), final


def _lru_out_kernel(hsum_ref, pb_ref, cb_ref, sg_ref, x_ref, gt_ref, gpost_ref, w_ref, o_ref):
    r, s, d = x_ref.shape
    w = hsum_ref.shape[-1]
    h = hsum_ref[...].astype(F32).reshape(r, s, w) + pb_ref[...].astype(F32).reshape(r, s, w) * cb_ref[...][None]
    y = _dot((h.reshape(r * s, w) * sg_ref[...].astype(F32)).astype(BF16), w_ref[...])
    x = x_ref[...].reshape(r * s, d)
    o_ref[...] = _post_norm_residual(x, y, gpost_ref[...], gt_ref[...]).reshape(r, s, d)


def _lru_out(hsum, pb, cb, sg, x5, gt, gpost, w, rb):
    bsz, r, ng, s, d = x5.shape
    nr = r // rb
    tm = rb * s
    width = hsum.shape[-1]
    xio = pl.BlockSpec((None, rb, None, s, d), lambda b, g, i: (b, i, g, 0, 0))
    row = pl.BlockSpec((None, tm, width), lambda b, g, i: (b, g * nr + i, 0))
    runs = pl.BlockSpec((None, None, s, width), lambda b, g, i: (b, g, 0, 0))
    vec = pl.BlockSpec((None, 1, d), lambda b, g, i: (b, 0, 0))
    full = lambda a: pl.BlockSpec(a.shape, lambda b, g, i: (0,) * a.ndim)
    return pl.pallas_call(
        _lru_out_kernel,
        grid=(bsz, ng, nr),
        in_specs=[row, row, runs, row, xio, vec, full(gpost), full(w)],
        out_specs=xio,
        out_shape=jax.ShapeDtypeStruct(x5.shape, F32),
        compiler_params=_cparams("parallel", "parallel", "parallel"),
        name="lru_out",
    )(hsum, pb, cb, sg, x5, gt, gpost, w)


def kernel(x, c, ctx, c_ctx, ada_w, ada_b, norm_pre, norm_post, gla_w_in, gla_wg_f, gla_bg_f, gla_wg_b, gla_bg_b, gla_norm, gla_w_out, lru_w_in, lru_conv_w, lru_conv_b, lru_wa_f, lru_ba_f, lru_wx_f, lru_bx_f, lru_lam_f, lru_wa_b, lru_ba_b, lru_wx_b, lru_bx_b, lru_lam_b, lru_w_out):
    bsz, seq, d = x.shape
    ctx_len = ctx.shape[1]
    rows = seq // GRID_W

    cvec = jnp.concatenate([c, c_ctx[None], jnp.zeros((8 - bsz - 1, d), F32)], axis=0)
    mod = _ada_modulation(cvec, ada_w, ada_b)

    def mods(i):
        lat = [mod[i, :bsz, None, j * d:(j + 1) * d] for j in range(3)]
        con = [jnp.broadcast_to(mod[i, bsz, None, None, j * d:(j + 1) * d], (bsz, 1, d)) for j in range(3)]
        return lat, con

    (sh, scl, gt), (sh_c, scl_c, gt_c) = mods(0)
    gpre, gpost = norm_pre[0][None], norm_post[0][None]
    lr_pad = 128 - 2 * GLA_GATE_RANK
    w_in = jnp.pad(gla_w_in[0], ((0, 0), (0, lr_pad))).astype(BF16)
    wg = jnp.zeros((128, 2 * GLA_DK), F32)
    wg = wg.at[:GLA_GATE_RANK, :GLA_DK].set(gla_wg_f[0])
    wg = wg.at[GLA_GATE_RANK:2 * GLA_GATE_RANK, GLA_DK:].set(gla_wg_b[0]).astype(BF16)
    bg = jnp.concatenate([gla_bg_f[0], gla_bg_b[0]])[None]
    gla_args = (gpre, gpost, w_in, wg, bg, gla_norm[0][None], gla_w_out[0].astype(BF16))
    s0 = jnp.zeros((bsz, GLA_HEADS, GLA_HEAD_V, GLA_HEAD_K), F32)
    ctx, s_f, s_b = _gla_layer(ctx, sh_c, scl_c, gt_c, *gla_args, s0, s0)
    x, _, _ = _gla_layer(x, sh, scl, gt, *gla_args, s_f, s_b)

    (sh, scl, gt), (sh_c, scl_c, _) = mods(1)
    gpre, gpost = norm_pre[1][None], norm_post[1][None]
    vec = lambda a: a[None]
    half = lambda wgt, bias: ((0.5 * wgt).astype(BF16), vec(0.5 * bias))
    p_f = (*half(lru_wa_f[0], lru_ba_f[0]), *half(lru_wx_f[0], lru_bx_f[0]), vec(lru_lam_f[0]))
    p_b = (*half(lru_wa_b[0], lru_ba_b[0]), *half(lru_wx_b[0], lru_bx_b[0]), vec(lru_lam_b[0]))
    scan_args = (gpre, lru_w_in[0].astype(BF16), lru_conv_w[0], vec(lru_conv_b[0]), p_f, p_b)
    h0 = jnp.zeros((bsz, 1, LRU_WIDTH), F32)
    cstep = ctx_len // N_SEG
    ctx5 = ctx.reshape(bsz, N_SEG, cstep, d).transpose(0, 2, 1, 3).reshape(bsz, cstep, 1, N_SEG, d)
    _, _, _, hbfin, pbfin, s_f = _lru_scan(ctx5, sh_c, scl_c, *scan_args, h0)
    _, s_b = _lru_carry(h0, hbfin, pbfin)
    x5 = x.reshape(bsz, rows, GRID_W // N_SEG, N_SEG, d)
    hsum, pb, sg, hbfin, pbfin, _ = _lru_scan(x5, sh, scl, *scan_args, s_f)
    cb, _ = _lru_carry(s_b, hbfin, pbfin)
    out5 = _lru_out(hsum, pb, cb, sg, x5, gt, gpost, lru_w_out[0].astype(BF16), rb=64)
    return out5.reshape(bsz, seq, d)
```

```python
import functools

import jax
import jax.numpy as jnp
from jax import lax
from jax.experimental import pallas as pl
from jax.experimental.pallas import tpu as pltpu

F32 = jnp.float32
BF16 = jnp.bfloat16

RMS_EPS = 1e-6
GRID_W = 64
GLA_HEADS = 4
GLA_HEAD_K = 128
GLA_HEAD_V = 256
GLA_DK = GLA_HEADS * GLA_HEAD_K
GLA_DV = GLA_HEADS * GLA_HEAD_V
GLA_GATE_RANK = 16
GLA_GATE_NORM = 16.0
GLA_CHUNK = 128
GLA_FACTORISED_MIN_CUM = -60.0
LRU_BLOCKS = 5
LRU_BLOCK_W = 256
LRU_WIDTH = LRU_BLOCKS * LRU_BLOCK_W
LRU_C = 8.0
CONV_W = 4
CONV_LEFT = CONV_W // 2
N_SEG = 8

VMEM_LIMIT_BYTES = 56 * 1024 * 1024


def _cparams(*sem):
    return pltpu.CompilerParams(dimension_semantics=sem, vmem_limit_bytes=VMEM_LIMIT_BYTES)


def _silu(x):
    hx = 0.5 * x
    return hx * jnp.tanh(hx) + hx


def _sqrt(x):
    return x * lax.rsqrt(jnp.maximum(x, 1e-30))


def _log_sigmoid(x):
    return jnp.minimum(x, 0.0) - jnp.log1p(jnp.exp(-jnp.abs(x)))


def _softplus(x):
    return jnp.maximum(x, 0.0) + jnp.log1p(jnp.exp(-jnp.abs(x)))


def _split_bf16(x):
    hi = x.astype(BF16)
    lo = (x - hi.astype(F32)).astype(BF16)
    return hi, lo


def _dot(a, b):
    return jnp.dot(a, b, preferred_element_type=F32)


def _dot_x3(a, b):
    a_hi, a_lo = _split_bf16(a)
    b_hi, b_lo = _split_bf16(b)
    return _dot(a_hi, b_hi) + (_dot(a_hi, b_lo) + _dot(a_lo, b_hi))


def _pre_norm_modulate(x, gpre, sh, scl):
    ms = jnp.mean(x * x, axis=-1, keepdims=True)
    return (x * lax.rsqrt(ms + RMS_EPS)) * (gpre * (1.0 + scl)) + sh


def _post_norm_residual(x, y, gpost, gt):
    ms = jnp.mean(y * y, axis=-1, keepdims=True)
    return x + gt * (y * lax.rsqrt(ms + RMS_EPS) * gpost)


def _ada_kernel(c_ref, w_ref, b_ref, o_ref):
    sc = _silu(c_ref[...])
    o_ref[...] = _dot_x3(sc, w_ref[...]) + b_ref[...]


def _ada_modulation(cvec, ada_w, ada_b):
    depth, d, n3 = ada_w.shape
    tn = 1024
    return pl.pallas_call(
        _ada_kernel,
        grid=(depth, n3 // tn),
        in_specs=[pl.BlockSpec((8, d), lambda i, j: (0, 0)),
                  pl.BlockSpec((None, d, tn), lambda i, j: (i, 0, j)),
                  pl.BlockSpec((None, 1, tn), lambda i, j: (i, 0, j))],
        out_specs=pl.BlockSpec((None, 8, tn), lambda i, j: (i, 0, j)),
        out_shape=jax.ShapeDtypeStruct((depth, 8, n3), F32),
        compiler_params=_cparams("parallel", "parallel"),
        name="ada_modulation",
    )(cvec, ada_w, ada_b.reshape(depth, 1, n3))


def _gla_in_kernel(x_ref, sh_ref, scl_ref, gpre_ref, w_ref, wg_ref, bg_ref,
                   q_ref, k_ref, v_ref, g_ref, cumf_ref, cumb_ref):
    tm = x_ref.shape[0]
    c = GLA_CHUNK
    row = lax.broadcasted_iota(jnp.int32, (c, 2 * c), 0)
    col = lax.broadcasted_iota(jnp.int32, (c, 2 * c), 1) % c
    lower = (row >= col).astype(BF16)
    upper = (row <= col).astype(BF16)
    gpre, sh, scl = gpre_ref[...], sh_ref[...], scl_ref[...]
    hb16 = _pre_norm_modulate(x_ref[...], gpre, sh, scl).astype(BF16)

    def project(i):
        return _dot(hb16[i * c:(i + 1) * c], w_ref[...])

    nxt = project(0)
    for i in range(tm // c):
        rows = slice(i * c, (i + 1) * c)
        proj = nxt
        if i + 1 < tm // c:
            nxt = project(i + 1)
        q_ref[rows, :] = (proj[:, :GLA_DK] * (GLA_HEAD_K ** -0.5)).astype(BF16)
        k_ref[rows, :] = proj[:, GLA_DK:2 * GLA_DK].astype(BF16)
        v_ref[rows, :] = proj[:, 2 * GLA_DK:2 * GLA_DK + GLA_DV].astype(BF16)
        g_ref[rows, :] = proj[:, 2 * GLA_DK + GLA_DV:2 * GLA_DK + 2 * GLA_DV].astype(BF16)
        lr = proj[:, 2 * GLA_DK + 2 * GLA_DV:]
        z = _dot(lr.astype(BF16), wg_ref[...]) + bg_ref[...]
        log_a = _log_sigmoid(z) * (1.0 / GLA_GATE_NORM)
        hi, lo = _split_bf16(log_a)
        hilo = jnp.concatenate([hi, lo], axis=0)
        cumf_ref[rows, :] = _dot(lower, hilo[:, :GLA_DK])
        cumb_ref[rows, :] = _dot(upper, hilo[:, GLA_DK:])


def _gla_in(x, sh, scl, gpre, w, wg, bg, tm):
    bsz, t, d = x.shape
    row = lambda wd: pl.BlockSpec((None, tm, wd), lambda b, i: (b, i, 0))
    vec = lambda wd: pl.BlockSpec((None, 1, wd), lambda b, i: (b, 0, 0))
    full = lambda a: pl.BlockSpec(a.shape, lambda b, i: (0,) * a.ndim)
    return pl.pallas_call(
        _gla_in_kernel,
        grid=(bsz, t // tm),
        in_specs=[row(d), vec(d), vec(d), full(gpre), full(w), full(wg), full(bg)],
        out_specs=[row(GLA_DK), row(GLA_DK), row(GLA_DV), row(GLA_DV), row(GLA_DK), row(GLA_DK)],
        out_shape=[jax.ShapeDtypeStruct((bsz, t, GLA_DK), BF16),
                   jax.ShapeDtypeStruct((bsz, t, GLA_DK), BF16),
                   jax.ShapeDtypeStruct((bsz, t, GLA_DV), BF16),
                   jax.ShapeDtypeStruct((bsz, t, GLA_DV), BF16),
                   jax.ShapeDtypeStruct((bsz, t, GLA_DK), F32),
                   jax.ShapeDtypeStruct((bsz, t, GLA_DK), F32)],
        compiler_params=_cparams("parallel", "parallel"),
        name="gla_in",
    )(x, sh, scl, gpre, w, wg, bg)


def _gla_scores_exact(q, k, cum, reverse, kf_s, cm_s, sc_s):
    c = q.shape[0]
    kf_s[...] = k
    cm_s[...] = cum
    sc_s[...] = jnp.zeros_like(sc_s)
    ridx = lax.broadcasted_iota(jnp.int32, (c, 1), 0)
    cidx = lax.broadcasted_iota(jnp.int32, (c, c), 1)

    def column(j, carry):
        kj = kf_s[pl.ds(j, 1), :]
        cj = cm_s[pl.ds(j, 1), :]
        live = (ridx <= j) if reverse else (ridx >= j)
        dec = jnp.exp(jnp.where(live, jnp.minimum(cum - cj, 0.0), -1e30))
        sj = jnp.sum(q * kj * dec, axis=-1, keepdims=True)
        sc_s[...] += jnp.where(cidx == j, sj, 0.0)
        return carry

    lax.fori_loop(0, c, column, 0)
    return sc_s[...]


def _gla_chunks(q_ref, k_ref, v_ref, cum_ref, s_ref, put_o, kf_s, cm_s, sc_s, *, reverse, factorised):
    tb = q_ref.shape[0]
    c = GLA_CHUNK
    row = lax.broadcasted_iota(jnp.int32, (c, c), 0)
    col = lax.broadcasted_iota(jnp.int32, (c, c), 1)
    keep = (row <= col) if reverse else (row >= col)
    chunks = range(tb // c)
    for i in (reversed(chunks) if reverse else chunks):
        rows = slice(i * c, (i + 1) * c)
        for hd in range(GLA_HEADS):
            kc = slice(hd * GLA_HEAD_K, (hd + 1) * GLA_HEAD_K)
            vc = slice(hd * GLA_HEAD_V, (hd + 1) * GLA_HEAD_V)
            cum = cum_ref[rows, kc]
            last = cum[0:1] if reverse else cum[c - 1:c]
            q = q_ref[rows, kc].astype(F32)
            k = k_ref[rows, kc].astype(F32)
            v = v_ref[rows, vc]
            qe = (q * jnp.exp(cum)).astype(BF16)
            kl = (k * jnp.exp(last - cum)).astype(BF16)
            st = s_ref[hd]
            if factorised:
                ke = (k * jnp.exp(-cum)).astype(BF16)
                scores = lax.dot_general(qe, ke, (((1,), (1,)), ((), ())), preferred_element_type=F32)
                scores = jnp.where(keep, scores, 0.0)
            else:
                scores = _gla_scores_exact(q, k, cum, reverse, kf_s, cm_s, sc_s)
            o = lax.dot_general(qe, st.astype(BF16), (((1,), (1,)), ((), ())),
                                preferred_element_type=F32)
            o = o + _dot(scores.astype(BF16), v)
            put_o(rows, vc, o)
            upd = lax.dot_general(v, kl, (((0,), (0,)), ((), ())), preferred_element_type=F32)
            s_ref[hd] = st * jnp.exp(last) + upd


def _gla_scan_kernel(*refs, reverse, fuse_out):
    if fuse_out:
        (q_ref, k_ref, v_ref, cum_ref, s0_ref, of_ref, g_ref, x_ref, gt_ref, gn_ref, gpost_ref, wout_ref,
         y_ref, sfin_ref, s_ref, kf_s, cm_s, sc_s, o_s) = refs
    else:
        q_ref, k_ref, v_ref, cum_ref, s0_ref, y_ref, sfin_ref, s_ref, kf_s, cm_s, sc_s = refs
    n = pl.program_id(1)

    @pl.when(n == 0)
    def _():
        s_ref[...] = s0_ref[...]

    if fuse_out:
        def put_o(rows, vc, o):
            o_s[rows, vc] = o + of_ref[rows, vc].astype(F32)
    else:
        def put_o(rows, vc, o):
            y_ref[rows, vc] = o.astype(BF16)

    c = GLA_CHUNK
    ends = [cum_ref[i * c:i * c + 1, :] if reverse else cum_ref[(i + 1) * c - 1:(i + 1) * c, :]
            for i in range(q_ref.shape[0] // c)]
    safe = jnp.min(functools.reduce(jnp.minimum, ends)) >= GLA_FACTORISED_MIN_CUM
    scan = functools.partial(_gla_chunks, q_ref, k_ref, v_ref, cum_ref, s_ref, put_o, kf_s, cm_s, sc_s,
                             reverse=reverse)
    pl.when(safe)(functools.partial(scan, factorised=True))
    pl.when(jnp.logical_not(safe))(functools.partial(scan, factorised=False))

    if fuse_out:
        gn = gn_ref[...]
        parts = []
        for hd in range(GLA_HEADS):
            oh = o_s[:, hd * GLA_HEAD_V:(hd + 1) * GLA_HEAD_V]
            ms = jnp.mean(oh * oh, axis=-1, keepdims=True)
            parts.append(oh * lax.rsqrt(ms + RMS_EPS) * gn)
        on = jnp.concatenate(parts, axis=-1) * _silu(g_ref[...].astype(F32))
        y = _dot(on.astype(BF16), wout_ref[...])
        y_ref[...] = _post_norm_residual(x_ref[...], y, gpost_ref[...], gt_ref[...])

    @pl.when(n == pl.num_programs(1) - 1)
    def _():
        sfin_ref[...] = s_ref[...]


def _gla_scan(q, k, v, cum, s0, tb, reverse, out_args=None):
    bsz, t, _ = q.shape
    nblk = t // tb
    c = GLA_CHUNK
    blk = (lambda b, n: (b, nblk - 1 - n, 0)) if reverse else (lambda b, n: (b, n, 0))
    row = lambda wd: pl.BlockSpec((None, tb, wd), blk)
    vec = lambda wd: pl.BlockSpec((None, 1, wd), lambda b, n: (b, 0, 0))
    full = lambda a: pl.BlockSpec(a.shape, lambda b, n: (0,) * a.ndim)
    st = pl.BlockSpec((None, GLA_HEADS, GLA_HEAD_V, GLA_HEAD_K), lambda b, n: (b, 0, 0, 0))
    in_specs = [row(GLA_DK), row(GLA_DK), row(GLA_DV), row(GLA_DK), st]
    scratch = [pltpu.VMEM((GLA_HEADS, GLA_HEAD_V, GLA_HEAD_K), F32),
               pltpu.VMEM((c, GLA_HEAD_K), F32), pltpu.VMEM((c, GLA_HEAD_K), F32), pltpu.VMEM((c, c), F32)]
    args = [q, k, v, cum, s0]
    if out_args is None:
        y_spec, y_shape = row(GLA_DV), jax.ShapeDtypeStruct((bsz, t, GLA_DV), BF16)
    else:
        o_other, g, x, gt, gn, gpost, w_out = out_args
        d = x.shape[-1]
        in_specs += [row(GLA_DV), row(GLA_DV), row(d), vec(d), full(gn), full(gpost), full(w_out)]
        args += [o_other, g, x, gt, gn, gpost, w_out]
        scratch.append(pltpu.VMEM((tb, GLA_DV), F32))
        y_spec, y_shape = row(d), jax.ShapeDtypeStruct((bsz, t, d), F32)
    return pl.pallas_call(
        functools.partial(_gla_scan_kernel, reverse=reverse, fuse_out=out_args is not None),
        grid=(bsz, nblk),
        in_specs=in_specs,
        out_specs=[y_spec, st],
        out_shape=[y_shape, jax.ShapeDtypeStruct(s0.shape, F32)],
        scratch_shapes=scratch,
        compiler_params=_cparams("parallel", "arbitrary"),
        name="gla_scan_bwd" if reverse else "gla_scan_fwd",
    )(*args)


def _gla_layer(x, sh, scl, gt, gpre, gpost, w_in, wg, bg, gn, w_out, s0_f, s0_b):
    t = x.shape[1]
    tm = min(512, t)
    q, k, v, g, cum_f, cum_b = _gla_in(x, sh, scl, gpre, w_in, wg, bg, tm)
    o_f, s_f = _gla_scan(q, k, v, cum_f, s0_f, tm, reverse=False)
    x_new, s_b = _gla_scan(q, k, v, cum_b, s0_b, tm, reverse=True, out_args=(o_f, g, x, gt, gn, gpost, w_out))
    return x_new, s_f, s_b


def _lru_scan_kernel(x_ref, xp_ref, xn_ref, sh_ref, scl_ref, gpre_ref, w_ref, cw_ref, cb_ref,
                     waf_ref, baf_ref, wxf_ref, bxf_ref, lamf_ref,
                     wab_ref, bab_ref, wxb_ref, bxb_ref, lamb_ref, h0_ref,
                     hsum_ref, pb_ref, g_ref, hbfin_ref, pbfin_ref, ffin_ref,
                     zext_s, hf_s, pf_s, hb_s, pbk_s, carry_s, cin_s, *au_s):
    g = pl.program_id(1)
    ng = pl.num_programs(1)
    r, s, d = x_ref.shape
    rows = r * s
    w = LRU_WIDTH
    halo_l = CONV_LEFT * s
    n_right = CONV_W - 1 - CONV_LEFT

    @pl.when(g == 0)
    def _():
        cin_s[...] = h0_ref[...]

    sh, scl, gpre = sh_ref[...], scl_ref[...], gpre_ref[...]
    hb16 = _pre_norm_modulate(x_ref[...].reshape(rows, d), gpre, sh, scl).astype(BF16)
    xh = jnp.concatenate([xp_ref[...].reshape(halo_l, d), xn_ref[...].reshape(n_right * s, d)], axis=0)
    zh = _dot(_pre_norm_modulate(xh, gpre, sh, scl).astype(BF16), w_ref[:, :w])
    sub = lax.broadcasted_iota(jnp.int32, (s, LRU_BLOCK_W), 0)

    def in_proj(nb):
        cs = slice(nb * LRU_BLOCK_W, (nb + 1) * LRU_BLOCK_W)
        zext_s[halo_l:halo_l + rows, cs] = _dot(hb16, w_ref[:, cs])
        g_ref[:, cs] = _dot(hb16, w_ref[:, w + nb * LRU_BLOCK_W:w + (nb + 1) * LRU_BLOCK_W]).astype(BF16)
        for j in range(CONV_LEFT):
            src = halo_l + (r - CONV_LEFT + j) * s
            inner = pltpu.roll(zext_s[src:src + s, cs], 1, 0)
            edge = jnp.where(g == 0, 0.0, pltpu.roll(zh[j * s:(j + 1) * s, cs], 1, 0))
            zext_s[j * s:(j + 1) * s, cs] = jnp.where(sub == 0, edge, inner)
        for j in range(n_right):
            src = halo_l + j * s
            inner = pltpu.roll(zext_s[src:src + s, cs], s - 1, 0)
            edge = jnp.where(g == ng - 1, 0.0,
                             pltpu.roll(zh[halo_l + j * s:halo_l + (j + 1) * s, cs], s - 1, 0))
            dst = halo_l + rows + j * s
            zext_s[dst:dst + s, cs] = jnp.where(sub == s - 1, edge, inner)

    in_proj(0)
    for nb in range(LRU_BLOCKS):
        cs = slice(nb * LRU_BLOCK_W, (nb + 1) * LRU_BLOCK_W)
        af_s, uf_s, ab_s, ub_s = au_s[4 * (nb % 2):4 * (nb % 2) + 4]
        zc = cb_ref[:, cs]
        for j in range(CONV_W):
            zc = zc + cw_ref[j:j + 1, cs] * zext_s[j * s:j * s + rows, cs]
        zcb = zc.astype(BF16)
        hzc = 0.5 * zc
        for wa_ref, ba_ref, wx_ref, bx_ref, lam_ref, a_s, u_s in (
                (waf_ref, baf_ref, wxf_ref, bxf_ref, lamf_ref, af_s, uf_s),
                (wab_ref, bab_ref, wxb_ref, bxb_ref, lamb_ref, ab_s, ub_s)):
            tr = jnp.tanh(_dot(zcb, wa_ref[nb]) + ba_ref[:, cs])
            ti = jnp.tanh(_dot(zcb, wx_ref[nb]) + bx_ref[:, cs])
            c1 = (-0.5 * LRU_C) * _softplus(-lam_ref[:, cs])
            a = jnp.exp(c1 * tr + c1)
            a_s[...] = a
            u_s[...] = _sqrt(1.0 - a * a) * (hzc * ti + hzc)

        if nb + 1 < LRU_BLOCKS:
            in_proj(nb + 1)

        hf = hb = jnp.zeros((s, LRU_BLOCK_W), F32)
        pf = pb = jnp.ones((s, LRU_BLOCK_W), F32)
        for t in range(r):
            rf = slice(t * s, (t + 1) * s)
            rb = slice((r - 1 - t) * s, (r - t) * s)
            af = af_s[rf, :]
            ab = ab_s[rb, :]
            hf = af * hf + uf_s[rf, :]
            pf = af * pf
            hb = ab * hb + ub_s[rb, :]
            pb = ab * pb
            hf_s[rf, :] = hf
            pf_s[rf, :] = pf
            hb_s[rb, :] = hb
            pbk_s[rb, :] = pb

        cin = cin_s[:, cs]
        for c in range(s):
            carry_s[c:c + 1, :] = cin
            cin = hf[c:c + 1] + pf[c:c + 1] * cin
        cin_s[:, cs] = cin
        carry = carry_s[...]
        hsum = (hf_s[...].reshape(r, s, LRU_BLOCK_W) + pf_s[...].reshape(r, s, LRU_BLOCK_W) * carry[None]
                + hb_s[...].reshape(r, s, LRU_BLOCK_W))
        hsum_ref[:, cs] = hsum.reshape(rows, LRU_BLOCK_W).astype(BF16)
        pb_ref[:, cs] = pbk_s[...].astype(BF16)
        hbfin_ref[:, cs] = hb
        pbfin_ref[:, cs] = pb

    ffin_ref[...] = cin_s[...]


def _lru_scan(x5, sh, scl, gpre, w_in, conv_w, conv_b, p_f, p_b, h0_f):
    bsz, r, ng, s, d = x5.shape
    rows = r * s
    w = LRU_WIDTH
    n_right = CONV_W - 1 - CONV_LEFT
    once = pl.Buffered(1)
    cur = pl.BlockSpec((None, r, None, s, d), lambda b, g: (b, 0, g, 0, 0))
    prev = pl.BlockSpec((None, CONV_LEFT, None, s, d),
                        lambda b, g: (b, r // CONV_LEFT - 1, jnp.maximum(g - 1, 0), 0, 0))
    nxt = pl.BlockSpec((None, n_right, None, s, d), lambda b, g: (b, 0, jnp.minimum(g + 1, ng - 1), 0, 0))
    vec = lambda wd: pl.BlockSpec((None, 1, wd), lambda b, g: (b, 0, 0))
    full = lambda a: pl.BlockSpec(a.shape, lambda b, g: (0,) * a.ndim, pipeline_mode=once)
    row = pl.BlockSpec((None, rows, w), lambda b, g: (b, g, 0))
    fin = pl.BlockSpec((None, None, s, w), lambda b, g: (b, g, 0, 0))
    blk = lambda: pltpu.VMEM((rows, LRU_BLOCK_W), F32)
    return pl.pallas_call(
        _lru_scan_kernel,
        grid=(bsz, ng),
        in_specs=[cur, prev, nxt, vec(d), vec(d), full(gpre), full(w_in), full(conv_w), full(conv_b)]
                 + [full(a) for a in p_f] + [full(a) for a in p_b] + [vec(w)],
        out_specs=[row, row, row, fin, fin, vec(w)],
        out_shape=[jax.ShapeDtypeStruct((bsz, ng * rows, w), BF16),
                   jax.ShapeDtypeStruct((bsz, ng * rows, w), BF16),
                   jax.ShapeDtypeStruct((bsz, ng * rows, w), BF16),
                   jax.ShapeDtypeStruct((bsz, ng, s, w), F32),
                   jax.ShapeDtypeStruct((bsz, ng, s, w), F32),
                   jax.ShapeDtypeStruct((bsz, 1, w), F32)],
        scratch_shapes=[pltpu.VMEM((rows + (CONV_W - 1) * s, w), F32)] + [blk() for _ in range(4)]
                       + [pltpu.VMEM((s, LRU_BLOCK_W), F32), pltpu.VMEM((1, w), F32)]
                       + [blk() for _ in range(8)],
        compiler_params=_cparams("parallel", "arbitrary"),
        name="lru_scan",
    )(x5, x5, x5, sh, scl, gpre, w_in, conv_w, conv_b, *p_f, *p_b, h0_f)


def _lru_carry_kernel(h0_ref, hfin_ref, pfin_ref, carry_ref, final_ref):
    c = h0_ref[...]
    for k in range(hfin_ref.shape[0] - 1, -1, -1):
        carry_ref[k:k + 1, :] = c
        c = hfin_ref[k:k + 1, :] + pfin_ref[k:k + 1, :] * c
    final_ref[...] = c


def _lru_carry(h0, hfin, pfin):
    bsz, ng, s, w = hfin.shape
    vec = pl.BlockSpec((None, 1, w), lambda b: (b, 0, 0))
    runs = pl.BlockSpec((None, ng * s, w), lambda b: (b, 0, 0))
    carry, final = pl.pallas_call(
        _lru_carry_kernel,
        grid=(bsz,),
        in_specs=[vec, runs, runs],
        out_specs=[runs, vec],
        out_shape=[jax.ShapeDtypeStruct((bsz, ng * s, w), F32), jax.ShapeDtypeStruct((bsz, 1, w), F32)],
        compiler_params=_cparams("parallel"),
        name="lru_carry",
    )(h0, hfin.reshape(bsz, ng * s, w), pfin.reshape(bsz, ng * s, w))
    return carry.reshape(bsz, ng, s, w), final


def _lru_out_kernel(hsum_ref, pb_ref, cb_ref, g_ref, x_ref, gt_ref, gpost_ref, w_ref, o_ref):
    r, s, d = x_ref.shape
    w = hsum_ref.shape[-1]
    h = hsum_ref[...].astype(F32).reshape(r, s, w) + pb_ref[...].astype(F32).reshape(r, s, w) * cb_ref[...][None]
    y = _dot((h.reshape(r * s, w) * _silu(g_ref[...].astype(F32))).astype(BF16), w_ref[...])
    x = x_ref[...].reshape(r * s, d)
    o_ref[...] = _post_norm_residual(x, y, gpost_ref[...], gt_ref[...]).reshape(r, s, d)


def _lru_out(hsum, pb, cb, g, x5, gt, gpost, w, rb):
    bsz, r, ng, s, d = x5.shape
    nr = r // rb
    tm = rb * s
    width = hsum.shape[-1]
    xio = pl.BlockSpec((None, rb, None, s, d), lambda b, g, i: (b, i, g, 0, 0))
    row = pl.BlockSpec((None, tm, width), lambda b, g, i: (b, g * nr + i, 0))
    runs = pl.BlockSpec((None, None, s, width), lambda b, g, i: (b, g, 0, 0))
    vec = pl.BlockSpec((None, 1, d), lambda b, g, i: (b, 0, 0))
    full = lambda a: pl.BlockSpec(a.shape, lambda b, g, i: (0,) * a.ndim)
    return pl.pallas_call(
        _lru_out_kernel,
        grid=(bsz, ng, nr),
        in_specs=[row, row, runs, row, xio, vec, full(gpost), full(w)],
        out_specs=xio,
        out_shape=jax.ShapeDtypeStruct(x5.shape, F32),
        compiler_params=_cparams("parallel", "parallel", "parallel"),
        name="lru_out",
    )(hsum, pb, cb, g, x5, gt, gpost, w)


def kernel(x, c, ctx, c_ctx, ada_w, ada_b, norm_pre, norm_post, gla_w_in, gla_wg_f, gla_bg_f, gla_wg_b, gla_bg_b, gla_norm, gla_w_out, lru_w_in, lru_conv_w, lru_conv_b, lru_wa_f, lru_ba_f, lru_wx_f, lru_bx_f, lru_lam_f, lru_wa_b, lru_ba_b, lru_wx_b, lru_bx_b, lru_lam_b, lru_w_out):
    bsz, seq, d = x.shape
    ctx_len = ctx.shape[1]
    rows = seq // GRID_W

    cvec = jnp.concatenate([c, c_ctx[None], jnp.zeros((8 - bsz - 1, d), F32)], axis=0)
    mod = _ada_modulation(cvec, ada_w, ada_b)

    def mods(i):
        lat = [mod[i, :bsz, None, j * d:(j + 1) * d] for j in range(3)]
        con = [jnp.broadcast_to(mod[i, bsz, None, None, j * d:(j + 1) * d], (bsz, 1, d)) for j in range(3)]
        return lat, con

    (sh, scl, gt), (sh_c, scl_c, gt_c) = mods(0)
    gpre, gpost = norm_pre[0][None], norm_post[0][None]
    lr_pad = 128 - 2 * GLA_GATE_RANK
    w_in = jnp.pad(gla_w_in[0], ((0, 0), (0, lr_pad))).astype(BF16)
    wg = jnp.zeros((128, 2 * GLA_DK), F32)
    wg = wg.at[:GLA_GATE_RANK, :GLA_DK].set(gla_wg_f[0])
    wg = wg.at[GLA_GATE_RANK:2 * GLA_GATE_RANK, GLA_DK:].set(gla_wg_b[0]).astype(BF16)
    bg = jnp.concatenate([gla_bg_f[0], gla_bg_b[0]])[None]
    gla_args = (gpre, gpost, w_in, wg, bg, gla_norm[0][None], gla_w_out[0].astype(BF16))
    s0 = jnp.zeros((bsz, GLA_HEADS, GLA_HEAD_V, GLA_HEAD_K), F32)
    ctx, s_f, s_b = _gla_layer(ctx, sh_c, scl_c, gt_c, *gla_args, s0, s0)
    x, _, _ = _gla_layer(x, sh, scl, gt, *gla_args, s_f, s_b)

    (sh, scl, gt), (sh_c, scl_c, _) = mods(1)
    gpre, gpost = norm_pre[1][None], norm_post[1][None]
    vec = lambda a: a[None]
    half = lambda wgt, bias: ((0.5 * wgt).astype(BF16), vec(0.5 * bias))
    p_f = (*half(lru_wa_f[0], lru_ba_f[0]), *half(lru_wx_f[0], lru_bx_f[0]), vec(lru_lam_f[0]))
    p_b = (*half(lru_wa_b[0], lru_ba_b[0]), *half(lru_wx_b[0], lru_bx_b[0]), vec(lru_lam_b[0]))
    scan_args = (gpre, lru_w_in[0].astype(BF16), lru_conv_w[0], vec(lru_conv_b[0]), p_f, p_b)
    h0 = jnp.zeros((bsz, 1, LRU_WIDTH), F32)
    cstep = ctx_len // N_SEG
    ctx5 = ctx.reshape(bsz, N_SEG, cstep, d).transpose(0, 2, 1, 3).reshape(bsz, cstep, 1, N_SEG, d)
    _, _, _, hbfin, pbfin, s_f = _lru_scan(ctx5, sh_c, scl_c, *scan_args, h0)
    _, s_b = _lru_carry(h0, hbfin, pbfin)
    x5 = x.reshape(bsz, rows, GRID_W // N_SEG, N_SEG, d)
    hsum, pb, g, hbfin, pbfin, _ = _lru_scan(x5, sh, scl, *scan_args, s_f)
    cb, _ = _lru_carry(s_b, hbfin, pbfin)
    out5 = _lru_out(hsum, pb, cb, g, x5, gt, gpost, lru_w_out[0].astype(BF16), rb=min(64, rows))
    return out5.reshape(bsz, seq, d)
```

```python
import functools

import jax
import jax.numpy as jnp
from jax import lax
from jax.experimental import pallas as pl
from jax.experimental.pallas import tpu as pltpu

F32 = jnp.float32
BF16 = jnp.bfloat16

RMS_EPS = 1e-6
GRID_W = 64
GLA_HEADS = 4
GLA_HEAD_K = 128
GLA_HEAD_V = 256
GLA_DK = GLA_HEADS * GLA_HEAD_K
GLA_DV = GLA_HEADS * GLA_HEAD_V
GLA_GATE_RANK = 16
GLA_GATE_NORM = 16.0
GLA_CHUNK = 128
GLA_FACTORISED_MIN_CUM = -60.0
LRU_BLOCKS = 5
LRU_BLOCK_W = 256
LRU_WIDTH = LRU_BLOCKS * LRU_BLOCK_W
LRU_C = 8.0
CONV_W = 4
CONV_LEFT = CONV_W // 2
N_SEG = 8

VMEM_LIMIT_BYTES = 56 * 1024 * 1024


def _cparams(*sem):
    return pltpu.CompilerParams(dimension_semantics=sem, vmem_limit_bytes=VMEM_LIMIT_BYTES)


def _silu(x):
    hx = 0.5 * x
    return hx * jnp.tanh(hx) + hx


def _sqrt(x):
    return x * lax.rsqrt(jnp.maximum(x, 1e-30))


def _log_sigmoid(x):
    return jnp.minimum(x, 0.0) - jnp.log(1.0 + jnp.exp(-jnp.abs(x)))


def _softplus(x):
    return jnp.maximum(x, 0.0) + jnp.log1p(jnp.exp(-jnp.abs(x)))


def _split_bf16(x):
    hi = x.astype(BF16)
    lo = (x - hi.astype(F32)).astype(BF16)
    return hi, lo


def _dot(a, b):
    return jnp.dot(a, b, preferred_element_type=F32)


def _dot_x3(a, b):
    a_hi, a_lo = _split_bf16(a)
    b_hi, b_lo = _split_bf16(b)
    return _dot(a_hi, b_hi) + (_dot(a_hi, b_lo) + _dot(a_lo, b_hi))


def _pre_norm_modulate(x, gpre, sh, scl):
    ms = jnp.mean(x * x, axis=-1, keepdims=True)
    return (x * lax.rsqrt(ms + RMS_EPS)) * (gpre * (1.0 + scl)) + sh


def _post_norm_residual(x, y, gpost, gt):
    ms = jnp.mean(y * y, axis=-1, keepdims=True)
    return x + gt * (y * lax.rsqrt(ms + RMS_EPS) * gpost)


def _ada_kernel(c_ref, w_ref, b_ref, o_ref):
    sc = _silu(c_ref[...])
    o_ref[...] = _dot_x3(sc, w_ref[...]) + b_ref[...]


def _ada_modulation(cvec, ada_w, ada_b):
    depth, d, n3 = ada_w.shape
    tn = 1024
    return pl.pallas_call(
        _ada_kernel,
        grid=(depth, n3 // tn),
        in_specs=[pl.BlockSpec((8, d), lambda i, j: (0, 0)),
                  pl.BlockSpec((None, d, tn), lambda i, j: (i, 0, j)),
                  pl.BlockSpec((None, 1, tn), lambda i, j: (i, 0, j))],
        out_specs=pl.BlockSpec((None, 8, tn), lambda i, j: (i, 0, j)),
        out_shape=jax.ShapeDtypeStruct((depth, 8, n3), F32),
        compiler_params=_cparams("parallel", "parallel"),
        name="ada_modulation",
    )(cvec, ada_w, ada_b.reshape(depth, 1, n3))


def _gla_in_kernel(x_ref, sh_ref, scl_ref, gpre_ref, w_ref, q_ref, k_ref, v_ref, g_ref, lr_ref):
    h = _pre_norm_modulate(x_ref[...], gpre_ref[...], sh_ref[...], scl_ref[...])
    proj = _dot(h.astype(BF16), w_ref[...])
    q_ref[...] = (proj[:, :GLA_DK] * (GLA_HEAD_K ** -0.5)).astype(BF16)
    k_ref[...] = proj[:, GLA_DK:2 * GLA_DK].astype(BF16)
    v_ref[...] = proj[:, 2 * GLA_DK:2 * GLA_DK + GLA_DV].astype(BF16)
    g_ref[...] = proj[:, 2 * GLA_DK + GLA_DV:2 * GLA_DK + 2 * GLA_DV].astype(BF16)
    lr_ref[...] = proj[:, 2 * GLA_DK + 2 * GLA_DV:]


def _gla_in(x, sh, scl, gpre, w, tm):
    bsz, t, d = x.shape
    lr_w = w.shape[1] - 2 * GLA_DK - 2 * GLA_DV
    row = lambda wd: pl.BlockSpec((None, tm, wd), lambda b, i: (b, i, 0))
    vec = lambda wd: pl.BlockSpec((None, 1, wd), lambda b, i: (b, 0, 0))
    full = lambda a: pl.BlockSpec(a.shape, lambda b, i: (0,) * a.ndim)
    return pl.pallas_call(
        _gla_in_kernel,
        grid=(bsz, t // tm),
        in_specs=[row(d), vec(d), vec(d), full(gpre), full(w)],
        out_specs=[row(GLA_DK), row(GLA_DK), row(GLA_DV), row(GLA_DV), row(lr_w)],
        out_shape=[jax.ShapeDtypeStruct((bsz, t, GLA_DK), BF16),
                   jax.ShapeDtypeStruct((bsz, t, GLA_DK), BF16),
                   jax.ShapeDtypeStruct((bsz, t, GLA_DV), BF16),
                   jax.ShapeDtypeStruct((bsz, t, GLA_DV), BF16),
                   jax.ShapeDtypeStruct((bsz, t, lr_w), F32)],
        compiler_params=_cparams("parallel", "parallel"),
        name="gla_in",
    )(x, sh, scl, gpre, w)


def _gla_log_decay_sums(lr_ref, wg_ref, bg_ref, cum_s, reverse):
    c = GLA_CHUNK
    row = lax.broadcasted_iota(jnp.int32, (c, 2 * c), 0)
    col = lax.broadcasted_iota(jnp.int32, (c, 2 * c), 1) % c
    tri = ((row <= col) if reverse else (row >= col)).astype(BF16)
    z = _dot(lr_ref[...].astype(BF16), wg_ref[...]) + bg_ref[...]
    log_a = _log_sigmoid(z) * (1.0 / GLA_GATE_NORM)
    hi, lo = _split_bf16(log_a)
    for i in range(lr_ref.shape[0] // c):
        rows = slice(i * c, (i + 1) * c)
        cum_s[rows, :] = _dot(tri, jnp.concatenate([hi[rows], lo[rows]], axis=0))


def _gla_scores_exact(q, k, cum, reverse, kf_s, cm_s, sc_s):
    c = q.shape[0]
    kf_s[...] = k
    cm_s[...] = cum
    sc_s[...] = jnp.zeros_like(sc_s)
    ridx = lax.broadcasted_iota(jnp.int32, (c, 1), 0)
    cidx = lax.broadcasted_iota(jnp.int32, (c, c), 1)

    def column(j, carry):
        kj = kf_s[pl.ds(j, 1), :]
        cj = cm_s[pl.ds(j, 1), :]
        live = (ridx <= j) if reverse else (ridx >= j)
        dec = jnp.exp(jnp.where(live, jnp.minimum(cum - cj, 0.0), -1e30))
        sj = jnp.sum(q * kj * dec, axis=-1, keepdims=True)
        sc_s[...] += jnp.where(cidx == j, sj, 0.0)
        return carry

    lax.fori_loop(0, c, column, 0)
    return sc_s[...]


def _gla_chunks(q_ref, k_ref, v_ref, cum_ref, s_ref, put_o, kf_s, cm_s, sc_s, *, reverse, factorised):
    tb = q_ref.shape[0]
    c = GLA_CHUNK
    row = lax.broadcasted_iota(jnp.int32, (c, c), 0)
    col = lax.broadcasted_iota(jnp.int32, (c, c), 1)
    keep = (row <= col) if reverse else (row >= col)
    chunks = range(tb // c)
    for i in (reversed(chunks) if reverse else chunks):
        rows = slice(i * c, (i + 1) * c)
        for hd in range(GLA_HEADS):
            kc = slice(hd * GLA_HEAD_K, (hd + 1) * GLA_HEAD_K)
            vc = slice(hd * GLA_HEAD_V, (hd + 1) * GLA_HEAD_V)
            cum = cum_ref[rows, kc]
            last = cum[0:1] if reverse else cum[c - 1:c]
            q = q_ref[rows, kc].astype(F32)
            k = k_ref[rows, kc].astype(F32)
            v = v_ref[rows, vc]
            qe = (q * jnp.exp(cum)).astype(BF16)
            kl = (k * jnp.exp(last - cum)).astype(BF16)
            st = s_ref[hd]
            if factorised:
                ke = (k * jnp.exp(-cum)).astype(BF16)
                scores = lax.dot_general(qe, ke, (((1,), (1,)), ((), ())), preferred_element_type=F32)
                scores = jnp.where(keep, scores, 0.0)
            else:
                scores = _gla_scores_exact(q, k, cum, reverse, kf_s, cm_s, sc_s)
            o = lax.dot_general(qe, st.astype(BF16), (((1,), (1,)), ((), ())),
                                preferred_element_type=F32)
            o = o + _dot(scores.astype(BF16), v)
            put_o(rows, vc, o)
            upd = lax.dot_general(v, kl, (((0,), (0,)), ((), ())), preferred_element_type=F32)
            s_ref[hd] = st * jnp.exp(last) + upd


def _gla_scan_kernel(*refs, reverse, fuse_out):
    if fuse_out:
        (q_ref, k_ref, v_ref, lr_ref, wg_ref, bg_ref, s0_ref,
         of_ref, g_ref, x_ref, gt_ref, gn_ref, gpost_ref, wout_ref,
         y_ref, sfin_ref, s_ref, cum_ref, kf_s, cm_s, sc_s, o_s) = refs
    else:
        (q_ref, k_ref, v_ref, lr_ref, wg_ref, bg_ref, s0_ref,
         y_ref, sfin_ref, s_ref, cum_ref, kf_s, cm_s, sc_s) = refs
    _gla_log_decay_sums(lr_ref, wg_ref, bg_ref, cum_ref, reverse)
    n = pl.program_id(1)

    @pl.when(n == 0)
    def _():
        s_ref[...] = s0_ref[...]

    if fuse_out:
        def put_o(rows, vc, o):
            o_s[rows, vc] = o + of_ref[rows, vc].astype(F32)
    else:
        def put_o(rows, vc, o):
            y_ref[rows, vc] = o.astype(BF16)

    c = GLA_CHUNK
    ends = [cum_ref[i * c:i * c + 1, :] if reverse else cum_ref[(i + 1) * c - 1:(i + 1) * c, :]
            for i in range(q_ref.shape[0] // c)]
    safe = jnp.min(functools.reduce(jnp.minimum, ends)) >= GLA_FACTORISED_MIN_CUM
    scan = functools.partial(_gla_chunks, q_ref, k_ref, v_ref, cum_ref, s_ref, put_o, kf_s, cm_s, sc_s,
                             reverse=reverse)
    pl.when(safe)(functools.partial(scan, factorised=True))
    pl.when(jnp.logical_not(safe))(functools.partial(scan, factorised=False))

    if fuse_out:
        gn = gn_ref[...]
        parts = []
        for hd in range(GLA_HEADS):
            oh = o_s[:, hd * GLA_HEAD_V:(hd + 1) * GLA_HEAD_V]
            ms = jnp.mean(oh * oh, axis=-1, keepdims=True)
            parts.append(oh * lax.rsqrt(ms + RMS_EPS) * gn)
        on = jnp.concatenate(parts, axis=-1) * _silu(g_ref[...].astype(F32))
        y = _dot(on.astype(BF16), wout_ref[...])
        y_ref[...] = _post_norm_residual(x_ref[...], y, gpost_ref[...], gt_ref[...])

    @pl.when(n == pl.num_programs(1) - 1)
    def _():
        sfin_ref[...] = s_ref[...]


def _gla_scan(q, k, v, lr, wg, bg, s0, tb, reverse, out_args=None):
    bsz, t, _ = q.shape
    nblk = t // tb
    c = GLA_CHUNK
    blk = (lambda b, n: (b, nblk - 1 - n, 0)) if reverse else (lambda b, n: (b, n, 0))
    row = lambda wd: pl.BlockSpec((None, tb, wd), blk)
    vec = lambda wd: pl.BlockSpec((None, 1, wd), lambda b, n: (b, 0, 0))
    full = lambda a: pl.BlockSpec(a.shape, lambda b, n: (0,) * a.ndim)
    st = pl.BlockSpec((None, GLA_HEADS, GLA_HEAD_V, GLA_HEAD_K), lambda b, n: (b, 0, 0, 0))
    in_specs = [row(GLA_DK), row(GLA_DK), row(GLA_DV), row(lr.shape[-1]), full(wg), full(bg), st]
    scratch = [pltpu.VMEM((GLA_HEADS, GLA_HEAD_V, GLA_HEAD_K), F32), pltpu.VMEM((tb, GLA_DK), F32),
               pltpu.VMEM((c, GLA_HEAD_K), F32), pltpu.VMEM((c, GLA_HEAD_K), F32), pltpu.VMEM((c, c), F32)]
    args = [q, k, v, lr, wg, bg, s0]
    if out_args is None:
        y_spec, y_shape = row(GLA_DV), jax.ShapeDtypeStruct((bsz, t, GLA_DV), BF16)
    else:
        o_other, g, x, gt, gn, gpost, w_out = out_args
        d = x.shape[-1]
        in_specs += [row(GLA_DV), row(GLA_DV), row(d), vec(d), full(gn), full(gpost), full(w_out)]
        args += [o_other, g, x, gt, gn, gpost, w_out]
        scratch.append(pltpu.VMEM((tb, GLA_DV), F32))
        y_spec, y_shape = row(d), jax.ShapeDtypeStruct((bsz, t, d), F32)
    return pl.pallas_call(
        functools.partial(_gla_scan_kernel, reverse=reverse, fuse_out=out_args is not None),
        grid=(bsz, nblk),
        in_specs=in_specs,
        out_specs=[y_spec, st],
        out_shape=[y_shape, jax.ShapeDtypeStruct(s0.shape, F32)],
        scratch_shapes=scratch,
        compiler_params=_cparams("parallel", "arbitrary"),
        name="gla_scan_bwd" if reverse else "gla_scan_fwd",
    )(*args)


def _gla_layer(x, sh, scl, gt, gpre, gpost, w_in, wg, bg, gn, w_out, s0_f, s0_b):
    t = x.shape[1]
    tm = min(512, t)
    q, k, v, g, lr = _gla_in(x, sh, scl, gpre, w_in, tm)
    o_f, s_f = _gla_scan(q, k, v, lr, wg[0], bg[0], s0_f, tm, reverse=False)
    x_new, s_b = _gla_scan(q, k, v, lr, wg[1], bg[1], s0_b, tm, reverse=True,
                           out_args=(o_f, g, x, gt, gn, gpost, w_out))
    return x_new, s_f, s_b


def _lru_scan_kernel(x_ref, xp_ref, xn_ref, sh_ref, scl_ref, gpre_ref, w_ref, cw_ref, cb_ref,
                     waf_ref, baf_ref, wxf_ref, bxf_ref, lamf_ref,
                     wab_ref, bab_ref, wxb_ref, bxb_ref, lamb_ref, h0_ref,
                     hsum_ref, pb_ref, g_ref, hbfin_ref, pbfin_ref, ffin_ref,
                     zext_s, hf_s, pf_s, hb_s, pbk_s, carry_s, cin_s, *au_s):
    g = pl.program_id(1)
    ng = pl.num_programs(1)
    r, s, d = x_ref.shape
    rows = r * s
    w = LRU_WIDTH
    halo_l = CONV_LEFT * s
    n_right = CONV_W - 1 - CONV_LEFT

    @pl.when(g == 0)
    def _():
        cin_s[...] = h0_ref[...]

    sh, scl, gpre = sh_ref[...], scl_ref[...], gpre_ref[...]
    hb16 = _pre_norm_modulate(x_ref[...].reshape(rows, d), gpre, sh, scl).astype(BF16)
    xh = jnp.concatenate([xp_ref[...].reshape(halo_l, d), xn_ref[...].reshape(n_right * s, d)], axis=0)
    zh = _dot(_pre_norm_modulate(xh, gpre, sh, scl).astype(BF16), w_ref[:, :w])
    sub = lax.broadcasted_iota(jnp.int32, (s, LRU_BLOCK_W), 0)

    def in_proj(nb):
        cs = slice(nb * LRU_BLOCK_W, (nb + 1) * LRU_BLOCK_W)
        zext_s[halo_l:halo_l + rows, cs] = _dot(hb16, w_ref[:, cs])
        g_ref[:, cs] = _dot(hb16, w_ref[:, w + nb * LRU_BLOCK_W:w + (nb + 1) * LRU_BLOCK_W]).astype(BF16)
        for j in range(CONV_LEFT):
            src = halo_l + (r - CONV_LEFT + j) * s
            inner = pltpu.roll(zext_s[src:src + s, cs], 1, 0)
            edge = jnp.where(g == 0, 0.0, pltpu.roll(zh[j * s:(j + 1) * s, cs], 1, 0))
            zext_s[j * s:(j + 1) * s, cs] = jnp.where(sub == 0, edge, inner)
        for j in range(n_right):
            src = halo_l + j * s
            inner = pltpu.roll(zext_s[src:src + s, cs], s - 1, 0)
            edge = jnp.where(g == ng - 1, 0.0,
                             pltpu.roll(zh[halo_l + j * s:halo_l + (j + 1) * s, cs], s - 1, 0))
            dst = halo_l + rows + j * s
            zext_s[dst:dst + s, cs] = jnp.where(sub == s - 1, edge, inner)

    def matmul_stage(nb):
        cs = slice(nb * LRU_BLOCK_W, (nb + 1) * LRU_BLOCK_W)
        in_proj(nb)
        zc = cb_ref[:, cs]
        for j in range(CONV_W):
            zc = zc + cw_ref[j:j + 1, cs] * zext_s[j * s:j * s + rows, cs]
        zcb = zc.astype(BF16)
        pre = [(_dot(zcb, wa_ref[nb]) + ba_ref[:, cs], _dot(zcb, wx_ref[nb]) + bx_ref[:, cs])
               for wa_ref, ba_ref, wx_ref, bx_ref in ((waf_ref, baf_ref, wxf_ref, bxf_ref),
                                                      (wab_ref, bab_ref, wxb_ref, bxb_ref))]
        return 0.5 * zc, pre

    staged = matmul_stage(0)
    for nb in range(LRU_BLOCKS):
        cs = slice(nb * LRU_BLOCK_W, (nb + 1) * LRU_BLOCK_W)
        af_s, uf_s, ab_s, ub_s = au_s[4 * (nb % 2):4 * (nb % 2) + 4]
        hzc, pre = staged
        if nb + 1 < LRU_BLOCKS:
            staged = matmul_stage(nb + 1)
        for (pre_r, pre_i), lam_ref, a_s, u_s in zip(pre, (lamf_ref, lamb_ref), (af_s, ab_s), (uf_s, ub_s)):
            tr = jnp.tanh(pre_r)
            ti = jnp.tanh(pre_i)
            c1 = (-0.5 * LRU_C) * _softplus(-lam_ref[:, cs])
            a = jnp.exp(c1 * tr + c1)
            a_s[...] = a
            u_s[...] = _sqrt(1.0 - a * a) * (hzc * ti + hzc)

        hf = hb = jnp.zeros((s, LRU_BLOCK_W), F32)
        pf = pb = jnp.ones((s, LRU_BLOCK_W), F32)
        for t in range(r):
            rf = slice(t * s, (t + 1) * s)
            rb = slice((r - 1 - t) * s, (r - t) * s)
            af = af_s[rf, :]
            ab = ab_s[rb, :]
            hf = af * hf + uf_s[rf, :]
            pf = af * pf
            hb = ab * hb + ub_s[rb, :]
            pb = ab * pb
            hf_s[rf, :] = hf
            pf_s[rf, :] = pf
            hb_s[rb, :] = hb
            pbk_s[rb, :] = pb

        cin = cin_s[:, cs]
        for c in range(s):
            carry_s[c:c + 1, :] = cin
            cin = hf[c:c + 1] + pf[c:c + 1] * cin
        cin_s[:, cs] = cin
        carry = carry_s[...]
        hsum = (hf_s[...].reshape(r, s, LRU_BLOCK_W) + pf_s[...].reshape(r, s, LRU_BLOCK_W) * carry[None]
                + hb_s[...].reshape(r, s, LRU_BLOCK_W))
        hsum_ref[:, cs] = hsum.reshape(rows, LRU_BLOCK_W).astype(BF16)
        pb_ref[:, cs] = pbk_s[...].astype(BF16)
        hbfin_ref[:, cs] = hb
        pbfin_ref[:, cs] = pb

    ffin_ref[...] = cin_s[...]


def _lru_scan(x5, sh, scl, gpre, w_in, conv_w, conv_b, p_f, p_b, h0_f):
    bsz, r, ng, s, d = x5.shape
    rows = r * s
    w = LRU_WIDTH
    n_right = CONV_W - 1 - CONV_LEFT
    once = pl.Buffered(1)
    cur = pl.BlockSpec((None, r, None, s, d), lambda b, g: (b, 0, g, 0, 0))
    prev = pl.BlockSpec((None, CONV_LEFT, None, s, d),
                        lambda b, g: (b, r // CONV_LEFT - 1, jnp.maximum(g - 1, 0), 0, 0))
    nxt = pl.BlockSpec((None, n_right, None, s, d), lambda b, g: (b, 0, jnp.minimum(g + 1, ng - 1), 0, 0))
    vec = lambda wd: pl.BlockSpec((None, 1, wd), lambda b, g: (b, 0, 0))
    full = lambda a: pl.BlockSpec(a.shape, lambda b, g: (0,) * a.ndim, pipeline_mode=once)
    row = pl.BlockSpec((None, rows, w), lambda b, g: (b, g, 0))
    fin = pl.BlockSpec((None, None, s, w), lambda b, g: (b, g, 0, 0))
    blk = lambda: pltpu.VMEM((rows, LRU_BLOCK_W), F32)
    return pl.pallas_call(
        _lru_scan_kernel,
        grid=(bsz, ng),
        in_specs=[cur, prev, nxt, vec(d), vec(d), full(gpre), full(w_in), full(conv_w), full(conv_b)]
                 + [full(a) for a in p_f] + [full(a) for a in p_b] + [vec(w)],
        out_specs=[row, row, row, fin, fin, vec(w)],
        out_shape=[jax.ShapeDtypeStruct((bsz, ng * rows, w), BF16),
                   jax.ShapeDtypeStruct((bsz, ng * rows, w), BF16),
                   jax.ShapeDtypeStruct((bsz, ng * rows, w), BF16),
                   jax.ShapeDtypeStruct((bsz, ng, s, w), F32),
                   jax.ShapeDtypeStruct((bsz, ng, s, w), F32),
                   jax.ShapeDtypeStruct((bsz, 1, w), F32)],
        scratch_shapes=[pltpu.VMEM((rows + (CONV_W - 1) * s, w), F32)] + [blk() for _ in range(4)]
                       + [pltpu.VMEM((s, LRU_BLOCK_W), F32), pltpu.VMEM((1, w), F32)]
                       + [blk() for _ in range(8)],
        compiler_params=_cparams("parallel", "arbitrary"),
        name="lru_scan",
    )(x5, x5, x5, sh, scl, gpre, w_in, conv_w, conv_b, *p_f, *p_b, h0_f)


def _lru_carry_kernel(h0_ref, hfin_ref, pfin_ref, carry_ref, final_ref):
    c = h0_ref[...]
    for k in range(hfin_ref.shape[0] - 1, -1, -1):
        carry_ref[k:k + 1, :] = c
        c = hfin_ref[k:k + 1, :] + pfin_ref[k:k + 1, :] * c
    final_ref[...] = c


def _lru_carry(h0, hfin, pfin):
    bsz, ng, s, w = hfin.shape
    vec = pl.BlockSpec((None, 1, w), lambda b: (b, 0, 0))
    runs = pl.BlockSpec((None, ng * s, w), lambda b: (b, 0, 0))
    carry, final = pl.pallas_call(
        _lru_carry_kernel,
        grid=(bsz,),
        in_specs=[vec, runs, runs],
        out_specs=[runs, vec],
        out_shape=[jax.ShapeDtypeStruct((bsz, ng * s, w), F32), jax.ShapeDtypeStruct((bsz, 1, w), F32)],
        compiler_params=_cparams("parallel"),
        name="lru_carry",
    )(h0, hfin.reshape(bsz, ng * s, w), pfin.reshape(bsz, ng * s, w))
    return carry.reshape(bsz, ng, s, w), final


def _lru_out_kernel(hsum_ref, pb_ref, cb_ref, g_ref, x_ref, gt_ref, gpost_ref, w_ref, o_ref):
    r, s, d = x_ref.shape
    w = hsum_ref.shape[-1]
    h = hsum_ref[...].astype(F32).reshape(r, s, w) + pb_ref[...].astype(F32).reshape(r, s, w) * cb_ref[...][None]
    y = _dot((h.reshape(r * s, w) * _silu(g_ref[...].astype(F32))).astype(BF16), w_ref[...])
    x = x_ref[...].reshape(r * s, d)
    o_ref[...] = _post_norm_residual(x, y, gpost_ref[...], gt_ref[...]).reshape(r, s, d)


def _lru_out(hsum, pb, cb, g, x5, gt, gpost, w, rb):
    bsz, r, ng, s, d = x5.shape
    nr = r // rb
    tm = rb * s
    width = hsum.shape[-1]
    xio = pl.BlockSpec((None, rb, None, s, d), lambda b, g, i: (b, i, g, 0, 0))
    row = pl.BlockSpec((None, tm, width), lambda b, g, i: (b, g * nr + i, 0))
    runs = pl.BlockSpec((None, None, s, width), lambda b, g, i: (b, g, 0, 0))
    vec = pl.BlockSpec((None, 1, d), lambda b, g, i: (b, 0, 0))
    full = lambda a: pl.BlockSpec(a.shape, lambda b, g, i: (0,) * a.ndim)
    return pl.pallas_call(
        _lru_out_kernel,
        grid=(bsz, ng, nr),
        in_specs=[row, row, runs, row, xio, vec, full(gpost), full(w)],
        out_specs=xio,
        out_shape=jax.ShapeDtypeStruct(x5.shape, F32),
        compiler_params=_cparams("parallel", "parallel", "parallel"),
        name="lru_out",
    )(hsum, pb, cb, g, x5, gt, gpost, w)


def kernel(x, c, ctx, c_ctx, ada_w, ada_b, norm_pre, norm_post, gla_w_in, gla_wg_f, gla_bg_f, gla_wg_b, gla_bg_b, gla_norm, gla_w_out, lru_w_in, lru_conv_w, lru_conv_b, lru_wa_f, lru_ba_f, lru_wx_f, lru_bx_f, lru_lam_f, lru_wa_b, lru_ba_b, lru_wx_b, lru_bx_b, lru_lam_b, lru_w_out):
    bsz, seq, d = x.shape
    ctx_len = ctx.shape[1]
    rows = seq // GRID_W

    cvec = jnp.concatenate([c, c_ctx[None], jnp.zeros((8 - bsz - 1, d), F32)], axis=0)
    mod = _ada_modulation(cvec, ada_w, ada_b)

    def mods(i):
        lat = [mod[i, :bsz, None, j * d:(j + 1) * d] for j in range(3)]
        con = [jnp.broadcast_to(mod[i, bsz, None, None, j * d:(j + 1) * d], (bsz, 1, d)) for j in range(3)]
        return lat, con

    (sh, scl, gt), (sh_c, scl_c, gt_c) = mods(0)
    gpre, gpost = norm_pre[0][None], norm_post[0][None]
    lr_pad = 128 - 2 * GLA_GATE_RANK
    w_in = jnp.pad(gla_w_in[0], ((0, 0), (0, lr_pad))).astype(BF16)
    rk = GLA_GATE_RANK
    wg = (jnp.pad(gla_wg_f[0], ((0, 128 - rk), (0, 0))).astype(BF16),
          jnp.pad(gla_wg_b[0], ((rk, 128 - 2 * rk), (0, 0))).astype(BF16))
    bg = (gla_bg_f[0][None], gla_bg_b[0][None])
    gla_args = (gpre, gpost, w_in, wg, bg, gla_norm[0][None], gla_w_out[0].astype(BF16))
    s0 = jnp.zeros((bsz, GLA_HEADS, GLA_HEAD_V, GLA_HEAD_K), F32)
    ctx, s_f, s_b = _gla_layer(ctx, sh_c, scl_c, gt_c, *gla_args, s0, s0)
    x, _, _ = _gla_layer(x, sh, scl, gt, *gla_args, s_f, s_b)

    (sh, scl, gt), (sh_c, scl_c, _) = mods(1)
    gpre, gpost = norm_pre[1][None], norm_post[1][None]
    vec = lambda a: a[None]
    half = lambda wgt, bias: ((0.5 * wgt).astype(BF16), vec(0.5 * bias))
    p_f = (*half(lru_wa_f[0], lru_ba_f[0]), *half(lru_wx_f[0], lru_bx_f[0]), vec(lru_lam_f[0]))
    p_b = (*half(lru_wa_b[0], lru_ba_b[0]), *half(lru_wx_b[0], lru_bx_b[0]), vec(lru_lam_b[0]))
    scan_args = (gpre, lru_w_in[0].astype(BF16), lru_conv_w[0], vec(lru_conv_b[0]), p_f, p_b)
    h0 = jnp.zeros((bsz, 1, LRU_WIDTH), F32)
    cstep = ctx_len // N_SEG
    ctx5 = ctx.reshape(bsz, N_SEG, cstep, d).transpose(0, 2, 1, 3).reshape(bsz, cstep, 1, N_SEG, d)
    _, _, _, hbfin, pbfin, s_f = _lru_scan(ctx5, sh_c, scl_c, *scan_args, h0)
    _, s_b = _lru_carry(h0, hbfin, pbfin)
    x5 = x.reshape(bsz, rows, GRID_W // N_SEG, N_SEG, d)
    hsum, pb, g, hbfin, pbfin, _ = _lru_scan(x5, sh, scl, *scan_args, s_f)
    cb, _ = _lru_carry(s_b, hbfin, pbfin)
    out5 = _lru_out(hsum, pb, cb, g, x5, gt, gpost, lru_w_out[0].astype(BF16), rb=min(64, rows))
    return out5.reshape(bsz, seq, d)
```

```python
import functools

import jax
import jax.numpy as jnp
from jax import lax
from jax.experimental import pallas as pl
from jax.experimental.pallas import tpu as pltpu

F32 = jnp.float32
BF16 = jnp.bfloat16

RMS_EPS = 1e-6
GRID_W = 64
GLA_HEADS = 4
GLA_HEAD_K = 128
GLA_HEAD_V = 256
GLA_DK = GLA_HEADS * GLA_HEAD_K
GLA_DV = GLA_HEADS * GLA_HEAD_V
GLA_GATE_RANK = 16
GLA_GATE_NORM = 16.0
GLA_CHUNK = 256
GLA_FACTORISED_MIN_CUM = -60.0
GLA_IN_ROWS = 512
GLA_SCAN_ROWS = 1024
LRU_BLOCKS = 5
LRU_BLOCK_W = 256
LRU_WIDTH = LRU_BLOCKS * LRU_BLOCK_W
LRU_C = 8.0
CONV_W = 4
CONV_LEFT = CONV_W // 2
N_SEG = 8

VMEM_LIMIT_BYTES = 56 * 1024 * 1024


def _cparams(*sem, flags=None):
    return pltpu.CompilerParams(dimension_semantics=sem, vmem_limit_bytes=VMEM_LIMIT_BYTES, flags=flags)


def _silu(x):
    hx = 0.5 * x
    return hx * jnp.tanh(hx) + hx


def _sqrt(x):
    return x * lax.rsqrt(jnp.maximum(x, 1e-30))


def _log_sigmoid(x):
    return jnp.minimum(x, 0.0) - jnp.log(1.0 + jnp.exp(-jnp.abs(x)))


def _softplus(x):
    return jnp.maximum(x, 0.0) + jnp.log1p(jnp.exp(-jnp.abs(x)))


def _split_bf16(x):
    hi = x.astype(BF16)
    lo = (x - hi.astype(F32)).astype(BF16)
    return hi, lo


def _dot(a, b):
    return jnp.dot(a, b, preferred_element_type=F32)


def _dot_x3(a, b):
    a_hi, a_lo = _split_bf16(a)
    b_hi, b_lo = _split_bf16(b)
    return _dot(a_hi, b_hi) + (_dot(a_hi, b_lo) + _dot(a_lo, b_hi))


def _pre_norm_modulate(x, gpre, sh, scl):
    ms = jnp.mean(x * x, axis=-1, keepdims=True)
    return (x * lax.rsqrt(ms + RMS_EPS)) * (gpre * (1.0 + scl)) + sh


def _post_norm_residual(x, y, gpost, gt):
    ms = jnp.mean(y * y, axis=-1, keepdims=True)
    return x + gt * (y * lax.rsqrt(ms + RMS_EPS) * gpost)


def _ada_kernel(c_ref, w_ref, b_ref, o_ref):
    sc = _silu(c_ref[...])
    o_ref[...] = _dot_x3(sc, w_ref[...]) + b_ref[...]


def _ada_modulation(cvec, ada_w, ada_b):
    depth, d, n3 = ada_w.shape
    tn = 1024
    return pl.pallas_call(
        _ada_kernel,
        grid=(depth, n3 // tn),
        in_specs=[pl.BlockSpec((8, d), lambda i, j: (0, 0)),
                  pl.BlockSpec((None, d, tn), lambda i, j: (i, 0, j)),
                  pl.BlockSpec((None, 1, tn), lambda i, j: (i, 0, j))],
        out_specs=pl.BlockSpec((None, 8, tn), lambda i, j: (i, 0, j)),
        out_shape=jax.ShapeDtypeStruct((depth, 8, n3), F32),
        compiler_params=_cparams("parallel", "parallel"),
        name="ada_modulation",
    )(cvec, ada_w, ada_b.reshape(depth, 1, n3))


def _gla_in_kernel(x_ref, sh_ref, scl_ref, gpre_ref, w_ref, q_ref, k_ref, v_ref, g_ref, lr_ref):
    h = _pre_norm_modulate(x_ref[...], gpre_ref[...], sh_ref[...], scl_ref[...])
    proj = _dot(h.astype(BF16), w_ref[...])
    q_ref[...] = (proj[:, :GLA_DK] * (GLA_HEAD_K ** -0.5)).astype(BF16)
    k_ref[...] = proj[:, GLA_DK:2 * GLA_DK].astype(BF16)
    v_ref[...] = proj[:, 2 * GLA_DK:2 * GLA_DK + GLA_DV].astype(BF16)
    g_ref[...] = proj[:, 2 * GLA_DK + GLA_DV:2 * GLA_DK + 2 * GLA_DV].astype(BF16)
    lr_ref[...] = proj[:, 2 * GLA_DK + 2 * GLA_DV:]


def _gla_in(x, sh, scl, gpre, w, tm):
    bsz, t, d = x.shape
    lr_w = w.shape[1] - 2 * GLA_DK - 2 * GLA_DV
    row = lambda wd: pl.BlockSpec((None, tm, wd), lambda b, i: (b, i, 0))
    vec = lambda wd: pl.BlockSpec((None, 1, wd), lambda b, i: (b, 0, 0))
    full = lambda a: pl.BlockSpec(a.shape, lambda b, i: (0,) * a.ndim)
    return pl.pallas_call(
        _gla_in_kernel,
        grid=(bsz, t // tm),
        in_specs=[row(d), vec(d), vec(d), full(gpre), full(w)],
        out_specs=[row(GLA_DK), row(GLA_DK), row(GLA_DV), row(GLA_DV), row(lr_w)],
        out_shape=[jax.ShapeDtypeStruct((bsz, t, GLA_DK), BF16),
                   jax.ShapeDtypeStruct((bsz, t, GLA_DK), BF16),
                   jax.ShapeDtypeStruct((bsz, t, GLA_DV), BF16),
                   jax.ShapeDtypeStruct((bsz, t, GLA_DV), BF16),
                   jax.ShapeDtypeStruct((bsz, t, lr_w), F32)],
        compiler_params=_cparams("parallel", "parallel"),
        name="gla_in",
    )(x, sh, scl, gpre, w)


def _gla_log_decay_sums(lr_ref, wg_ref, bg_ref, cum_s, reverse):
    c = GLA_CHUNK
    row = lax.broadcasted_iota(jnp.int32, (c, 2 * c), 0)
    col = lax.broadcasted_iota(jnp.int32, (c, 2 * c), 1) % c
    tri = ((row <= col) if reverse else (row >= col)).astype(BF16)
    z = _dot(lr_ref[...].astype(BF16), wg_ref[...]) + bg_ref[...]
    log_a = _log_sigmoid(z) * (1.0 / GLA_GATE_NORM)
    hi, lo = _split_bf16(log_a)
    for i in range(lr_ref.shape[0] // c):
        rows = slice(i * c, (i + 1) * c)
        cum_s[rows, :] = _dot(tri, jnp.concatenate([hi[rows], lo[rows]], axis=0))


def _gla_scores_exact(q, k, cum, reverse, kf_s, cm_s, sc_s):
    c = q.shape[0]
    kf_s[...] = k
    cm_s[...] = cum
    sc_s[...] = jnp.zeros_like(sc_s)
    ridx = lax.broadcasted_iota(jnp.int32, (c, 1), 0)
    cidx = lax.broadcasted_iota(jnp.int32, (c, c), 1)

    def column(j, carry):
        kj = kf_s[pl.ds(j, 1), :]
        cj = cm_s[pl.ds(j, 1), :]
        live = (ridx <= j) if reverse else (ridx >= j)
        dec = jnp.exp(jnp.where(live, jnp.minimum(cum - cj, 0.0), -1e30))
        sj = jnp.sum(q * kj * dec, axis=-1, keepdims=True)
        sc_s[...] += jnp.where(cidx == j, sj, 0.0)
        return carry

    lax.fori_loop(0, c, column, 0)
    return sc_s[...]


def _gla_chunks(q_ref, k_ref, v_ref, cum_ref, s_ref, put_o, kf_s, cm_s, sc_s, *, reverse, factorised):
    tb = q_ref.shape[0]
    c = GLA_CHUNK
    row = lax.broadcasted_iota(jnp.int32, (c, c), 0)
    col = lax.broadcasted_iota(jnp.int32, (c, c), 1)
    keep = (row <= col) if reverse else (row >= col)
    chunks = range(tb // c)
    for i in (reversed(chunks) if reverse else chunks):
        rows = slice(i * c, (i + 1) * c)
        for hd in range(GLA_HEADS):
            kc = slice(hd * GLA_HEAD_K, (hd + 1) * GLA_HEAD_K)
            vc = slice(hd * GLA_HEAD_V, (hd + 1) * GLA_HEAD_V)
            cum = cum_ref[rows, kc]
            last = cum[0:1] if reverse else cum[c - 1:c]
            q = q_ref[rows, kc].astype(F32)
            k = k_ref[rows, kc].astype(F32)
            v = v_ref[rows, vc]
            qe = (q * jnp.exp(cum)).astype(BF16)
            kl = (k * jnp.exp(last - cum)).astype(BF16)
            st = s_ref[hd]
            if factorised:
                ke = (k * jnp.exp(-cum)).astype(BF16)
                scores = lax.dot_general(qe, ke, (((1,), (1,)), ((), ())), preferred_element_type=F32)
                scores = jnp.where(keep, scores, 0.0)
            else:
                scores = _gla_scores_exact(q, k, cum, reverse, kf_s, cm_s, sc_s)
            o = lax.dot_general(qe, st.astype(BF16), (((1,), (1,)), ((), ())),
                                preferred_element_type=F32)
            o = o + _dot(scores.astype(BF16), v)
            put_o(rows, vc, o)
            upd = lax.dot_general(v, kl, (((0,), (0,)), ((), ())), preferred_element_type=F32)
            s_ref[hd] = st * jnp.exp(last) + upd


def _gla_scan_kernel(*refs, reverse, fuse_out):
    if fuse_out:
        (q_ref, k_ref, v_ref, lr_ref, wg_ref, bg_ref, s0_ref,
         of_ref, g_ref, x_ref, gt_ref, gn_ref, gpost_ref, wout_ref,
         y_ref, sfin_ref, s_ref, cum_ref, kf_s, cm_s, sc_s, o_s) = refs
    else:
        (q_ref, k_ref, v_ref, lr_ref, wg_ref, bg_ref, s0_ref,
         y_ref, sfin_ref, s_ref, cum_ref, kf_s, cm_s, sc_s) = refs
    _gla_log_decay_sums(lr_ref, wg_ref, bg_ref, cum_ref, reverse)
    n = pl.program_id(1)

    @pl.when(n == 0)
    def _():
        s_ref[...] = s0_ref[...]

    if fuse_out:
        def put_o(rows, vc, o):
            o_s[rows, vc] = o + of_ref[rows, vc].astype(F32)
    else:
        def put_o(rows, vc, o):
            y_ref[rows, vc] = o.astype(BF16)

    c = GLA_CHUNK
    ends = [cum_ref[i * c:i * c + 1, :] if reverse else cum_ref[(i + 1) * c - 1:(i + 1) * c, :]
            for i in range(q_ref.shape[0] // c)]
    safe = jnp.min(functools.reduce(jnp.minimum, ends)) >= GLA_FACTORISED_MIN_CUM
    scan = functools.partial(_gla_chunks, q_ref, k_ref, v_ref, cum_ref, s_ref, put_o, kf_s, cm_s, sc_s,
                             reverse=reverse)
    pl.when(safe)(functools.partial(scan, factorised=True))
    pl.when(jnp.logical_not(safe))(functools.partial(scan, factorised=False))

    if fuse_out:
        gn = gn_ref[...]
        parts = []
        for hd in range(GLA_HEADS):
            oh = o_s[:, hd * GLA_HEAD_V:(hd + 1) * GLA_HEAD_V]
            ms = jnp.mean(oh * oh, axis=-1, keepdims=True)
            parts.append(oh * lax.rsqrt(ms + RMS_EPS) * gn)
        on = jnp.concatenate(parts, axis=-1) * _silu(g_ref[...].astype(F32))
        y = _dot(on.astype(BF16), wout_ref[...])
        y_ref[...] = _post_norm_residual(x_ref[...], y, gpost_ref[...], gt_ref[...])

    @pl.when(n == pl.num_programs(1) - 1)
    def _():
        sfin_ref[...] = s_ref[...]


def _gla_scan(q, k, v, lr, wg, bg, s0, tb, reverse, out_args=None):
    bsz, t, _ = q.shape
    nblk = t // tb
    c = GLA_CHUNK
    pos = (lambda n: nblk - 1 - n) if reverse else (lambda n: n)
    row = lambda wd: pl.BlockSpec((None, tb, wd), lambda b, n: (b, pos(n), 0))
    vec = lambda wd: pl.BlockSpec((None, 1, wd), lambda b, n: (b, 0, 0))
    full = lambda a: pl.BlockSpec(a.shape, lambda b, n: (0,) * a.ndim)
    st = pl.BlockSpec((None, GLA_HEADS, GLA_HEAD_V, GLA_HEAD_K), lambda b, n: (b, 0, 0, 0))
    in_specs = [row(GLA_DK), row(GLA_DK), row(GLA_DV), row(lr.shape[-1]), full(wg), full(bg), st]
    scratch = [pltpu.VMEM((GLA_HEADS, GLA_HEAD_V, GLA_HEAD_K), F32), pltpu.VMEM((tb, GLA_DK), F32),
               pltpu.VMEM((c, GLA_HEAD_K), F32), pltpu.VMEM((c, GLA_HEAD_K), F32), pltpu.VMEM((c, c), F32)]
    args = [q, k, v, lr, wg, bg, s0]
    if out_args is None:
        y_spec, y_shape = row(GLA_DV), jax.ShapeDtypeStruct((bsz, t, GLA_DV), BF16)
    else:
        o_other, g, x, gt, gn, gpost, w_out = out_args
        d = x.shape[-1]
        in_specs += [row(GLA_DV), row(GLA_DV), row(d), vec(d), full(gn), full(gpost), full(w_out)]
        args += [o_other, g, x, gt, gn, gpost, w_out]
        scratch.append(pltpu.VMEM((tb, GLA_DV), F32))
        y_spec, y_shape = row(d), jax.ShapeDtypeStruct((bsz, t, d), F32)
    return pl.pallas_call(
        functools.partial(_gla_scan_kernel, reverse=reverse, fuse_out=out_args is not None),
        grid=(bsz, nblk),
        in_specs=in_specs,
        out_specs=[y_spec, st],
        out_shape=[y_shape, jax.ShapeDtypeStruct(s0.shape, F32)],
        scratch_shapes=scratch,
        compiler_params=_cparams("parallel", "arbitrary"),
        name="gla_scan_bwd" if reverse else "gla_scan_fwd",
    )(*args)


def _gla_layer(x, sh, scl, gt, gpre, gpost, w_in, wg, bg, gn, w_out, s0_f, s0_b):
    t = x.shape[1]
    q, k, v, g, lr = _gla_in(x, sh, scl, gpre, w_in, min(GLA_IN_ROWS, t))
    tb = min(GLA_SCAN_ROWS, t)
    o_f, s_f = _gla_scan(q, k, v, lr, wg[0], bg[0], s0_f, tb, reverse=False)
    x_new, s_b = _gla_scan(q, k, v, lr, wg[1], bg[1], s0_b, tb, reverse=True,
                           out_args=(o_f, g, x, gt, gn, gpost, w_out))
    return x_new, s_f, s_b


def _lru_scan_kernel(x_ref, xp_ref, xn_ref, sh_ref, scl_ref, gpre_ref, w_ref, cw_ref, cb_ref,
                     waf_ref, baf_ref, wxf_ref, bxf_ref, lamf_ref,
                     wab_ref, bab_ref, wxb_ref, bxb_ref, lamb_ref, h0_ref,
                     hsum_ref, pb_ref, g_ref, hbfin_ref, pbfin_ref, ffin_ref,
                     zext_s, hf_s, pf_s, hb_s, pbk_s, carry_s, cin_s, *pre_s):
    g = pl.program_id(1)
    ng = pl.num_programs(1)
    r, s, d = x_ref.shape
    rows = r * s
    w = LRU_WIDTH
    halo_l = CONV_LEFT * s
    n_right = CONV_W - 1 - CONV_LEFT

    @pl.when(g == 0)
    def _():
        cin_s[...] = h0_ref[...]

    sh, scl, gpre = sh_ref[...], scl_ref[...], gpre_ref[...]
    hb16 = _pre_norm_modulate(x_ref[...].reshape(rows, d), gpre, sh, scl).astype(BF16)
    xh = jnp.concatenate([xp_ref[...].reshape(halo_l, d), xn_ref[...].reshape(n_right * s, d)], axis=0)
    zh = _dot(_pre_norm_modulate(xh, gpre, sh, scl).astype(BF16), w_ref[:, :w])
    sub = lax.broadcasted_iota(jnp.int32, (s, LRU_BLOCK_W), 0)

    def in_proj(nb):
        cs = slice(nb * LRU_BLOCK_W, (nb + 1) * LRU_BLOCK_W)
        zext_s[halo_l:halo_l + rows, cs] = _dot(hb16, w_ref[:, cs])
        g_ref[:, cs] = _dot(hb16, w_ref[:, w + nb * LRU_BLOCK_W:w + (nb + 1) * LRU_BLOCK_W]).astype(BF16)
        for j in range(CONV_LEFT):
            src = halo_l + (r - CONV_LEFT + j) * s
            inner = pltpu.roll(zext_s[src:src + s, cs], 1, 0)
            edge = jnp.where(g == 0, 0.0, pltpu.roll(zh[j * s:(j + 1) * s, cs], 1, 0))
            zext_s[j * s:(j + 1) * s, cs] = jnp.where(sub == 0, edge, inner)
        for j in range(n_right):
            src = halo_l + j * s
            inner = pltpu.roll(zext_s[src:src + s, cs], s - 1, 0)
            edge = jnp.where(g == ng - 1, 0.0,
                             pltpu.roll(zh[halo_l + j * s:halo_l + (j + 1) * s, cs], s - 1, 0))
            dst = halo_l + rows + j * s
            zext_s[dst:dst + s, cs] = jnp.where(sub == s - 1, edge, inner)

    def gate_matmuls(nb):
        cs = slice(nb * LRU_BLOCK_W, (nb + 1) * LRU_BLOCK_W)
        zc = cb_ref[:, cs]
        for j in range(CONV_W):
            zc = zc + cw_ref[j:j + 1, cs] * zext_s[j * s:j * s + rows, cs]
        zcb = zc.astype(BF16)
        prf_s, pif_s, prb_s, pib_s, hzc_s = pre_s[5 * (nb % 2):5 * (nb % 2) + 5]
        prf_s[...] = _dot(zcb, waf_ref[nb]) + baf_ref[:, cs]
        pif_s[...] = _dot(zcb, wxf_ref[nb]) + bxf_ref[:, cs]
        prb_s[...] = _dot(zcb, wab_ref[nb]) + bab_ref[:, cs]
        pib_s[...] = _dot(zcb, wxb_ref[nb]) + bxb_ref[:, cs]
        hzc_s[...] = 0.5 * zc

    in_proj(0)
    gate_matmuls(0)
    for nb in range(LRU_BLOCKS):
        cs = slice(nb * LRU_BLOCK_W, (nb + 1) * LRU_BLOCK_W)
        if nb + 1 < LRU_BLOCKS:
            in_proj(nb + 1)
            gate_matmuls(nb + 1)
        prf_s, pif_s, prb_s, pib_s, hzc_s = pre_s[5 * (nb % 2):5 * (nb % 2) + 5]
        c1f = jnp.broadcast_to((-0.5 * LRU_C) * _softplus(-lamf_ref[:, cs]), (s, LRU_BLOCK_W))
        c1b = jnp.broadcast_to((-0.5 * LRU_C) * _softplus(-lamb_ref[:, cs]), (s, LRU_BLOCK_W))

        def gate(pr_s, pi_s, c1, rws):
            tr = jnp.tanh(pr_s[rws, :])
            ti = jnp.tanh(pi_s[rws, :])
            hz = hzc_s[rws, :]
            a = jnp.exp(c1 * tr + c1)
            return a, _sqrt(1.0 - a * a) * (hz * ti + hz)

        hf = hb = jnp.zeros((s, LRU_BLOCK_W), F32)
        pf = pb = jnp.ones((s, LRU_BLOCK_W), F32)
        for t in range(r):
            rf = slice(t * s, (t + 1) * s)
            rb = slice((r - 1 - t) * s, (r - t) * s)
            af, uf = gate(prf_s, pif_s, c1f, rf)
            ab, ub = gate(prb_s, pib_s, c1b, rb)
            hf = af * hf + uf
            pf = af * pf
            hb = ab * hb + ub
            pb = ab * pb
            hf_s[rf, :] = hf
            pf_s[rf, :] = pf
            hb_s[rb, :] = hb
            pbk_s[rb, :] = pb

        cin = cin_s[:, cs]
        for c in range(s):
            carry_s[c:c + 1, :] = cin
            cin = hf[c:c + 1] + pf[c:c + 1] * cin
        cin_s[:, cs] = cin
        carry = carry_s[...][None]
        piece = min(r, 16)
        for i in range(r // piece):
            rws = slice(i * piece * s, (i + 1) * piece * s)
            shape3 = (piece, s, LRU_BLOCK_W)
            hsum = (hf_s[rws, :].reshape(shape3) + pf_s[rws, :].reshape(shape3) * carry
                    + hb_s[rws, :].reshape(shape3))
            hsum_ref[rws, cs] = hsum.reshape(piece * s, LRU_BLOCK_W).astype(BF16)
            pb_ref[rws, cs] = pbk_s[rws, :].astype(BF16)
        hbfin_ref[:, cs] = hb
        pbfin_ref[:, cs] = pb

    ffin_ref[...] = cin_s[...]


def _lru_scan(x5, sh, scl, gpre, w_in, conv_w, conv_b, p_f, p_b, h0_f):
    bsz, r, ng, s, d = x5.shape
    rows = r * s
    w = LRU_WIDTH
    n_right = CONV_W - 1 - CONV_LEFT
    once = pl.Buffered(1)
    cur = pl.BlockSpec((None, r, None, s, d), lambda b, g: (b, 0, g, 0, 0))
    prev = pl.BlockSpec((None, CONV_LEFT, None, s, d),
                        lambda b, g: (b, r // CONV_LEFT - 1, jnp.maximum(g - 1, 0), 0, 0))
    nxt = pl.BlockSpec((None, n_right, None, s, d), lambda b, g: (b, 0, jnp.minimum(g + 1, ng - 1), 0, 0))
    vec = lambda wd: pl.BlockSpec((None, 1, wd), lambda b, g: (b, 0, 0))
    full = lambda a: pl.BlockSpec(a.shape, lambda b, g: (0,) * a.ndim, pipeline_mode=once)
    row = pl.BlockSpec((None, rows, w), lambda b, g: (b, g, 0))
    fin = pl.BlockSpec((None, None, s, w), lambda b, g: (b, g, 0, 0))
    blk = lambda: pltpu.VMEM((rows, LRU_BLOCK_W), F32)
    return pl.pallas_call(
        _lru_scan_kernel,
        grid=(bsz, ng),
        in_specs=[cur, prev, nxt, vec(d), vec(d), full(gpre), full(w_in), full(conv_w), full(conv_b)]
                 + [full(a) for a in p_f] + [full(a) for a in p_b] + [vec(w)],
        out_specs=[row, row, row, fin, fin, vec(w)],
        out_shape=[jax.ShapeDtypeStruct((bsz, ng * rows, w), BF16),
                   jax.ShapeDtypeStruct((bsz, ng * rows, w), BF16),
                   jax.ShapeDtypeStruct((bsz, ng * rows, w), BF16),
                   jax.ShapeDtypeStruct((bsz, ng, s, w), F32),
                   jax.ShapeDtypeStruct((bsz, ng, s, w), F32),
                   jax.ShapeDtypeStruct((bsz, 1, w), F32)],
        scratch_shapes=[pltpu.VMEM((rows + (CONV_W - 1) * s, w), F32)] + [blk() for _ in range(4)]
                       + [pltpu.VMEM((s, LRU_BLOCK_W), F32), pltpu.VMEM((1, w), F32)]
                       + [blk() for _ in range(10)],
        compiler_params=_cparams("parallel", "arbitrary"),
        name="lru_scan",
    )(x5, x5, x5, sh, scl, gpre, w_in, conv_w, conv_b, *p_f, *p_b, h0_f)


def _lru_carry_kernel(h0_ref, hfin_ref, pfin_ref, carry_ref, final_ref):
    c = h0_ref[...]
    for k in range(hfin_ref.shape[0] - 1, -1, -1):
        carry_ref[k:k + 1, :] = c
        c = hfin_ref[k:k + 1, :] + pfin_ref[k:k + 1, :] * c
    final_ref[...] = c


def _lru_carry(h0, hfin, pfin):
    bsz, ng, s, w = hfin.shape
    vec = pl.BlockSpec((None, 1, w), lambda b: (b, 0, 0))
    runs = pl.BlockSpec((None, ng * s, w), lambda b: (b, 0, 0))
    carry, final = pl.pallas_call(
        _lru_carry_kernel,
        grid=(bsz,),
        in_specs=[vec, runs, runs],
        out_specs=[runs, vec],
        out_shape=[jax.ShapeDtypeStruct((bsz, ng * s, w), F32), jax.ShapeDtypeStruct((bsz, 1, w), F32)],
        compiler_params=_cparams("parallel"),
        name="lru_carry",
    )(h0, hfin.reshape(bsz, ng * s, w), pfin.reshape(bsz, ng * s, w))
    return carry.reshape(bsz, ng, s, w), final


def _lru_out_kernel(hsum_ref, pb_ref, cb_ref, g_ref, x_ref, gt_ref, gpost_ref, w_ref, o_ref):
    r, s, d = x_ref.shape
    w = hsum_ref.shape[-1]
    h = hsum_ref[...].astype(F32).reshape(r, s, w) + pb_ref[...].astype(F32).reshape(r, s, w) * cb_ref[...][None]
    y = _dot((h.reshape(r * s, w) * _silu(g_ref[...].astype(F32))).astype(BF16), w_ref[...])
    x = x_ref[...].reshape(r * s, d)
    o_ref[...] = _post_norm_residual(x, y, gpost_ref[...], gt_ref[...]).reshape(r, s, d)


def _lru_out(hsum, pb, cb, g, x5, gt, gpost, w, rb):
    bsz, r, ng, s, d = x5.shape
    nr = r // rb
    tm = rb * s
    width = hsum.shape[-1]
    xio = pl.BlockSpec((None, rb, None, s, d), lambda b, g, i: (b, i, g, 0, 0))
    row = pl.BlockSpec((None, tm, width), lambda b, g, i: (b, g * nr + i, 0))
    runs = pl.BlockSpec((None, None, s, width), lambda b, g, i: (b, g, 0, 0))
    vec = pl.BlockSpec((None, 1, d), lambda b, g, i: (b, 0, 0))
    full = lambda a: pl.BlockSpec(a.shape, lambda b, g, i: (0,) * a.ndim)
    return pl.pallas_call(
        _lru_out_kernel,
        grid=(bsz, ng, nr),
        in_specs=[row, row, runs, row, xio, vec, full(gpost), full(w)],
        out_specs=xio,
        out_shape=jax.ShapeDtypeStruct(x5.shape, F32),
        compiler_params=_cparams("parallel", "parallel", "parallel"),
        name="lru_out",
    )(hsum, pb, cb, g, x5, gt, gpost, w)


def kernel(x, c, ctx, c_ctx, ada_w, ada_b, norm_pre, norm_post, gla_w_in, gla_wg_f, gla_bg_f, gla_wg_b, gla_bg_b, gla_norm, gla_w_out, lru_w_in, lru_conv_w, lru_conv_b, lru_wa_f, lru_ba_f, lru_wx_f, lru_bx_f, lru_lam_f, lru_wa_b, lru_ba_b, lru_wx_b, lru_bx_b, lru_lam_b, lru_w_out):
    bsz, seq, d = x.shape
    ctx_len = ctx.shape[1]
    rows = seq // GRID_W

    cvec = jnp.concatenate([c, c_ctx[None], jnp.zeros((8 - bsz - 1, d), F32)], axis=0)
    mod = _ada_modulation(cvec, ada_w, ada_b)

    def mods(i):
        lat = [mod[i, :bsz, None, j * d:(j + 1) * d] for j in range(3)]
        con = [jnp.broadcast_to(mod[i, bsz, None, None, j * d:(j + 1) * d], (bsz, 1, d)) for j in range(3)]
        return lat, con

    (sh, scl, gt), (sh_c, scl_c, gt_c) = mods(0)
    gpre, gpost = norm_pre[0][None], norm_post[0][None]
    lr_pad = 128 - 2 * GLA_GATE_RANK
    w_in = jnp.pad(gla_w_in[0], ((0, 0), (0, lr_pad))).astype(BF16)
    rk = GLA_GATE_RANK
    wg = (jnp.pad(gla_wg_f[0], ((0, 128 - rk), (0, 0))).astype(BF16),
          jnp.pad(gla_wg_b[0], ((rk, 128 - 2 * rk), (0, 0))).astype(BF16))
    bg = (gla_bg_f[0][None], gla_bg_b[0][None])
    gla_args = (gpre, gpost, w_in, wg, bg, gla_norm[0][None], gla_w_out[0].astype(BF16))
    s0 = jnp.zeros((bsz, GLA_HEADS, GLA_HEAD_V, GLA_HEAD_K), F32)
    ctx, s_f, s_b = _gla_layer(ctx, sh_c, scl_c, gt_c, *gla_args, s0, s0)
    x, _, _ = _gla_layer(x, sh, scl, gt, *gla_args, s_f, s_b)

    (sh, scl, gt), (sh_c, scl_c, _) = mods(1)
    gpre, gpost = norm_pre[1][None], norm_post[1][None]
    vec = lambda a: a[None]
    half = lambda wgt, bias: ((0.5 * wgt).astype(BF16), vec(0.5 * bias))
    p_f = (*half(lru_wa_f[0], lru_ba_f[0]), *half(lru_wx_f[0], lru_bx_f[0]), vec(lru_lam_f[0]))
    p_b = (*half(lru_wa_b[0], lru_ba_b[0]), *half(lru_wx_b[0], lru_bx_b[0]), vec(lru_lam_b[0]))
    scan_args = (gpre, lru_w_in[0].astype(BF16), lru_conv_w[0], vec(lru_conv_b[0]), p_f, p_b)
    h0 = jnp.zeros((bsz, 1, LRU_WIDTH), F32)
    cstep = ctx_len // N_SEG
    ctx5 = ctx.reshape(bsz, N_SEG, cstep, d).transpose(0, 2, 1, 3).reshape(bsz, cstep, 1, N_SEG, d)
    _, _, _, hbfin, pbfin, s_f = _lru_scan(ctx5, sh_c, scl_c, *scan_args, h0)
    _, s_b = _lru_carry(h0, hbfin, pbfin)
    x5 = x.reshape(bsz, rows, GRID_W // N_SEG, N_SEG, d)
    hsum, pb, g, hbfin, pbfin, _ = _lru_scan(x5, sh, scl, *scan_args, s_f)
    cb, _ = _lru_carry(s_b, hbfin, pbfin)
    out5 = _lru_out(hsum, pb, cb, g, x5, gt, gpost, lru_w_out[0].astype(BF16), rb=min(64, rows))
    return out5.reshape(bsz, seq, d)
```

```python
import functools

import jax
import jax.numpy as jnp
from jax import lax
from jax.experimental import pallas as pl
from jax.experimental.pallas import tpu as pltpu

F32 = jnp.float32
BF16 = jnp.bfloat16

RMS_EPS = 1e-6
GRID_W = 64
GLA_HEADS = 4
GLA_HEAD_K = 128
GLA_HEAD_V = 256
GLA_DK = GLA_HEADS * GLA_HEAD_K
GLA_DV = GLA_HEADS * GLA_HEAD_V
GLA_GATE_RANK = 16
GLA_GATE_NORM = 16.0
GLA_CHUNK = 256
GLA_FACTORISED_MIN_CUM = -60.0
GLA_IN_ROWS = 1024
GLA_SCAN_ROWS = 1024
LRU_BLOCKS = 5
LRU_BLOCK_W = 256
LRU_WIDTH = LRU_BLOCKS * LRU_BLOCK_W
LRU_C = 8.0
CONV_W = 4
CONV_LEFT = CONV_W // 2
N_SEG = 8
LRU_OUT_STEPS = 128

VMEM_LIMIT_BYTES = 56 * 1024 * 1024


def _cparams(*sem, flags=None):
    return pltpu.CompilerParams(dimension_semantics=sem, vmem_limit_bytes=VMEM_LIMIT_BYTES, flags=flags)


def _silu(x):
    hx = 0.5 * x
    return hx * jnp.tanh(hx) + hx


def _sqrt(x):
    return x * lax.rsqrt(jnp.maximum(x, 1e-30))


def _log_sigmoid(x):
    return jnp.minimum(x, 0.0) - jnp.log(1.0 + jnp.exp(-jnp.abs(x)))


def _softplus(x):
    return jnp.maximum(x, 0.0) + jnp.log1p(jnp.exp(-jnp.abs(x)))


def _split_bf16(x):
    hi = x.astype(BF16)
    lo = (x - hi.astype(F32)).astype(BF16)
    return hi, lo


def _dot(a, b):
    return jnp.dot(a, b, preferred_element_type=F32)


def _dot_x3(a, b):
    a_hi, a_lo = _split_bf16(a)
    b_hi, b_lo = _split_bf16(b)
    return _dot(a_hi, b_hi) + (_dot(a_hi, b_lo) + _dot(a_lo, b_hi))


def _pre_norm_modulate(x, gpre, sh, scl):
    ms = jnp.mean(x * x, axis=-1, keepdims=True)
    return (x * lax.rsqrt(ms + RMS_EPS)) * (gpre * (1.0 + scl)) + sh


def _post_norm_residual(x, y, gpost, gt):
    ms = jnp.mean(y * y, axis=-1, keepdims=True)
    return x + (y * lax.rsqrt(ms + RMS_EPS)) * (gt * gpost)


def _ada_kernel(c_ref, w_ref, b_ref, o_ref):
    sc = _silu(c_ref[...])
    o_ref[...] = _dot_x3(sc, w_ref[...]) + b_ref[...]


def _ada_modulation(cvec, ada_w, ada_b):
    depth, d, n3 = ada_w.shape
    tn = 1024
    return pl.pallas_call(
        _ada_kernel,
        grid=(depth, n3 // tn),
        in_specs=[pl.BlockSpec((8, d), lambda i, j: (0, 0)),
                  pl.BlockSpec((None, d, tn), lambda i, j: (i, 0, j)),
                  pl.BlockSpec((None, 1, tn), lambda i, j: (i, 0, j))],
        out_specs=pl.BlockSpec((None, 8, tn), lambda i, j: (i, 0, j)),
        out_shape=jax.ShapeDtypeStruct((depth, 8, n3), F32),
        compiler_params=_cparams("parallel", "parallel"),
        name="ada_modulation",
    )(cvec, ada_w, ada_b.reshape(depth, 1, n3))


def _gla_in_kernel(x_ref, sh_ref, scl_ref, gpre_ref, w_ref, q_ref, k_ref, v_ref, g_ref, lr_ref):
    h = _pre_norm_modulate(x_ref[...], gpre_ref[...], sh_ref[...], scl_ref[...])
    proj = _dot(h.astype(BF16), w_ref[...])
    q_ref[...] = (proj[:, :GLA_DK] * (GLA_HEAD_K ** -0.5)).astype(BF16)
    k_ref[...] = proj[:, GLA_DK:2 * GLA_DK].astype(BF16)
    v_ref[...] = proj[:, 2 * GLA_DK:2 * GLA_DK + GLA_DV].astype(BF16)
    g_ref[...] = proj[:, 2 * GLA_DK + GLA_DV:2 * GLA_DK + 2 * GLA_DV].astype(BF16)
    lr_ref[...] = proj[:, 2 * GLA_DK + 2 * GLA_DV:]


def _gla_in(x, sh, scl, gpre, w, tm):
    bsz, t, d = x.shape
    lr_w = w.shape[1] - 2 * GLA_DK - 2 * GLA_DV
    row = lambda wd: pl.BlockSpec((None, tm, wd), lambda b, i: (b, i, 0))
    vec = lambda wd: pl.BlockSpec((None, 1, wd), lambda b, i: (b, 0, 0))
    full = lambda a: pl.BlockSpec(a.shape, lambda b, i: (0,) * a.ndim)
    return pl.pallas_call(
        _gla_in_kernel,
        grid=(bsz, t // tm),
        in_specs=[row(d), vec(d), vec(d), full(gpre), full(w)],
        out_specs=[row(GLA_DK), row(GLA_DK), row(GLA_DV), row(GLA_DV), row(lr_w)],
        out_shape=[jax.ShapeDtypeStruct((bsz, t, GLA_DK), BF16),
                   jax.ShapeDtypeStruct((bsz, t, GLA_DK), BF16),
                   jax.ShapeDtypeStruct((bsz, t, GLA_DV), BF16),
                   jax.ShapeDtypeStruct((bsz, t, GLA_DV), BF16),
                   jax.ShapeDtypeStruct((bsz, t, lr_w), F32)],
        compiler_params=_cparams("parallel", "parallel"),
        name="gla_in",
    )(x, sh, scl, gpre, w)


def _gla_log_decay_sums(lr_ref, wg_ref, bg_ref, cum_s, reverse):
    c = GLA_CHUNK
    row = lax.broadcasted_iota(jnp.int32, (c, 2 * c), 0)
    col = lax.broadcasted_iota(jnp.int32, (c, 2 * c), 1) % c
    tri = ((row <= col) if reverse else (row >= col)).astype(BF16)
    z = _dot(lr_ref[...].astype(BF16), wg_ref[...]) + bg_ref[...]
    log_a = _log_sigmoid(z) * (1.0 / GLA_GATE_NORM)
    hi, lo = _split_bf16(log_a)
    for i in range(lr_ref.shape[0] // c):
        rows = slice(i * c, (i + 1) * c)
        cum_s[rows, :] = _dot(tri, jnp.concatenate([hi[rows], lo[rows]], axis=0))


def _gla_scores_exact(q, k, cum, reverse, kf_s, cm_s, sc_s):
    c = q.shape[0]
    kf_s[...] = k
    cm_s[...] = cum
    sc_s[...] = jnp.zeros_like(sc_s)
    ridx = lax.broadcasted_iota(jnp.int32, (c, 1), 0)
    cidx = lax.broadcasted_iota(jnp.int32, (c, c), 1)

    def column(j, carry):
        kj = kf_s[pl.ds(j, 1), :]
        cj = cm_s[pl.ds(j, 1), :]
        live = (ridx <= j) if reverse else (ridx >= j)
        dec = jnp.exp(jnp.where(live, jnp.minimum(cum - cj, 0.0), -1e30))
        sj = jnp.sum(q * kj * dec, axis=-1, keepdims=True)
        sc_s[...] += jnp.where(cidx == j, sj, 0.0)
        return carry

    lax.fori_loop(0, c, column, 0)
    return sc_s[...]


def _gla_chunks(q_ref, k_ref, v_ref, cum_ref, s_ref, put_o, kf_s, cm_s, sc_s, *, reverse, factorised):
    tb = q_ref.shape[0]
    c = GLA_CHUNK
    row = lax.broadcasted_iota(jnp.int32, (c, c), 0)
    col = lax.broadcasted_iota(jnp.int32, (c, c), 1)
    keep = (row <= col) if reverse else (row >= col)
    chunks = range(tb // c)
    for i in (reversed(chunks) if reverse else chunks):
        rows = slice(i * c, (i + 1) * c)
        for hd in range(GLA_HEADS):
            kc = slice(hd * GLA_HEAD_K, (hd + 1) * GLA_HEAD_K)
            vc = slice(hd * GLA_HEAD_V, (hd + 1) * GLA_HEAD_V)
            cum = cum_ref[rows, kc]
            last = cum[0:1] if reverse else cum[c - 1:c]
            q = q_ref[rows, kc].astype(F32)
            k = k_ref[rows, kc].astype(F32)
            v = v_ref[rows, vc]
            qe = (q * jnp.exp(cum)).astype(BF16)
            kl = (k * jnp.exp(last - cum)).astype(BF16)
            st = s_ref[hd]
            if factorised:
                ke = (k * jnp.exp(-cum)).astype(BF16)
                scores = lax.dot_general(qe, ke, (((1,), (1,)), ((), ())), preferred_element_type=F32)
                scores = jnp.where(keep, scores, 0.0)
            else:
                scores = _gla_scores_exact(q, k, cum, reverse, kf_s, cm_s, sc_s)
            o = lax.dot_general(qe, st.astype(BF16), (((1,), (1,)), ((), ())),
                                preferred_element_type=F32)
            o = o + _dot(scores.astype(BF16), v)
            put_o(rows, vc, o)
            upd = lax.dot_general(v, kl, (((0,), (0,)), ((), ())), preferred_element_type=F32)
            s_ref[hd] = st * jnp.exp(last) + upd


def _gla_scan_kernel(*refs, reverse, fuse_out):
    if fuse_out:
        (q_ref, k_ref, v_ref, lr_ref, wg_ref, bg_ref, s0_ref,
         of_ref, g_ref, x_ref, gt_ref, gn_ref, gpost_ref, wout_ref,
         y_ref, sfin_ref, s_ref, cum_ref, kf_s, cm_s, sc_s, o_s) = refs
    else:
        (q_ref, k_ref, v_ref, lr_ref, wg_ref, bg_ref, s0_ref,
         y_ref, sfin_ref, s_ref, cum_ref, kf_s, cm_s, sc_s) = refs
    _gla_log_decay_sums(lr_ref, wg_ref, bg_ref, cum_ref, reverse)
    n = pl.program_id(1)

    @pl.when(n == 0)
    def _():
        s_ref[...] = s0_ref[...]

    if fuse_out:
        def put_o(rows, vc, o):
            o_s[rows, vc] = o + of_ref[rows, vc].astype(F32)
    else:
        def put_o(rows, vc, o):
            y_ref[rows, vc] = o.astype(BF16)

    c = GLA_CHUNK
    ends = [cum_ref[i * c:i * c + 1, :] if reverse else cum_ref[(i + 1) * c - 1:(i + 1) * c, :]
            for i in range(q_ref.shape[0] // c)]
    safe = jnp.min(functools.reduce(jnp.minimum, ends)) >= GLA_FACTORISED_MIN_CUM
    scan = functools.partial(_gla_chunks, q_ref, k_ref, v_ref, cum_ref, s_ref, put_o, kf_s, cm_s, sc_s,
                             reverse=reverse)
    pl.when(safe)(functools.partial(scan, factorised=True))
    pl.when(jnp.logical_not(safe))(functools.partial(scan, factorised=False))

    if fuse_out:
        gn = gn_ref[...]
        parts = []
        for hd in range(GLA_HEADS):
            oh = o_s[:, hd * GLA_HEAD_V:(hd + 1) * GLA_HEAD_V]
            ms = jnp.mean(oh * oh, axis=-1, keepdims=True)
            parts.append(oh * lax.rsqrt(ms + RMS_EPS) * gn)
        on = jnp.concatenate(parts, axis=-1) * _silu(g_ref[...].astype(F32))
        y = _dot(on.astype(BF16), wout_ref[...])
        y_ref[...] = _post_norm_residual(x_ref[...], y, gpost_ref[...], gt_ref[...])

    @pl.when(n == pl.num_programs(1) - 1)
    def _():
        sfin_ref[...] = s_ref[...]


def _gla_scan(q, k, v, lr, wg, bg, s0, tb, reverse, out_args=None):
    bsz, t, _ = q.shape
    nblk = t // tb
    c = GLA_CHUNK
    pos = (lambda n: nblk - 1 - n) if reverse else (lambda n: n)
    row = lambda wd: pl.BlockSpec((None, tb, wd), lambda b, n: (b, pos(n), 0))
    vec = lambda wd: pl.BlockSpec((None, 1, wd), lambda b, n: (b, 0, 0))
    full = lambda a: pl.BlockSpec(a.shape, lambda b, n: (0,) * a.ndim)
    st = pl.BlockSpec((None, GLA_HEADS, GLA_HEAD_V, GLA_HEAD_K), lambda b, n: (b, 0, 0, 0))
    in_specs = [row(GLA_DK), row(GLA_DK), row(GLA_DV), row(lr.shape[-1]), full(wg), full(bg), st]
    scratch = [pltpu.VMEM((GLA_HEADS, GLA_HEAD_V, GLA_HEAD_K), F32), pltpu.VMEM((tb, GLA_DK), F32),
               pltpu.VMEM((c, GLA_HEAD_K), F32), pltpu.VMEM((c, GLA_HEAD_K), F32), pltpu.VMEM((c, c), F32)]
    args = [q, k, v, lr, wg, bg, s0]
    if out_args is None:
        y_spec, y_shape = row(GLA_DV), jax.ShapeDtypeStruct((bsz, t, GLA_DV), BF16)
    else:
        o_other, g, x, gt, gn, gpost, w_out = out_args
        d = x.shape[-1]
        in_specs += [row(GLA_DV), row(GLA_DV), row(d), vec(d), full(gn), full(gpost), full(w_out)]
        args += [o_other, g, x, gt, gn, gpost, w_out]
        scratch.append(pltpu.VMEM((tb, GLA_DV), F32))
        y_spec, y_shape = row(d), jax.ShapeDtypeStruct((bsz, t, d), F32)
    return pl.pallas_call(
        functools.partial(_gla_scan_kernel, reverse=reverse, fuse_out=out_args is not None),
        grid=(bsz, nblk),
        in_specs=in_specs,
        out_specs=[y_spec, st],
        out_shape=[y_shape, jax.ShapeDtypeStruct(s0.shape, F32)],
        scratch_shapes=scratch,
        compiler_params=_cparams("parallel", "arbitrary"),
        name="gla_scan_bwd" if reverse else "gla_scan_fwd",
    )(*args)


def _gla_layer(x, sh, scl, gt, gpre, gpost, w_in, wg, bg, gn, w_out, s0_f, s0_b):
    t = x.shape[1]
    q, k, v, g, lr = _gla_in(x, sh, scl, gpre, w_in, min(GLA_IN_ROWS, t))
    tb = min(GLA_SCAN_ROWS, t)
    o_f, s_f = _gla_scan(q, k, v, lr, wg[0], bg[0], s0_f, tb, reverse=False)
    x_new, s_b = _gla_scan(q, k, v, lr, wg[1], bg[1], s0_b, tb, reverse=True,
                           out_args=(o_f, g, x, gt, gn, gpost, w_out))
    return x_new, s_f, s_b


def _lru_scan_kernel(x_ref, xp_ref, xn_ref, sh_ref, scl_ref, gpre_ref, w_ref, cw_ref, cb_ref,
                     waf_ref, baf_ref, wxf_ref, bxf_ref, lamf_ref,
                     wab_ref, bab_ref, wxb_ref, bxb_ref, lamb_ref, h0_ref,
                     hsum_ref, pb_ref, g_ref, hbfin_ref, pbfin_ref, ffin_ref,
                     zext_s, hf_s, pf_s, hb_s, pbk_s, carry_s, cin_s, *pre_s):
    g = pl.program_id(1)
    ng = pl.num_programs(1)
    r, s, d = x_ref.shape
    rows = r * s
    w = LRU_WIDTH
    halo_l = CONV_LEFT * s
    n_right = CONV_W - 1 - CONV_LEFT

    @pl.when(g == 0)
    def _():
        cin_s[...] = h0_ref[...]

    sh, scl, gpre = sh_ref[...], scl_ref[...], gpre_ref[...]
    hb16 = _pre_norm_modulate(x_ref[...].reshape(rows, d), gpre, sh, scl).astype(BF16)
    xh = jnp.concatenate([xp_ref[...].reshape(halo_l, d), xn_ref[...].reshape(n_right * s, d)], axis=0)
    zh = _dot(_pre_norm_modulate(xh, gpre, sh, scl).astype(BF16), w_ref[:, :w])
    sub = lax.broadcasted_iota(jnp.int32, (s, LRU_BLOCK_W), 0)

    def in_proj(nb):
        cs = slice(nb * LRU_BLOCK_W, (nb + 1) * LRU_BLOCK_W)
        zext_s[halo_l:halo_l + rows, cs] = _dot(hb16, w_ref[:, cs])
        g_ref[:, cs] = _dot(hb16, w_ref[:, w + nb * LRU_BLOCK_W:w + (nb + 1) * LRU_BLOCK_W]).astype(BF16)
        for j in range(CONV_LEFT):
            src = halo_l + (r - CONV_LEFT + j) * s
            inner = pltpu.roll(zext_s[src:src + s, cs], 1, 0)
            edge = jnp.where(g == 0, 0.0, pltpu.roll(zh[j * s:(j + 1) * s, cs], 1, 0))
            zext_s[j * s:(j + 1) * s, cs] = jnp.where(sub == 0, edge, inner)
        for j in range(n_right):
            src = halo_l + j * s
            inner = pltpu.roll(zext_s[src:src + s, cs], s - 1, 0)
            edge = jnp.where(g == ng - 1, 0.0,
                             pltpu.roll(zh[halo_l + j * s:halo_l + (j + 1) * s, cs], s - 1, 0))
            dst = halo_l + rows + j * s
            zext_s[dst:dst + s, cs] = jnp.where(sub == s - 1, edge, inner)

    def gate_matmuls(nb):
        cs = slice(nb * LRU_BLOCK_W, (nb + 1) * LRU_BLOCK_W)
        hz = cb_ref[:, cs]
        for j in range(CONV_W):
            hz = hz + cw_ref[j:j + 1, cs] * zext_s[j * s:j * s + rows, cs]
        hzb = hz.astype(BF16)
        prf_s, pif_s, prb_s, pib_s, hzc_s = pre_s[5 * (nb % 2):5 * (nb % 2) + 5]
        prf_s[...] = _dot(hzb, waf_ref[nb]) + baf_ref[:, cs]
        pif_s[...] = _dot(hzb, wxf_ref[nb]) + bxf_ref[:, cs]
        prb_s[...] = _dot(hzb, wab_ref[nb]) + bab_ref[:, cs]
        pib_s[...] = _dot(hzb, wxb_ref[nb]) + bxb_ref[:, cs]
        hzc_s[...] = hz

    in_proj(0)
    gate_matmuls(0)
    for nb in range(LRU_BLOCKS):
        cs = slice(nb * LRU_BLOCK_W, (nb + 1) * LRU_BLOCK_W)
        if nb + 1 < LRU_BLOCKS:
            in_proj(nb + 1)
            gate_matmuls(nb + 1)
        prf_s, pif_s, prb_s, pib_s, hzc_s = pre_s[5 * (nb % 2):5 * (nb % 2) + 5]
        c1f = jnp.broadcast_to((-0.5 * LRU_C) * _softplus(-lamf_ref[:, cs]), (s, LRU_BLOCK_W))
        c1b = jnp.broadcast_to((-0.5 * LRU_C) * _softplus(-lamb_ref[:, cs]), (s, LRU_BLOCK_W))

        def gate(pr_s, pi_s, c1, rws):
            tr = jnp.tanh(pr_s[rws, :])
            ti = jnp.tanh(pi_s[rws, :])
            hz = hzc_s[rws, :]
            a = jnp.exp(c1 * tr + c1)
            return a, _sqrt(1.0 - a * a) * (hz * ti + hz)

        hf = hb = jnp.zeros((s, LRU_BLOCK_W), F32)
        pf = pb = jnp.ones((s, LRU_BLOCK_W), F32)
        for t in range(r):
            rf = slice(t * s, (t + 1) * s)
            rb = slice((r - 1 - t) * s, (r - t) * s)
            af, uf = gate(prf_s, pif_s, c1f, rf)
            ab, ub = gate(prb_s, pib_s, c1b, rb)
            hf = af * hf + uf
            pf = af * pf
            hb = ab * hb + ub
            pb = ab * pb
            hf_s[rf, :] = hf
            pf_s[rf, :] = pf
            hb_s[rb, :] = hb
            pbk_s[rb, :] = pb

        cin = cin_s[:, cs]
        for c in range(s):
            carry_s[c:c + 1, :] = cin
            cin = hf[c:c + 1] + pf[c:c + 1] * cin
        cin_s[:, cs] = cin
        carry = carry_s[...][None]
        piece = min(r, 16)
        for i in range(r // piece):
            rws = slice(i * piece * s, (i + 1) * piece * s)
            shape3 = (piece, s, LRU_BLOCK_W)
            hsum = (hf_s[rws, :].reshape(shape3) + pf_s[rws, :].reshape(shape3) * carry
                    + hb_s[rws, :].reshape(shape3))
            hsum_ref[rws, cs] = hsum.reshape(piece * s, LRU_BLOCK_W).astype(BF16)
            pb_ref[rws, cs] = pbk_s[rws, :].astype(BF16)
        hbfin_ref[:, cs] = hb
        pbfin_ref[:, cs] = pb

    ffin_ref[...] = cin_s[...]


def _lru_scan(x5, sh, scl, gpre, w_in, conv_w, conv_b, p_f, p_b, h0_f):
    bsz, r, ng, s, d = x5.shape
    rows = r * s
    w = LRU_WIDTH
    n_right = CONV_W - 1 - CONV_LEFT
    once = pl.Buffered(1)
    cur = pl.BlockSpec((None, r, None, s, d), lambda b, g: (b, 0, g, 0, 0))
    prev = pl.BlockSpec((None, CONV_LEFT, None, s, d),
                        lambda b, g: (b, r // CONV_LEFT - 1, jnp.maximum(g - 1, 0), 0, 0))
    nxt = pl.BlockSpec((None, n_right, None, s, d), lambda b, g: (b, 0, jnp.minimum(g + 1, ng - 1), 0, 0))
    vec = lambda wd: pl.BlockSpec((None, 1, wd), lambda b, g: (b, 0, 0))
    full = lambda a: pl.BlockSpec(a.shape, lambda b, g: (0,) * a.ndim, pipeline_mode=once)
    row = pl.BlockSpec((None, rows, w), lambda b, g: (b, g, 0))
    fin = pl.BlockSpec((None, None, s, w), lambda b, g: (b, g, 0, 0))
    blk = lambda: pltpu.VMEM((rows, LRU_BLOCK_W), F32)
    return pl.pallas_call(
        _lru_scan_kernel,
        grid=(bsz, ng),
        in_specs=[cur, prev, nxt, vec(d), vec(d), full(gpre), full(w_in), full(conv_w), full(conv_b)]
                 + [full(a) for a in p_f] + [full(a) for a in p_b] + [vec(w)],
        out_specs=[row, row, row, fin, fin, vec(w)],
        out_shape=[jax.ShapeDtypeStruct((bsz, ng * rows, w), BF16),
                   jax.ShapeDtypeStruct((bsz, ng * rows, w), BF16),
                   jax.ShapeDtypeStruct((bsz, ng * rows, w), BF16),
                   jax.ShapeDtypeStruct((bsz, ng, s, w), F32),
                   jax.ShapeDtypeStruct((bsz, ng, s, w), F32),
                   jax.ShapeDtypeStruct((bsz, 1, w), F32)],
        scratch_shapes=[pltpu.VMEM((rows + (CONV_W - 1) * s, w), F32)] + [blk() for _ in range(4)]
                       + [pltpu.VMEM((s, LRU_BLOCK_W), F32), pltpu.VMEM((1, w), F32)]
                       + [blk() for _ in range(10)],
        compiler_params=_cparams("parallel", "arbitrary"),
        name="lru_scan",
    )(x5, x5, x5, sh, scl, gpre, w_in, conv_w, conv_b, *p_f, *p_b, h0_f)


def _lru_carry_kernel(h0_ref, hfin_ref, pfin_ref, carry_ref, final_ref):
    c = h0_ref[...]
    for k in range(hfin_ref.shape[0] - 1, -1, -1):
        carry_ref[k:k + 1, :] = c
        c = hfin_ref[k:k + 1, :] + pfin_ref[k:k + 1, :] * c
    final_ref[...] = c


def _lru_carry(h0, hfin, pfin):
    bsz, ng, s, w = hfin.shape
    vec = pl.BlockSpec((None, 1, w), lambda b: (b, 0, 0))
    runs = pl.BlockSpec((None, ng * s, w), lambda b: (b, 0, 0))
    carry, final = pl.pallas_call(
        _lru_carry_kernel,
        grid=(bsz,),
        in_specs=[vec, runs, runs],
        out_specs=[runs, vec],
        out_shape=[jax.ShapeDtypeStruct((bsz, ng * s, w), F32), jax.ShapeDtypeStruct((bsz, 1, w), F32)],
        compiler_params=_cparams("parallel"),
        name="lru_carry",
    )(h0, hfin.reshape(bsz, ng * s, w), pfin.reshape(bsz, ng * s, w))
    return carry.reshape(bsz, ng, s, w), final


def _lru_out_kernel(hsum_ref, pb_ref, cb_ref, g_ref, x_ref, gt_ref, gpost_ref, w_ref, o_ref):
    r, s, d = x_ref.shape
    w = hsum_ref.shape[-1]
    h = hsum_ref[...].astype(F32).reshape(r, s, w) + pb_ref[...].astype(F32).reshape(r, s, w) * cb_ref[...][None]
    y = _dot((h.reshape(r * s, w) * _silu(g_ref[...].astype(F32))).astype(BF16), w_ref[...])
    x = x_ref[...].reshape(r * s, d)
    o_ref[...] = _post_norm_residual(x, y, gpost_ref[...], gt_ref[...]).reshape(r, s, d)


def _lru_out(hsum, pb, cb, g, x5, gt, gpost, w, rb):
    bsz, r, ng, s, d = x5.shape
    nr = r // rb
    tm = rb * s
    width = hsum.shape[-1]
    xio = pl.BlockSpec((None, rb, None, s, d), lambda b, g, i: (b, i, g, 0, 0))
    row = pl.BlockSpec((None, tm, width), lambda b, g, i: (b, g * nr + i, 0))
    runs = pl.BlockSpec((None, None, s, width), lambda b, g, i: (b, g, 0, 0))
    vec = pl.BlockSpec((None, 1, d), lambda b, g, i: (b, 0, 0))
    full = lambda a: pl.BlockSpec(a.shape, lambda b, g, i: (0,) * a.ndim)
    return pl.pallas_call(
        _lru_out_kernel,
        grid=(bsz, ng, nr),
        in_specs=[row, row, runs, row, xio, vec, full(gpost), full(w)],
        out_specs=xio,
        out_shape=jax.ShapeDtypeStruct(x5.shape, F32),
        compiler_params=_cparams("parallel", "parallel", "parallel"),
        name="lru_out",
    )(hsum, pb, cb, g, x5, gt, gpost, w)


def kernel(x, c, ctx, c_ctx, ada_w, ada_b, norm_pre, norm_post, gla_w_in, gla_wg_f, gla_bg_f, gla_wg_b, gla_bg_b, gla_norm, gla_w_out, lru_w_in, lru_conv_w, lru_conv_b, lru_wa_f, lru_ba_f, lru_wx_f, lru_bx_f, lru_lam_f, lru_wa_b, lru_ba_b, lru_wx_b, lru_bx_b, lru_lam_b, lru_w_out):
    bsz, seq, d = x.shape
    ctx_len = ctx.shape[1]
    rows = seq // GRID_W

    cvec = jnp.concatenate([c, c_ctx[None], jnp.zeros((8 - bsz - 1, d), F32)], axis=0)
    mod = _ada_modulation(cvec, ada_w, ada_b)

    def mods(i):
        lat = [mod[i, :bsz, None, j * d:(j + 1) * d] for j in range(3)]
        con = [jnp.broadcast_to(mod[i, bsz, None, None, j * d:(j + 1) * d], (bsz, 1, d)) for j in range(3)]
        return lat, con

    (sh, scl, gt), (sh_c, scl_c, gt_c) = mods(0)
    gpre, gpost = norm_pre[0][None], norm_post[0][None]
    lr_pad = 128 - 2 * GLA_GATE_RANK
    w_in = jnp.pad(gla_w_in[0], ((0, 0), (0, lr_pad))).astype(BF16)
    rk = GLA_GATE_RANK
    wg = (jnp.pad(gla_wg_f[0], ((0, 128 - rk), (0, 0))).astype(BF16),
          jnp.pad(gla_wg_b[0], ((rk, 128 - 2 * rk), (0, 0))).astype(BF16))
    bg = (gla_bg_f[0][None], gla_bg_b[0][None])
    gla_args = (gpre, gpost, w_in, wg, bg, gla_norm[0][None], gla_w_out[0].astype(BF16))
    s0 = jnp.zeros((bsz, GLA_HEADS, GLA_HEAD_V, GLA_HEAD_K), F32)
    ctx, s_f, s_b = _gla_layer(ctx, sh_c, scl_c, gt_c, *gla_args, s0, s0)
    x, _, _ = _gla_layer(x, sh, scl, gt, *gla_args, s_f, s_b)

    (sh, scl, gt), (sh_c, scl_c, _) = mods(1)
    gpre, gpost = norm_pre[1][None], norm_post[1][None]
    vec = lambda a: a[None]
    gate = lambda wgt, bias: (wgt.astype(BF16), vec(0.5 * bias))
    p_f = (*gate(lru_wa_f[0], lru_ba_f[0]), *gate(lru_wx_f[0], lru_bx_f[0]), vec(lru_lam_f[0]))
    p_b = (*gate(lru_wa_b[0], lru_ba_b[0]), *gate(lru_wx_b[0], lru_bx_b[0]), vec(lru_lam_b[0]))
    scan_args = (gpre, lru_w_in[0].astype(BF16), 0.5 * lru_conv_w[0], vec(0.5 * lru_conv_b[0]), p_f, p_b)
    h0 = jnp.zeros((bsz, 1, LRU_WIDTH), F32)
    cstep = ctx_len // N_SEG
    ctx5 = ctx.reshape(bsz, N_SEG, cstep, d).transpose(0, 2, 1, 3).reshape(bsz, cstep, 1, N_SEG, d)
    _, _, _, hbfin, pbfin, s_f = _lru_scan(ctx5, sh_c, scl_c, *scan_args, h0)
    _, s_b = _lru_carry(h0, hbfin, pbfin)
    x5 = x.reshape(bsz, rows, GRID_W // N_SEG, N_SEG, d)
    hsum, pb, g, hbfin, pbfin, _ = _lru_scan(x5, sh, scl, *scan_args, s_f)
    cb, _ = _lru_carry(s_b, hbfin, pbfin)
    out5 = _lru_out(hsum, pb, cb, g, x5, gt, gpost, lru_w_out[0].astype(BF16), rb=min(LRU_OUT_STEPS, rows))
    return out5.reshape(bsz, seq, d)
```

```python
import functools

import jax
import jax.numpy as jnp
from jax import lax
from jax.experimental import pallas as pl
from jax.experimental.pallas import tpu as pltpu

F32 = jnp.float32
BF16 = jnp.bfloat16

RMS_EPS = 1e-6
GRID_W = 64
GLA_HEADS = 4
GLA_HEAD_K = 128
GLA_HEAD_V = 256
GLA_DK = GLA_HEADS * GLA_HEAD_K
GLA_DV = GLA_HEADS * GLA_HEAD_V
GLA_GATE_RANK = 16
GLA_GATE_NORM = 16.0
GLA_CHUNK = 256
GLA_FACTORISED_MIN_CUM = -60.0
GLA_SUM_PIECE = 128
GLA_IN_ROWS = 1024
GLA_SCAN_ROWS = 1024
LRU_BLOCKS = 5
LRU_BLOCK_W = 256
LRU_WIDTH = LRU_BLOCKS * LRU_BLOCK_W
LRU_C = 8.0
CONV_W = 4
CONV_LEFT = CONV_W // 2
N_SEG = 8
LRU_OUT_STEPS = 128

VMEM_LIMIT_BYTES = 56 * 1024 * 1024


def _cparams(*sem, flags=None):
    return pltpu.CompilerParams(dimension_semantics=sem, vmem_limit_bytes=VMEM_LIMIT_BYTES, flags=flags)


def _silu(x):
    hx = 0.5 * x
    return hx * jnp.tanh(hx) + hx


def _sqrt(x):
    return x * lax.rsqrt(jnp.maximum(x, 1e-30))


def _log_sigmoid(x):
    return jnp.minimum(x, 0.0) - jnp.log(1.0 + jnp.exp(-jnp.abs(x)))


def _softplus(x):
    return jnp.maximum(x, 0.0) + jnp.log1p(jnp.exp(-jnp.abs(x)))


def _split_bf16(x):
    hi = x.astype(BF16)
    lo = (x - hi.astype(F32)).astype(BF16)
    return hi, lo


def _dot(a, b):
    return jnp.dot(a, b, preferred_element_type=F32)


def _dot_x3(a, b):
    a_hi, a_lo = _split_bf16(a)
    b_hi, b_lo = _split_bf16(b)
    return _dot(a_hi, b_hi) + (_dot(a_hi, b_lo) + _dot(a_lo, b_hi))


def _pre_norm_modulate(x, gpre, sh, scl):
    ms = jnp.mean(x * x, axis=-1, keepdims=True)
    return (x * lax.rsqrt(ms + RMS_EPS)) * (gpre * (1.0 + scl)) + sh


def _post_norm_residual(x, y, gpost, gt):
    ms = jnp.mean(y * y, axis=-1, keepdims=True)
    return x + (y * lax.rsqrt(ms + RMS_EPS)) * (gt * gpost)


def _ada_kernel(c_ref, w_ref, b_ref, o_ref):
    sc = _silu(c_ref[...])
    o_ref[...] = _dot_x3(sc, w_ref[...]) + b_ref[...]


def _ada_modulation(cvec, ada_w, ada_b):
    depth, d, n3 = ada_w.shape
    tn = 1024
    return pl.pallas_call(
        _ada_kernel,
        grid=(depth, n3 // tn),
        in_specs=[pl.BlockSpec((8, d), lambda i, j: (0, 0)),
                  pl.BlockSpec((None, d, tn), lambda i, j: (i, 0, j)),
                  pl.BlockSpec((None, 1, tn), lambda i, j: (i, 0, j))],
        out_specs=pl.BlockSpec((None, 8, tn), lambda i, j: (i, 0, j)),
        out_shape=jax.ShapeDtypeStruct((depth, 8, n3), F32),
        compiler_params=_cparams("parallel", "parallel"),
        name="ada_modulation",
    )(cvec, ada_w, ada_b.reshape(depth, 1, n3))


def _gla_log_decay_sums(lr, wg_ref, bg_ref, cum_ref, reverse):
    c, p = GLA_CHUNK, GLA_SUM_PIECE
    row = lax.broadcasted_iota(jnp.int32, (p, 2 * p), 0)
    col = lax.broadcasted_iota(jnp.int32, (p, 2 * p), 1) % p
    tri = ((row <= col) if reverse else (row >= col)).astype(BF16)
    z = _dot(lr, wg_ref[...]) + bg_ref[...]
    log_a = _log_sigmoid(z) * (1.0 / GLA_GATE_NORM)
    hi, lo = _split_bf16(log_a)
    for i in range(lr.shape[0] // c):
        carry = None
        pieces = range(c // p)
        for j in (reversed(pieces) if reverse else pieces):
            rows = slice(i * c + j * p, i * c + (j + 1) * p)
            s = _dot(tri, jnp.concatenate([hi[rows], lo[rows]], axis=0))
            if carry is not None:
                s = s + carry
            cum_ref[rows, :] = s
            carry = s[0:1] if reverse else s[p - 1:p]


def _gla_in_kernel(x_ref, sh_ref, scl_ref, gpre_ref, w_ref, wlr_ref, wgf_ref, bgf_ref, wgb_ref, bgb_ref,
                   q_ref, k_ref, v_ref, g_ref, cumf_ref, cumb_ref):
    hb16 = _pre_norm_modulate(x_ref[...], gpre_ref[...], sh_ref[...], scl_ref[...]).astype(BF16)
    lr = _dot(hb16, wlr_ref[...]).astype(BF16)
    proj = _dot(hb16, w_ref[...])
    _gla_log_decay_sums(lr, wgf_ref, bgf_ref, cumf_ref, reverse=False)
    _gla_log_decay_sums(lr, wgb_ref, bgb_ref, cumb_ref, reverse=True)
    q_ref[...] = (proj[:, :GLA_DK] * (GLA_HEAD_K ** -0.5)).astype(BF16)
    k_ref[...] = proj[:, GLA_DK:2 * GLA_DK].astype(BF16)
    v_ref[...] = proj[:, 2 * GLA_DK:2 * GLA_DK + GLA_DV].astype(BF16)
    g_ref[...] = proj[:, 2 * GLA_DK + GLA_DV:].astype(BF16)


def _gla_in(x, sh, scl, gpre, w, w_lr, wg, bg, tm):
    bsz, t, d = x.shape
    row = lambda wd: pl.BlockSpec((None, tm, wd), lambda b, i: (b, i, 0))
    vec = lambda wd: pl.BlockSpec((None, 1, wd), lambda b, i: (b, 0, 0))
    full = lambda a: pl.BlockSpec(a.shape, lambda b, i: (0,) * a.ndim)
    return pl.pallas_call(
        _gla_in_kernel,
        grid=(bsz, t // tm),
        in_specs=[row(d), vec(d), vec(d), full(gpre), full(w), full(w_lr),
                  full(wg[0]), full(bg[0]), full(wg[1]), full(bg[1])],
        out_specs=[row(GLA_DK), row(GLA_DK), row(GLA_DV), row(GLA_DV), row(GLA_DK), row(GLA_DK)],
        out_shape=[jax.ShapeDtypeStruct((bsz, t, GLA_DK), BF16),
                   jax.ShapeDtypeStruct((bsz, t, GLA_DK), BF16),
                   jax.ShapeDtypeStruct((bsz, t, GLA_DV), BF16),
                   jax.ShapeDtypeStruct((bsz, t, GLA_DV), BF16),
                   jax.ShapeDtypeStruct((bsz, t, GLA_DK), F32),
                   jax.ShapeDtypeStruct((bsz, t, GLA_DK), F32)],
        compiler_params=_cparams("parallel", "parallel"),
        name="gla_in",
    )(x, sh, scl, gpre, w, w_lr, wg[0], bg[0], wg[1], bg[1])


def _gla_scores_exact(q, k, cum, reverse, kf_s, cm_s, sc_s):
    c = q.shape[0]
    kf_s[...] = k
    cm_s[...] = cum
    sc_s[...] = jnp.zeros_like(sc_s)
    ridx = lax.broadcasted_iota(jnp.int32, (c, 1), 0)
    cidx = lax.broadcasted_iota(jnp.int32, (c, c), 1)

    def column(j, carry):
        kj = kf_s[pl.ds(j, 1), :]
        cj = cm_s[pl.ds(j, 1), :]
        live = (ridx <= j) if reverse else (ridx >= j)
        dec = jnp.exp(jnp.where(live, jnp.minimum(cum - cj, 0.0), -1e30))
        sj = jnp.sum(q * kj * dec, axis=-1, keepdims=True)
        sc_s[...] += jnp.where(cidx == j, sj, 0.0)
        return carry

    lax.fori_loop(0, c, column, 0)
    return sc_s[...]


def _gla_chunks(q_ref, k_ref, v_ref, cum_ref, s_ref, put_o, kf_s, cm_s, sc_s, *, reverse, factorised):
    tb = q_ref.shape[0]
    c = GLA_CHUNK
    row = lax.broadcasted_iota(jnp.int32, (c, c), 0)
    col = lax.broadcasted_iota(jnp.int32, (c, c), 1)
    keep = (row <= col) if reverse else (row >= col)
    chunks = range(tb // c)
    for i in (reversed(chunks) if reverse else chunks):
        rows = slice(i * c, (i + 1) * c)
        for hd in range(GLA_HEADS):
            kc = slice(hd * GLA_HEAD_K, (hd + 1) * GLA_HEAD_K)
            vc = slice(hd * GLA_HEAD_V, (hd + 1) * GLA_HEAD_V)
            cum = cum_ref[rows, kc]
            last = cum[0:1] if reverse else cum[c - 1:c]
            q = q_ref[rows, kc].astype(F32)
            k = k_ref[rows, kc].astype(F32)
            v = v_ref[rows, vc]
            qe = (q * jnp.exp(cum)).astype(BF16)
            kl = (k * jnp.exp(last - cum)).astype(BF16)
            st = s_ref[hd]
            if factorised:
                ke = (k * jnp.exp(-cum)).astype(BF16)
                scores = lax.dot_general(qe, ke, (((1,), (1,)), ((), ())), preferred_element_type=F32)
                scores = jnp.where(keep, scores, 0.0)
            else:
                scores = _gla_scores_exact(q, k, cum, reverse, kf_s, cm_s, sc_s)
            o = lax.dot_general(qe, st.astype(BF16), (((1,), (1,)), ((), ())),
                                preferred_element_type=F32)
            o = o + _dot(scores.astype(BF16), v)
            put_o(rows, vc, o)
            upd = lax.dot_general(v, kl, (((0,), (0,)), ((), ())), preferred_element_type=F32)
            s_ref[hd] = st * jnp.exp(last) + upd


def _gla_scan_kernel(*refs, reverse, fuse_out):
    if fuse_out:
        (q_ref, k_ref, v_ref, cum_ref, s0_ref, of_ref, g_ref, x_ref, gt_ref, gn_ref, gpost_ref, wout_ref,
         y_ref, sfin_ref, s_ref, kf_s, cm_s, sc_s, o_s) = refs
    else:
        q_ref, k_ref, v_ref, cum_ref, s0_ref, y_ref, sfin_ref, s_ref, kf_s, cm_s, sc_s = refs
    n = pl.program_id(1)

    @pl.when(n == 0)
    def _():
        s_ref[...] = s0_ref[...]

    if fuse_out:
        def put_o(rows, vc, o):
            o_s[rows, vc] = o + of_ref[rows, vc].astype(F32)
    else:
        def put_o(rows, vc, o):
            y_ref[rows, vc] = o.astype(BF16)

    c = GLA_CHUNK
    ends = [cum_ref[i * c:i * c + 1, :] if reverse else cum_ref[(i + 1) * c - 1:(i + 1) * c, :]
            for i in range(q_ref.shape[0] // c)]
    safe = jnp.min(functools.reduce(jnp.minimum, ends)) >= GLA_FACTORISED_MIN_CUM
    scan = functools.partial(_gla_chunks, q_ref, k_ref, v_ref, cum_ref, s_ref, put_o, kf_s, cm_s, sc_s,
                             reverse=reverse)
    pl.when(safe)(functools.partial(scan, factorised=True))
    pl.when(jnp.logical_not(safe))(functools.partial(scan, factorised=False))

    if fuse_out:
        gn = gn_ref[...]
        parts = []
        for hd in range(GLA_HEADS):
            oh = o_s[:, hd * GLA_HEAD_V:(hd + 1) * GLA_HEAD_V]
            ms = jnp.mean(oh * oh, axis=-1, keepdims=True)
            parts.append(oh * lax.rsqrt(ms + RMS_EPS) * gn)
        on = jnp.concatenate(parts, axis=-1) * _silu(g_ref[...].astype(F32))
        y = _dot(on.astype(BF16), wout_ref[...])
        y_ref[...] = _post_norm_residual(x_ref[...], y, gpost_ref[...], gt_ref[...])

    @pl.when(n == pl.num_programs(1) - 1)
    def _():
        sfin_ref[...] = s_ref[...]


def _gla_scan(q, k, v, cum, s0, tb, reverse, out_args=None):
    bsz, t, _ = q.shape
    nblk = t // tb
    c = GLA_CHUNK
    pos = (lambda n: nblk - 1 - n) if reverse else (lambda n: n)
    row = lambda wd: pl.BlockSpec((None, tb, wd), lambda b, n: (b, pos(n), 0))
    vec = lambda wd: pl.BlockSpec((None, 1, wd), lambda b, n: (b, 0, 0))
    full = lambda a: pl.BlockSpec(a.shape, lambda b, n: (0,) * a.ndim)
    st = pl.BlockSpec((None, GLA_HEADS, GLA_HEAD_V, GLA_HEAD_K), lambda b, n: (b, 0, 0, 0))
    in_specs = [row(GLA_DK), row(GLA_DK), row(GLA_DV), row(GLA_DK), st]
    scratch = [pltpu.VMEM((GLA_HEADS, GLA_HEAD_V, GLA_HEAD_K), F32),
               pltpu.VMEM((c, GLA_HEAD_K), F32), pltpu.VMEM((c, GLA_HEAD_K), F32), pltpu.VMEM((c, c), F32)]
    args = [q, k, v, cum, s0]
    if out_args is None:
        y_spec, y_shape = row(GLA_DV), jax.ShapeDtypeStruct((bsz, t, GLA_DV), BF16)
    else:
        o_other, g, x, gt, gn, gpost, w_out = out_args
        d = x.shape[-1]
        in_specs += [row(GLA_DV), row(GLA_DV), row(d), vec(d), full(gn), full(gpost), full(w_out)]
        args += [o_other, g, x, gt, gn, gpost, w_out]
        scratch.append(pltpu.VMEM((tb, GLA_DV), F32))
        y_spec, y_shape = row(d), jax.ShapeDtypeStruct((bsz, t, d), F32)
    return pl.pallas_call(
        functools.partial(_gla_scan_kernel, reverse=reverse, fuse_out=out_args is not None),
        grid=(bsz, nblk),
        in_specs=in_specs,
        out_specs=[y_spec, st],
        out_shape=[y_shape, jax.ShapeDtypeStruct(s0.shape, F32)],
        scratch_shapes=scratch,
        compiler_params=_cparams("parallel", "arbitrary"),
        name="gla_scan_bwd" if reverse else "gla_scan_fwd",
    )(*args)


def _gla_layer(x, sh, scl, gt, gpre, gpost, w_in, w_lr, wg, bg, gn, w_out, s0_f, s0_b):
    t = x.shape[1]
    q, k, v, g, cum_f, cum_b = _gla_in(x, sh, scl, gpre, w_in, w_lr, wg, bg, min(GLA_IN_ROWS, t))
    tb = min(GLA_SCAN_ROWS, t)
    o_f, s_f = _gla_scan(q, k, v, cum_f, s0_f, tb, reverse=False)
    x_new, s_b = _gla_scan(q, k, v, cum_b, s0_b, tb, reverse=True,
                           out_args=(o_f, g, x, gt, gn, gpost, w_out))
    return x_new, s_f, s_b


def _lru_scan_kernel(x_ref, xp_ref, xn_ref, sh_ref, scl_ref, gpre_ref, w_ref, cw_ref, cb_ref,
                     waf_ref, baf_ref, wxf_ref, bxf_ref, lamf_ref,
                     wab_ref, bab_ref, wxb_ref, bxb_ref, lamb_ref, h0_ref,
                     hsum_ref, pb_ref, g_ref, hbfin_ref, pbfin_ref, ffin_ref,
                     zext_s, hf_s, pf_s, hb_s, pbk_s, carry_s, cin_s, *pre_s):
    g = pl.program_id(1)
    ng = pl.num_programs(1)
    r, s, d = x_ref.shape
    rows = r * s
    w = LRU_WIDTH
    halo_l = CONV_LEFT * s
    n_right = CONV_W - 1 - CONV_LEFT

    @pl.when(g == 0)
    def _():
        cin_s[...] = h0_ref[...]

    sh, scl, gpre = sh_ref[...], scl_ref[...], gpre_ref[...]
    hb16 = _pre_norm_modulate(x_ref[...].reshape(rows, d), gpre, sh, scl).astype(BF16)
    xh = jnp.concatenate([xp_ref[...].reshape(halo_l, d), xn_ref[...].reshape(n_right * s, d)], axis=0)
    zh = _dot(_pre_norm_modulate(xh, gpre, sh, scl).astype(BF16), w_ref[:, :w])
    sub = lax.broadcasted_iota(jnp.int32, (s, LRU_BLOCK_W), 0)

    def in_proj(nb):
        cs = slice(nb * LRU_BLOCK_W, (nb + 1) * LRU_BLOCK_W)
        zext_s[halo_l:halo_l + rows, cs] = _dot(hb16, w_ref[:, cs])
        g_ref[:, cs] = _dot(hb16, w_ref[:, w + nb * LRU_BLOCK_W:w + (nb + 1) * LRU_BLOCK_W]).astype(BF16)
        for j in range(CONV_LEFT):
            src = halo_l + (r - CONV_LEFT + j) * s
            inner = pltpu.roll(zext_s[src:src + s, cs], 1, 0)
            edge = jnp.where(g == 0, 0.0, pltpu.roll(zh[j * s:(j + 1) * s, cs], 1, 0))
            zext_s[j * s:(j + 1) * s, cs] = jnp.where(sub == 0, edge, inner)
        for j in range(n_right):
            src = halo_l + j * s
            inner = pltpu.roll(zext_s[src:src + s, cs], s - 1, 0)
            edge = jnp.where(g == ng - 1, 0.0,
                             pltpu.roll(zh[halo_l + j * s:halo_l + (j + 1) * s, cs], s - 1, 0))
            dst = halo_l + rows + j * s
            zext_s[dst:dst + s, cs] = jnp.where(sub == s - 1, edge, inner)

    def gate_matmuls(nb):
        cs = slice(nb * LRU_BLOCK_W, (nb + 1) * LRU_BLOCK_W)
        hz = cb_ref[:, cs]
        for j in range(CONV_W):
            hz = hz + cw_ref[j:j + 1, cs] * zext_s[j * s:j * s + rows, cs]
        hzb = hz.astype(BF16)
        prf_s, pif_s, prb_s, pib_s, hzc_s = pre_s[5 * (nb % 2):5 * (nb % 2) + 5]
        prf_s[...] = _dot(hzb, waf_ref[nb]) + baf_ref[:, cs]
        pif_s[...] = _dot(hzb, wxf_ref[nb]) + bxf_ref[:, cs]
        prb_s[...] = _dot(hzb, wab_ref[nb]) + bab_ref[:, cs]
        pib_s[...] = _dot(hzb, wxb_ref[nb]) + bxb_ref[:, cs]
        hzc_s[...] = hz

    in_proj(0)
    gate_matmuls(0)
    for nb in range(LRU_BLOCKS):
        cs = slice(nb * LRU_BLOCK_W, (nb + 1) * LRU_BLOCK_W)
        if nb + 1 < LRU_BLOCKS:
            in_proj(nb + 1)
            gate_matmuls(nb + 1)
        prf_s, pif_s, prb_s, pib_s, hzc_s = pre_s[5 * (nb % 2):5 * (nb % 2) + 5]
        c1f = jnp.broadcast_to((-0.5 * LRU_C) * _softplus(-lamf_ref[:, cs]), (s, LRU_BLOCK_W))
        c1b = jnp.broadcast_to((-0.5 * LRU_C) * _softplus(-lamb_ref[:, cs]), (s, LRU_BLOCK_W))

        def gate(pr_s, pi_s, c1, rws):
            tr = jnp.tanh(pr_s[rws, :])
            ti = jnp.tanh(pi_s[rws, :])
            hz = hzc_s[rws, :]
            a = jnp.exp(c1 * tr + c1)
            return a, _sqrt(1.0 - a * a) * (hz * ti + hz)

        hf = hb = jnp.zeros((s, LRU_BLOCK_W), F32)
        pf = pb = jnp.ones((s, LRU_BLOCK_W), F32)
        for t in range(r):
            rf = slice(t * s, (t + 1) * s)
            rb = slice((r - 1 - t) * s, (r - t) * s)
            af, uf = gate(prf_s, pif_s, c1f, rf)
            ab, ub = gate(prb_s, pib_s, c1b, rb)
            hf = af * hf + uf
            pf = af * pf
            hb = ab * hb + ub
            pb = ab * pb
            hf_s[rf, :] = hf
            pf_s[rf, :] = pf
            hb_s[rb, :] = hb
            pbk_s[rb, :] = pb

        cin = cin_s[:, cs]
        for c in range(s):
            carry_s[c:c + 1, :] = cin
            cin = hf[c:c + 1] + pf[c:c + 1] * cin
        cin_s[:, cs] = cin
        carry = carry_s[...][None]
        piece = min(r, 16)
        for i in range(r // piece):
            rws = slice(i * piece * s, (i + 1) * piece * s)
            shape3 = (piece, s, LRU_BLOCK_W)
            hsum = (hf_s[rws, :].reshape(shape3) + pf_s[rws, :].reshape(shape3) * carry
                    + hb_s[rws, :].reshape(shape3))
            hsum_ref[rws, cs] = hsum.reshape(piece * s, LRU_BLOCK_W).astype(BF16)
            pb_ref[rws, cs] = pbk_s[rws, :].astype(BF16)
        hbfin_ref[:, cs] = hb
        pbfin_ref[:, cs] = pb

    ffin_ref[...] = cin_s[...]


def _lru_scan(x5, sh, scl, gpre, w_in, conv_w, conv_b, p_f, p_b, h0_f):
    bsz, r, ng, s, d = x5.shape
    rows = r * s
    w = LRU_WIDTH
    n_right = CONV_W - 1 - CONV_LEFT
    once = pl.Buffered(1)
    cur = pl.BlockSpec((None, r, None, s, d), lambda b, g: (b, 0, g, 0, 0))
    prev = pl.BlockSpec((None, CONV_LEFT, None, s, d),
                        lambda b, g: (b, r // CONV_LEFT - 1, jnp.maximum(g - 1, 0), 0, 0))
    nxt = pl.BlockSpec((None, n_right, None, s, d), lambda b, g: (b, 0, jnp.minimum(g + 1, ng - 1), 0, 0))
    vec = lambda wd: pl.BlockSpec((None, 1, wd), lambda b, g: (b, 0, 0))
    full = lambda a: pl.BlockSpec(a.shape, lambda b, g: (0,) * a.ndim, pipeline_mode=once)
    row = pl.BlockSpec((None, rows, w), lambda b, g: (b, g, 0))
    fin = pl.BlockSpec((None, None, s, w), lambda b, g: (b, g, 0, 0))
    blk = lambda: pltpu.VMEM((rows, LRU_BLOCK_W), F32)
    return pl.pallas_call(
        _lru_scan_kernel,
        grid=(bsz, ng),
        in_specs=[cur, prev, nxt, vec(d), vec(d), full(gpre), full(w_in), full(conv_w), full(conv_b)]
                 + [full(a) for a in p_f] + [full(a) for a in p_b] + [vec(w)],
        out_specs=[row, row, row, fin, fin, vec(w)],
        out_shape=[jax.ShapeDtypeStruct((bsz, ng * rows, w), BF16),
                   jax.ShapeDtypeStruct((bsz, ng * rows, w), BF16),
                   jax.ShapeDtypeStruct((bsz, ng * rows, w), BF16),
                   jax.ShapeDtypeStruct((bsz, ng, s, w), F32),
                   jax.ShapeDtypeStruct((bsz, ng, s, w), F32),
                   jax.ShapeDtypeStruct((bsz, 1, w), F32)],
        scratch_shapes=[pltpu.VMEM((rows + (CONV_W - 1) * s, w), F32)] + [blk() for _ in range(4)]
                       + [pltpu.VMEM((s, LRU_BLOCK_W), F32), pltpu.VMEM((1, w), F32)]
                       + [blk() for _ in range(10)],
        compiler_params=_cparams("parallel", "arbitrary"),
        name="lru_scan",
    )(x5, x5, x5, sh, scl, gpre, w_in, conv_w, conv_b, *p_f, *p_b, h0_f)


def _lru_carry_kernel(h0_ref, hfin_ref, pfin_ref, carry_ref, final_ref):
    c = h0_ref[...]
    for k in range(hfin_ref.shape[0] - 1, -1, -1):
        carry_ref[k:k + 1, :] = c
        c = hfin_ref[k:k + 1, :] + pfin_ref[k:k + 1, :] * c
    final_ref[...] = c


def _lru_carry(h0, hfin, pfin):
    bsz, ng, s, w = hfin.shape
    vec = pl.BlockSpec((None, 1, w), lambda b: (b, 0, 0))
    runs = pl.BlockSpec((None, ng * s, w), lambda b: (b, 0, 0))
    carry, final = pl.pallas_call(
        _lru_carry_kernel,
        grid=(bsz,),
        in_specs=[vec, runs, runs],
        out_specs=[runs, vec],
        out_shape=[jax.ShapeDtypeStruct((bsz, ng * s, w), F32), jax.ShapeDtypeStruct((bsz, 1, w), F32)],
        compiler_params=_cparams("parallel"),
        name="lru_carry",
    )(h0, hfin.reshape(bsz, ng * s, w), pfin.reshape(bsz, ng * s, w))
    return carry.reshape(bsz, ng, s, w), final


def _lru_out_kernel(hsum_ref, pb_ref, cb_ref, g_ref, x_ref, gt_ref, gpost_ref, w_ref, o_ref):
    r, s, d = x_ref.shape
    w = hsum_ref.shape[-1]
    h = hsum_ref[...].astype(F32).reshape(r, s, w) + pb_ref[...].astype(F32).reshape(r, s, w) * cb_ref[...][None]
    y = _dot((h.reshape(r * s, w) * _silu(g_ref[...].astype(F32))).astype(BF16), w_ref[...])
    x = x_ref[...].reshape(r * s, d)
    o_ref[...] = _post_norm_residual(x, y, gpost_ref[...], gt_ref[...]).reshape(r, s, d)


def _lru_out(hsum, pb, cb, g, x5, gt, gpost, w, rb):
    bsz, r, ng, s, d = x5.shape
    nr = r // rb
    tm = rb * s
    width = hsum.shape[-1]
    xio = pl.BlockSpec((None, rb, None, s, d), lambda b, g, i: (b, i, g, 0, 0))
    row = pl.BlockSpec((None, tm, width), lambda b, g, i: (b, g * nr + i, 0))
    runs = pl.BlockSpec((None, None, s, width), lambda b, g, i: (b, g, 0, 0))
    vec = pl.BlockSpec((None, 1, d), lambda b, g, i: (b, 0, 0))
    full = lambda a: pl.BlockSpec(a.shape, lambda b, g, i: (0,) * a.ndim)
    return pl.pallas_call(
        _lru_out_kernel,
        grid=(bsz, ng, nr),
        in_specs=[row, row, runs, row, xio, vec, full(gpost), full(w)],
        out_specs=xio,
        out_shape=jax.ShapeDtypeStruct(x5.shape, F32),
        compiler_params=_cparams("parallel", "parallel", "parallel"),
        name="lru_out",
    )(hsum, pb, cb, g, x5, gt, gpost, w)


def kernel(x, c, ctx, c_ctx, ada_w, ada_b, norm_pre, norm_post, gla_w_in, gla_wg_f, gla_bg_f, gla_wg_b, gla_bg_b, gla_norm, gla_w_out, lru_w_in, lru_conv_w, lru_conv_b, lru_wa_f, lru_ba_f, lru_wx_f, lru_bx_f, lru_lam_f, lru_wa_b, lru_ba_b, lru_wx_b, lru_bx_b, lru_lam_b, lru_w_out):
    bsz, seq, d = x.shape
    ctx_len = ctx.shape[1]
    rows = seq // GRID_W

    cvec = jnp.concatenate([c, c_ctx[None], jnp.zeros((8 - bsz - 1, d), F32)], axis=0)
    mod = _ada_modulation(cvec, ada_w, ada_b)

    def mods(i):
        lat = [mod[i, :bsz, None, j * d:(j + 1) * d] for j in range(3)]
        con = [jnp.broadcast_to(mod[i, bsz, None, None, j * d:(j + 1) * d], (bsz, 1, d)) for j in range(3)]
        return lat, con

    (sh, scl, gt), (sh_c, scl_c, gt_c) = mods(0)
    gpre, gpost = norm_pre[0][None], norm_post[0][None]
    rk = GLA_GATE_RANK
    n_main = 2 * GLA_DK + 2 * GLA_DV
    w_in = gla_w_in[0][:, :n_main].astype(BF16)
    w_lr = jnp.pad(gla_w_in[0][:, n_main:], ((0, 0), (0, 128 - 2 * rk))).astype(BF16)
    wg = (jnp.pad(gla_wg_f[0], ((0, 128 - rk), (0, 0))).astype(BF16),
          jnp.pad(gla_wg_b[0], ((rk, 128 - 2 * rk), (0, 0))).astype(BF16))
    bg = (gla_bg_f[0][None], gla_bg_b[0][None])
    gla_args = (gpre, gpost, w_in, w_lr, wg, bg, gla_norm[0][None], gla_w_out[0].astype(BF16))
    s0 = jnp.zeros((bsz, GLA_HEADS, GLA_HEAD_V, GLA_HEAD_K), F32)
    ctx, s_f, s_b = _gla_layer(ctx, sh_c, scl_c, gt_c, *gla_args, s0, s0)
    x, _, _ = _gla_layer(x, sh, scl, gt, *gla_args, s_f, s_b)

    (sh, scl, gt), (sh_c, scl_c, _) = mods(1)
    gpre, gpost = norm_pre[1][None], norm_post[1][None]
    vec = lambda a: a[None]
    gate = lambda wgt, bias: (wgt.astype(BF16), vec(0.5 * bias))
    p_f = (*gate(lru_wa_f[0], lru_ba_f[0]), *gate(lru_wx_f[0], lru_bx_f[0]), vec(lru_lam_f[0]))
    p_b = (*gate(lru_wa_b[0], lru_ba_b[0]), *gate(lru_wx_b[0], lru_bx_b[0]), vec(lru_lam_b[0]))
    scan_args = (gpre, lru_w_in[0].astype(BF16), 0.5 * lru_conv_w[0], vec(0.5 * lru_conv_b[0]), p_f, p_b)
    h0 = jnp.zeros((bsz, 1, LRU_WIDTH), F32)
    cstep = ctx_len // N_SEG
    ctx5 = ctx.reshape(bsz, N_SEG, cstep, d).transpose(0, 2, 1, 3).reshape(bsz, cstep, 1, N_SEG, d)
    _, _, _, hbfin, pbfin, s_f = _lru_scan(ctx5, sh_c, scl_c, *scan_args, h0)
    _, s_b = _lru_carry(h0, hbfin, pbfin)
    x5 = x.reshape(bsz, rows, GRID_W // N_SEG, N_SEG, d)
    hsum, pb, g, hbfin, pbfin, _ = _lru_scan(x5, sh, scl, *scan_args, s_f)
    cb, _ = _lru_carry(s_b, hbfin, pbfin)
    out5 = _lru_out(hsum, pb, cb, g, x5, gt, gpost, lru_w_out[0].astype(BF16), rb=min(LRU_OUT_STEPS, rows))
    return out5.reshape(bsz, seq, d)
```

```python
import functools

import jax
import jax.numpy as jnp
from jax import lax
from jax.experimental import pallas as pl
from jax.experimental.pallas import tpu as pltpu

F32 = jnp.float32
BF16 = jnp.bfloat16

RMS_EPS = 1e-6
LOG2_E = 1.4426950408889634
GRID_W = 64
GLA_HEADS = 4
GLA_HEAD_K = 128
GLA_HEAD_V = 256
GLA_DK = GLA_HEADS * GLA_HEAD_K
GLA_DV = GLA_HEADS * GLA_HEAD_V
GLA_GATE_RANK = 16
GLA_GATE_NORM = 16.0
GLA_CHUNK = 256
GLA_FACTORISED_MIN_CUM = -60.0
GLA_SUM_PIECE = 128
GLA_IN_ROWS = 1024
GLA_SCAN_ROWS = 1024
LRU_BLOCKS = 5
LRU_BLOCK_W = 256
LRU_WIDTH = LRU_BLOCKS * LRU_BLOCK_W
LRU_C = 8.0
CONV_W = 4
CONV_LEFT = CONV_W // 2
N_SEG = 8
LRU_OUT_STEPS = 128

VMEM_LIMIT_BYTES = 56 * 1024 * 1024


def _cparams(*sem):
    return pltpu.CompilerParams(dimension_semantics=sem, vmem_limit_bytes=VMEM_LIMIT_BYTES)


def _silu(x):
    hx = 0.5 * x
    return hx * jnp.tanh(hx) + hx


def _sqrt(x):
    return x * lax.rsqrt(jnp.maximum(x, 1e-30))


def _log_sigmoid(x):
    return jnp.minimum(x, 0.0) - jnp.log(1.0 + jnp.exp(-jnp.abs(x)))


def _softplus(x):
    return jnp.maximum(x, 0.0) + jnp.log1p(jnp.exp(-jnp.abs(x)))


def _split_bf16(x):
    hi = x.astype(BF16)
    lo = (x - hi.astype(F32)).astype(BF16)
    return hi, lo


def _dot(a, b):
    return jnp.dot(a, b, preferred_element_type=F32)


def _dot_x3(a, b):
    a_hi, a_lo = _split_bf16(a)
    b_hi, b_lo = _split_bf16(b)
    return _dot(a_hi, b_hi) + (_dot(a_hi, b_lo) + _dot(a_lo, b_hi))


def _pre_norm_modulate(x, gpre, sh, scl):
    ms = jnp.mean(x * x, axis=-1, keepdims=True)
    return (x * lax.rsqrt(ms + RMS_EPS)) * (gpre * (1.0 + scl)) + sh


def _post_norm_residual(x, y, gpost, gt):
    ms = jnp.mean(y * y, axis=-1, keepdims=True)
    return x + (y * lax.rsqrt(ms + RMS_EPS)) * (gt * gpost)


def _ada_kernel(c_ref, w_ref, b_ref, o_ref):
    sc = _silu(c_ref[...])
    o_ref[...] = _dot_x3(sc, w_ref[...]) + b_ref[...]


def _ada_modulation(cvec, ada_w, ada_b):
    depth, d, n3 = ada_w.shape
    tn = 1024
    return pl.pallas_call(
        _ada_kernel,
        grid=(depth, n3 // tn),
        in_specs=[pl.BlockSpec((8, d), lambda i, j: (0, 0)),
                  pl.BlockSpec((None, d, tn), lambda i, j: (i, 0, j)),
                  pl.BlockSpec((None, 1, tn), lambda i, j: (i, 0, j))],
        out_specs=pl.BlockSpec((None, 8, tn), lambda i, j: (i, 0, j)),
        out_shape=jax.ShapeDtypeStruct((depth, 8, n3), F32),
        compiler_params=_cparams("parallel", "parallel"),
        name="ada_modulation",
    )(cvec, ada_w, ada_b.reshape(depth, 1, n3))


def _gla_log_decay_sums(lr, wg_ref, bg_ref, cum_ref, reverse):
    c, p = GLA_CHUNK, GLA_SUM_PIECE
    row = lax.broadcasted_iota(jnp.int32, (p, 2 * p), 0)
    col = lax.broadcasted_iota(jnp.int32, (p, 2 * p), 1) % p
    tri = ((row <= col) if reverse else (row >= col)).astype(BF16)
    z = _dot(lr, wg_ref[...]) + bg_ref[...]
    log_a = _log_sigmoid(z) * (1.0 / GLA_GATE_NORM)
    hi, lo = _split_bf16(log_a)
    for i in range(lr.shape[0] // c):
        carry = None
        pieces = range(c // p)
        for j in (reversed(pieces) if reverse else pieces):
            rows = slice(i * c + j * p, i * c + (j + 1) * p)
            s = _dot(tri, jnp.concatenate([hi[rows], lo[rows]], axis=0))
            if carry is not None:
                s = s + carry
            cum_ref[rows, :] = s
            carry = s[0:1] if reverse else s[p - 1:p]


def _gla_in_kernel(x_ref, sh_ref, scl_ref, gpre_ref, w_ref, wlr_ref, wgf_ref, bgf_ref, wgb_ref, bgb_ref,
                   q_ref, k_ref, v_ref, g_ref, cumf_ref, cumb_ref):
    hb16 = _pre_norm_modulate(x_ref[...], gpre_ref[...], sh_ref[...], scl_ref[...]).astype(BF16)
    lr = _dot(hb16, wlr_ref[...]).astype(BF16)
    proj = _dot(hb16, w_ref[...])
    _gla_log_decay_sums(lr, wgf_ref, bgf_ref, cumf_ref, reverse=False)
    _gla_log_decay_sums(lr, wgb_ref, bgb_ref, cumb_ref, reverse=True)
    q_ref[...] = (proj[:, :GLA_DK] * (GLA_HEAD_K ** -0.5)).astype(BF16)
    k_ref[...] = proj[:, GLA_DK:2 * GLA_DK].astype(BF16)
    v_ref[...] = proj[:, 2 * GLA_DK:2 * GLA_DK + GLA_DV].astype(BF16)
    g_ref[...] = proj[:, 2 * GLA_DK + GLA_DV:].astype(BF16)


def _gla_in(x, sh, scl, gpre, w, w_lr, wg, bg, tm):
    bsz, t, d = x.shape
    row = lambda wd: pl.BlockSpec((None, tm, wd), lambda b, i: (b, i, 0))
    vec = lambda wd: pl.BlockSpec((None, 1, wd), lambda b, i: (b, 0, 0))
    full = lambda a: pl.BlockSpec(a.shape, lambda b, i: (0,) * a.ndim)
    return pl.pallas_call(
        _gla_in_kernel,
        grid=(bsz, t // tm),
        in_specs=[row(d), vec(d), vec(d), full(gpre), full(w), full(w_lr),
                  full(wg[0]), full(bg[0]), full(wg[1]), full(bg[1])],
        out_specs=[row(GLA_DK), row(GLA_DK), row(GLA_DV), row(GLA_DV), row(GLA_DK), row(GLA_DK)],
        out_shape=[jax.ShapeDtypeStruct((bsz, t, GLA_DK), BF16),
                   jax.ShapeDtypeStruct((bsz, t, GLA_DK), BF16),
                   jax.ShapeDtypeStruct((bsz, t, GLA_DV), BF16),
                   jax.ShapeDtypeStruct((bsz, t, GLA_DV), BF16),
                   jax.ShapeDtypeStruct((bsz, t, GLA_DK), F32),
                   jax.ShapeDtypeStruct((bsz, t, GLA_DK), F32)],
        compiler_params=_cparams("parallel", "parallel"),
        name="gla_in",
    )(x, sh, scl, gpre, w, w_lr, wg[0], bg[0], wg[1], bg[1])


def _gla_scores_exact(q, k, cum, reverse, kf_s, cm_s, sc_s):
    c = q.shape[0]
    kf_s[...] = k
    cm_s[...] = cum
    sc_s[...] = jnp.zeros_like(sc_s)
    ridx = lax.broadcasted_iota(jnp.int32, (c, 1), 0)
    cidx = lax.broadcasted_iota(jnp.int32, (c, c), 1)

    def column(j, carry):
        kj = kf_s[pl.ds(j, 1), :]
        cj = cm_s[pl.ds(j, 1), :]
        live = (ridx <= j) if reverse else (ridx >= j)
        dec = jnp.exp(jnp.where(live, jnp.minimum(cum - cj, 0.0), -1e30))
        sj = jnp.sum(q * kj * dec, axis=-1, keepdims=True)
        sc_s[...] += jnp.where(cidx == j, sj, 0.0)
        return carry

    lax.fori_loop(0, c, column, 0)
    return sc_s[...]


def _gla_chunks(q_ref, k_ref, v_ref, cum_ref, s_ref, put_o, kf_s, cm_s, sc_s, *, reverse, factorised):
    tb = q_ref.shape[0]
    c = GLA_CHUNK
    row = lax.broadcasted_iota(jnp.int32, (c, c), 0)
    col = lax.broadcasted_iota(jnp.int32, (c, c), 1)
    keep = (row <= col) if reverse else (row >= col)
    chunks = range(tb // c)
    for i in (reversed(chunks) if reverse else chunks):
        rows = slice(i * c, (i + 1) * c)
        for hd in range(GLA_HEADS):
            kc = slice(hd * GLA_HEAD_K, (hd + 1) * GLA_HEAD_K)
            vc = slice(hd * GLA_HEAD_V, (hd + 1) * GLA_HEAD_V)
            cum = cum_ref[rows, kc]
            last = cum[0:1] if reverse else cum[c - 1:c]
            q = q_ref[rows, kc].astype(F32)
            k = k_ref[rows, kc].astype(F32)
            v = v_ref[rows, vc]
            qe = (q * jnp.exp(cum)).astype(BF16)
            kl = (k * jnp.exp(last - cum)).astype(BF16)
            st = s_ref[hd]
            if factorised:
                ke = (k * jnp.exp(-cum)).astype(BF16)
                scores = lax.dot_general(qe, ke, (((1,), (1,)), ((), ())), preferred_element_type=F32)
                scores = jnp.where(keep, scores, 0.0)
            else:
                scores = _gla_scores_exact(q, k, cum, reverse, kf_s, cm_s, sc_s)
            o = lax.dot_general(qe, st.astype(BF16), (((1,), (1,)), ((), ())),
                                preferred_element_type=F32)
            o = o + _dot(scores.astype(BF16), v)
            put_o(rows, vc, o)
            upd = lax.dot_general(v, kl, (((0,), (0,)), ((), ())), preferred_element_type=F32)
            s_ref[hd] = st * jnp.exp(last) + upd


def _gla_scan_kernel(*refs, reverse, fuse_out):
    if fuse_out:
        (q_ref, k_ref, v_ref, cum_ref, s0_ref, of_ref, g_ref, x_ref, gt_ref, gn_ref, gpost_ref, wout_ref,
         y_ref, sfin_ref, s_ref, kf_s, cm_s, sc_s, o_s) = refs
    else:
        q_ref, k_ref, v_ref, cum_ref, s0_ref, y_ref, sfin_ref, s_ref, kf_s, cm_s, sc_s = refs
    n = pl.program_id(1)

    @pl.when(n == 0)
    def _():
        s_ref[...] = s0_ref[...]

    if fuse_out:
        def put_o(rows, vc, o):
            o_s[rows, vc] = o + of_ref[rows, vc].astype(F32)
    else:
        def put_o(rows, vc, o):
            y_ref[rows, vc] = o.astype(BF16)

    c = GLA_CHUNK
    ends = [cum_ref[i * c:i * c + 1, :] if reverse else cum_ref[(i + 1) * c - 1:(i + 1) * c, :]
            for i in range(q_ref.shape[0] // c)]
    safe = jnp.min(functools.reduce(jnp.minimum, ends)) >= GLA_FACTORISED_MIN_CUM
    scan = functools.partial(_gla_chunks, q_ref, k_ref, v_ref, cum_ref, s_ref, put_o, kf_s, cm_s, sc_s,
                             reverse=reverse)
    pl.when(safe)(functools.partial(scan, factorised=True))
    pl.when(jnp.logical_not(safe))(functools.partial(scan, factorised=False))

    if fuse_out:
        gn = gn_ref[...]
        parts = []
        for hd in range(GLA_HEADS):
            oh = o_s[:, hd * GLA_HEAD_V:(hd + 1) * GLA_HEAD_V]
            ms = jnp.mean(oh * oh, axis=-1, keepdims=True)
            parts.append(oh * lax.rsqrt(ms + RMS_EPS) * gn)
        on = jnp.concatenate(parts, axis=-1) * _silu(g_ref[...].astype(F32))
        y = _dot(on.astype(BF16), wout_ref[...])
        y_ref[...] = _post_norm_residual(x_ref[...], y, gpost_ref[...], gt_ref[...])

    @pl.when(n == pl.num_programs(1) - 1)
    def _():
        sfin_ref[...] = s_ref[...]


def _gla_scan(q, k, v, cum, s0, tb, reverse, out_args=None):
    bsz, t, _ = q.shape
    nblk = t // tb
    c = GLA_CHUNK
    pos = (lambda n: nblk - 1 - n) if reverse else (lambda n: n)
    row = lambda wd: pl.BlockSpec((None, tb, wd), lambda b, n: (b, pos(n), 0))
    vec = lambda wd: pl.BlockSpec((None, 1, wd), lambda b, n: (b, 0, 0))
    full = lambda a: pl.BlockSpec(a.shape, lambda b, n: (0,) * a.ndim)
    st = pl.BlockSpec((None, GLA_HEADS, GLA_HEAD_V, GLA_HEAD_K), lambda b, n: (b, 0, 0, 0))
    in_specs = [row(GLA_DK), row(GLA_DK), row(GLA_DV), row(GLA_DK), st]
    scratch = [pltpu.VMEM((GLA_HEADS, GLA_HEAD_V, GLA_HEAD_K), F32),
               pltpu.VMEM((c, GLA_HEAD_K), F32), pltpu.VMEM((c, GLA_HEAD_K), F32), pltpu.VMEM((c, c), F32)]
    args = [q, k, v, cum, s0]
    if out_args is None:
        y_spec, y_shape = row(GLA_DV), jax.ShapeDtypeStruct((bsz, t, GLA_DV), BF16)
    else:
        o_other, g, x, gt, gn, gpost, w_out = out_args
        d = x.shape[-1]
        in_specs += [row(GLA_DV), row(GLA_DV), row(d), vec(d), full(gn), full(gpost), full(w_out)]
        args += [o_other, g, x, gt, gn, gpost, w_out]
        scratch.append(pltpu.VMEM((tb, GLA_DV), F32))
        y_spec, y_shape = row(d), jax.ShapeDtypeStruct((bsz, t, d), F32)
    return pl.pallas_call(
        functools.partial(_gla_scan_kernel, reverse=reverse, fuse_out=out_args is not None),
        grid=(bsz, nblk),
        in_specs=in_specs,
        out_specs=[y_spec, st],
        out_shape=[y_shape, jax.ShapeDtypeStruct(s0.shape, F32)],
        scratch_shapes=scratch,
        compiler_params=_cparams("parallel", "arbitrary"),
        name="gla_scan_bwd" if reverse else "gla_scan_fwd",
    )(*args)


def _gla_layer(x, sh, scl, gt, gpre, gpost, w_in, w_lr, wg, bg, gn, w_out, s0_f, s0_b):
    t = x.shape[1]
    q, k, v, g, cum_f, cum_b = _gla_in(x, sh, scl, gpre, w_in, w_lr, wg, bg, min(GLA_IN_ROWS, t))
    tb = min(GLA_SCAN_ROWS, t)
    o_f, s_f = _gla_scan(q, k, v, cum_f, s0_f, min(2 * tb, t), reverse=False)
    x_new, s_b = _gla_scan(q, k, v, cum_b, s0_b, tb, reverse=True,
                           out_args=(o_f, g, x, gt, gn, gpost, w_out))
    return x_new, s_f, s_b


def _lru_scan_kernel(x_ref, xp_ref, xn_ref, sh_ref, scl_ref, gpre_ref, w_ref, cw_ref, cb_ref,
                     waf_ref, baf_ref, wxf_ref, bxf_ref, lamf_ref,
                     wab_ref, bab_ref, wxb_ref, bxb_ref, lamb_ref, h0_ref,
                     hsum_ref, pb_ref, g_ref, hbfin_ref, pbfin_ref, ffin_ref,
                     zext_s, hf_s, pf_s, hb_s, pbk_s, carry_s, cin_s, *pre_s):
    g = pl.program_id(1)
    ng = pl.num_programs(1)
    r, s, d = x_ref.shape
    rows = r * s
    w = LRU_WIDTH
    halo_l = CONV_LEFT * s
    n_right = CONV_W - 1 - CONV_LEFT

    @pl.when(g == 0)
    def _():
        cin_s[...] = h0_ref[...]

    sh, scl, gpre = sh_ref[...], scl_ref[...], gpre_ref[...]
    hb16 = _pre_norm_modulate(x_ref[...].reshape(rows, d), gpre, sh, scl).astype(BF16)
    xh = jnp.concatenate([xp_ref[...].reshape(halo_l, d), xn_ref[...].reshape(n_right * s, d)], axis=0)
    zh = _dot(_pre_norm_modulate(xh, gpre, sh, scl).astype(BF16), w_ref[:, :w])
    sub = lax.broadcasted_iota(jnp.int32, (s, LRU_BLOCK_W), 0)

    def in_proj(nb):
        cs = slice(nb * LRU_BLOCK_W, (nb + 1) * LRU_BLOCK_W)
        zext_s[halo_l:halo_l + rows, cs] = _dot(hb16, w_ref[:, cs])
        g_ref[:, cs] = _dot(hb16, w_ref[:, w + nb * LRU_BLOCK_W:w + (nb + 1) * LRU_BLOCK_W]).astype(BF16)
        for j in range(CONV_LEFT):
            src = halo_l + (r - CONV_LEFT + j) * s
            inner = pltpu.roll(zext_s[src:src + s, cs], 1, 0)
            edge = jnp.where(g == 0, 0.0, pltpu.roll(zh[j * s:(j + 1) * s, cs], 1, 0))
            zext_s[j * s:(j + 1) * s, cs] = jnp.where(sub == 0, edge, inner)
        for j in range(n_right):
            src = halo_l + j * s
            inner = pltpu.roll(zext_s[src:src + s, cs], s - 1, 0)
            edge = jnp.where(g == ng - 1, 0.0,
                             pltpu.roll(zh[halo_l + j * s:halo_l + (j + 1) * s, cs], s - 1, 0))
            dst = halo_l + rows + j * s
            zext_s[dst:dst + s, cs] = jnp.where(sub == s - 1, edge, inner)

    def gate_matmuls(nb):
        cs = slice(nb * LRU_BLOCK_W, (nb + 1) * LRU_BLOCK_W)
        hz = cb_ref[:, cs]
        for j in range(CONV_W):
            hz = hz + cw_ref[j:j + 1, cs] * zext_s[j * s:j * s + rows, cs]
        hzb = hz.astype(BF16)
        prf_s, pif_s, prb_s, pib_s, hzc_s = pre_s[5 * (nb % 2):5 * (nb % 2) + 5]
        prf_s[...] = _dot(hzb, waf_ref[nb]) + baf_ref[:, cs]
        pif_s[...] = _dot(hzb, wxf_ref[nb]) + bxf_ref[:, cs]
        prb_s[...] = _dot(hzb, wab_ref[nb]) + bab_ref[:, cs]
        pib_s[...] = _dot(hzb, wxb_ref[nb]) + bxb_ref[:, cs]
        hzc_s[...] = hz

    in_proj(0)
    gate_matmuls(0)
    for nb in range(LRU_BLOCKS):
        cs = slice(nb * LRU_BLOCK_W, (nb + 1) * LRU_BLOCK_W)
        if nb + 1 < LRU_BLOCKS:
            in_proj(nb + 1)
            gate_matmuls(nb + 1)
        prf_s, pif_s, prb_s, pib_s, hzc_s = pre_s[5 * (nb % 2):5 * (nb % 2) + 5]
        c1f = jnp.broadcast_to((-0.5 * LRU_C * LOG2_E) * _softplus(-lamf_ref[:, cs]), (s, LRU_BLOCK_W))
        c1b = jnp.broadcast_to((-0.5 * LRU_C * LOG2_E) * _softplus(-lamb_ref[:, cs]), (s, LRU_BLOCK_W))

        def gate(pr_s, pi_s, c1, rws):
            tr = jnp.tanh(pr_s[rws, :])
            ti = jnp.tanh(pi_s[rws, :])
            hz = hzc_s[rws, :]
            a = jnp.exp2(c1 * tr + c1)
            return a, _sqrt(1.0 - a * a) * (hz * ti + hz)

        hf = hb = jnp.zeros((s, LRU_BLOCK_W), F32)
        pf = pb = jnp.ones((s, LRU_BLOCK_W), F32)
        for t in range(r):
            rf = slice(t * s, (t + 1) * s)
            rb = slice((r - 1 - t) * s, (r - t) * s)
            af, uf = gate(prf_s, pif_s, c1f, rf)
            ab, ub = gate(prb_s, pib_s, c1b, rb)
            hf = af * hf + uf
            pf = af * pf
            hb = ab * hb + ub
            pb = ab * pb
            hf_s[rf, :] = hf
            pf_s[rf, :] = pf
            hb_s[rb, :] = hb
            pbk_s[rb, :] = pb

        cin = cin_s[:, cs]
        for c in range(s):
            carry_s[c:c + 1, :] = cin
            cin = hf[c:c + 1] + pf[c:c + 1] * cin
        cin_s[:, cs] = cin
        carry = carry_s[...][None]
        piece = min(r, 16)
        for i in range(r // piece):
            rws = slice(i * piece * s, (i + 1) * piece * s)
            shape3 = (piece, s, LRU_BLOCK_W)
            hsum = (hf_s[rws, :].reshape(shape3) + pf_s[rws, :].reshape(shape3) * carry
                    + hb_s[rws, :].reshape(shape3))
            hsum_ref[rws, cs] = hsum.reshape(piece * s, LRU_BLOCK_W).astype(BF16)
            pb_ref[rws, cs] = pbk_s[rws, :].astype(BF16)
        hbfin_ref[:, cs] = hb
        pbfin_ref[:, cs] = pb

    ffin_ref[...] = cin_s[...]


def _lru_scan(x5, sh, scl, gpre, w_in, conv_w, conv_b, p_f, p_b, h0_f):
    bsz, r, ng, s, d = x5.shape
    rows = r * s
    w = LRU_WIDTH
    n_right = CONV_W - 1 - CONV_LEFT
    once = pl.Buffered(1)
    cur = pl.BlockSpec((None, r, None, s, d), lambda b, g: (b, 0, g, 0, 0))
    prev = pl.BlockSpec((None, CONV_LEFT, None, s, d),
                        lambda b, g: (b, r // CONV_LEFT - 1, jnp.maximum(g - 1, 0), 0, 0))
    nxt = pl.BlockSpec((None, n_right, None, s, d), lambda b, g: (b, 0, jnp.minimum(g + 1, ng - 1), 0, 0))
    vec = lambda wd: pl.BlockSpec((None, 1, wd), lambda b, g: (b, 0, 0))
    full = lambda a: pl.BlockSpec(a.shape, lambda b, g: (0,) * a.ndim, pipeline_mode=once)
    row = pl.BlockSpec((None, rows, w), lambda b, g: (b, g, 0))
    fin = pl.BlockSpec((None, None, s, w), lambda b, g: (b, g, 0, 0))
    blk = lambda: pltpu.VMEM((rows, LRU_BLOCK_W), F32)
    return pl.pallas_call(
        _lru_scan_kernel,
        grid=(bsz, ng),
        in_specs=[cur, prev, nxt, vec(d), vec(d), full(gpre), full(w_in), full(conv_w), full(conv_b)]
                 + [full(a) for a in p_f] + [full(a) for a in p_b] + [vec(w)],
        out_specs=[row, row, row, fin, fin, vec(w)],
        out_shape=[jax.ShapeDtypeStruct((bsz, ng * rows, w), BF16),
                   jax.ShapeDtypeStruct((bsz, ng * rows, w), BF16),
                   jax.ShapeDtypeStruct((bsz, ng * rows, w), BF16),
                   jax.ShapeDtypeStruct((bsz, ng, s, w), F32),
                   jax.ShapeDtypeStruct((bsz, ng, s, w), F32),
                   jax.ShapeDtypeStruct((bsz, 1, w), F32)],
        scratch_shapes=[pltpu.VMEM((rows + (CONV_W - 1) * s, w), F32)] + [blk() for _ in range(4)]
                       + [pltpu.VMEM((s, LRU_BLOCK_W), F32), pltpu.VMEM((1, w), F32)]
                       + [blk() for _ in range(10)],
        compiler_params=_cparams("parallel", "arbitrary"),
        name="lru_scan",
    )(x5, x5, x5, sh, scl, gpre, w_in, conv_w, conv_b, *p_f, *p_b, h0_f)


def _lru_carry_kernel(h0_ref, hfin_ref, pfin_ref, carry_ref, final_ref):
    c = h0_ref[...]
    for k in range(hfin_ref.shape[0] - 1, -1, -1):
        carry_ref[k:k + 1, :] = c
        c = hfin_ref[k:k + 1, :] + pfin_ref[k:k + 1, :] * c
    final_ref[...] = c


def _lru_carry(h0, hfin, pfin):
    bsz, ng, s, w = hfin.shape
    vec = pl.BlockSpec((None, 1, w), lambda b: (b, 0, 0))
    runs = pl.BlockSpec((None, ng * s, w), lambda b: (b, 0, 0))
    carry, final = pl.pallas_call(
        _lru_carry_kernel,
        grid=(bsz,),
        in_specs=[vec, runs, runs],
        out_specs=[runs, vec],
        out_shape=[jax.ShapeDtypeStruct((bsz, ng * s, w), F32), jax.ShapeDtypeStruct((bsz, 1, w), F32)],
        compiler_params=_cparams("parallel"),
        name="lru_carry",
    )(h0, hfin.reshape(bsz, ng * s, w), pfin.reshape(bsz, ng * s, w))
    return carry.reshape(bsz, ng, s, w), final


def _lru_out_kernel(hsum_ref, pb_ref, cb_ref, g_ref, x_ref, gt_ref, gpost_ref, w_ref, o_ref):
    r, s, d = x_ref.shape
    w = hsum_ref.shape[-1]
    h = hsum_ref[...].astype(F32).reshape(r, s, w) + pb_ref[...].astype(F32).reshape(r, s, w) * cb_ref[...][None]
    y = _dot((h.reshape(r * s, w) * _silu(g_ref[...].astype(F32))).astype(BF16), w_ref[...])
    x = x_ref[...].reshape(r * s, d)
    o_ref[...] = _post_norm_residual(x, y, gpost_ref[...], gt_ref[...]).reshape(r, s, d)


def _lru_out(hsum, pb, cb, g, x5, gt, gpost, w, rb):
    bsz, r, ng, s, d = x5.shape
    nr = r // rb
    tm = rb * s
    width = hsum.shape[-1]
    xio = pl.BlockSpec((None, rb, None, s, d), lambda b, g, i: (b, i, g, 0, 0))
    row = pl.BlockSpec((None, tm, width), lambda b, g, i: (b, g * nr + i, 0))
    runs = pl.BlockSpec((None, None, s, width), lambda b, g, i: (b, g, 0, 0))
    vec = pl.BlockSpec((None, 1, d), lambda b, g, i: (b, 0, 0))
    full = lambda a: pl.BlockSpec(a.shape, lambda b, g, i: (0,) * a.ndim)
    return pl.pallas_call(
        _lru_out_kernel,
        grid=(bsz, ng, nr),
        in_specs=[row, row, runs, row, xio, vec, full(gpost), full(w)],
        out_specs=xio,
        out_shape=jax.ShapeDtypeStruct(x5.shape, F32),
        compiler_params=_cparams("parallel", "parallel", "parallel"),
        name="lru_out",
    )(hsum, pb, cb, g, x5, gt, gpost, w)


def kernel(x, c, ctx, c_ctx, ada_w, ada_b, norm_pre, norm_post, gla_w_in, gla_wg_f, gla_bg_f, gla_wg_b, gla_bg_b, gla_norm, gla_w_out, lru_w_in, lru_conv_w, lru_conv_b, lru_wa_f, lru_ba_f, lru_wx_f, lru_bx_f, lru_lam_f, lru_wa_b, lru_ba_b, lru_wx_b, lru_bx_b, lru_lam_b, lru_w_out):
    bsz, seq, d = x.shape
    ctx_len = ctx.shape[1]
    rows = seq // GRID_W

    cvec = jnp.concatenate([c, c_ctx[None], jnp.zeros((8 - bsz - 1, d), F32)], axis=0)
    mod = _ada_modulation(cvec, ada_w, ada_b)

    def mods(i):
        lat = [mod[i, :bsz, None, j * d:(j + 1) * d] for j in range(3)]
        con = [jnp.broadcast_to(mod[i, bsz, None, None, j * d:(j + 1) * d], (bsz, 1, d)) for j in range(3)]
        return lat, con

    (sh, scl, gt), (sh_c, scl_c, gt_c) = mods(0)
    gpre, gpost = norm_pre[0][None], norm_post[0][None]
    rk = GLA_GATE_RANK
    n_main = 2 * GLA_DK + 2 * GLA_DV
    w_in = gla_w_in[0][:, :n_main].astype(BF16)
    w_lr = jnp.pad(gla_w_in[0][:, n_main:], ((0, 0), (0, 128 - 2 * rk))).astype(BF16)
    wg = (jnp.pad(gla_wg_f[0], ((0, 128 - rk), (0, 0))).astype(BF16),
          jnp.pad(gla_wg_b[0], ((rk, 128 - 2 * rk), (0, 0))).astype(BF16))
    bg = (gla_bg_f[0][None], gla_bg_b[0][None])
    gla_args = (gpre, gpost, w_in, w_lr, wg, bg, gla_norm[0][None], gla_w_out[0].astype(BF16))
    s0 = jnp.zeros((bsz, GLA_HEADS, GLA_HEAD_V, GLA_HEAD_K), F32)
    ctx, s_f, s_b = _gla_layer(ctx, sh_c, scl_c, gt_c, *gla_args, s0, s0)
    x, _, _ = _gla_layer(x, sh, scl, gt, *gla_args, s_f, s_b)

    (sh, scl, gt), (sh_c, scl_c, _) = mods(1)
    gpre, gpost = norm_pre[1][None], norm_post[1][None]
    vec = lambda a: a[None]
    gate = lambda wgt, bias: (wgt.astype(BF16), vec(0.5 * bias))
    p_f = (*gate(lru_wa_f[0], lru_ba_f[0]), *gate(lru_wx_f[0], lru_bx_f[0]), vec(lru_lam_f[0]))
    p_b = (*gate(lru_wa_b[0], lru_ba_b[0]), *gate(lru_wx_b[0], lru_bx_b[0]), vec(lru_lam_b[0]))
    scan_args = (gpre, lru_w_in[0].astype(BF16), 0.5 * lru_conv_w[0], vec(0.5 * lru_conv_b[0]), p_f, p_b)
    h0 = jnp.zeros((bsz, 1, LRU_WIDTH), F32)
    cstep = ctx_len // N_SEG
    ctx5 = ctx.reshape(bsz, N_SEG, cstep, d).transpose(0, 2, 1, 3).reshape(bsz, cstep, 1, N_SEG, d)
    _, _, _, hbfin, pbfin, s_f = _lru_scan(ctx5, sh_c, scl_c, *scan_args, h0)
    _, s_b = _lru_carry(h0, hbfin, pbfin)
    x5 = x.reshape(bsz, rows, GRID_W // N_SEG, N_SEG, d)
    hsum, pb, g, hbfin, pbfin, _ = _lru_scan(x5, sh, scl, *scan_args, s_f)
    cb, _ = _lru_carry(s_b, hbfin, pbfin)
    out5 = _lru_out(hsum, pb, cb, g, x5, gt, gpost, lru_w_out[0].astype(BF16), rb=min(LRU_OUT_STEPS, rows))
    return out5.reshape(bsz, seq, d)
```

```python
import functools

import jax
import jax.numpy as jnp
from jax import lax
from jax.experimental import pallas as pl
from jax.experimental.pallas import tpu as pltpu

F32 = jnp.float32
BF16 = jnp.bfloat16

RMS_EPS = 1e-6
LOG2_E = 1.4426950408889634
GRID_W = 64
GLA_HEADS = 4
GLA_HEAD_K = 128
GLA_HEAD_V = 256
GLA_DK = GLA_HEADS * GLA_HEAD_K
GLA_DV = GLA_HEADS * GLA_HEAD_V
GLA_GATE_RANK = 16
GLA_GATE_NORM = 16.0
GLA_CHUNK = 256
GLA_FACTORISED_MIN_CUM = -86.0
GLA_SUM_PIECE = 128
GLA_IN_ROWS = 1024
GLA_SCAN_ROWS = 1024
LRU_BLOCKS = 5
LRU_BLOCK_W = 256
LRU_WIDTH = LRU_BLOCKS * LRU_BLOCK_W
LRU_C = 8.0
CONV_W = 4
CONV_LEFT = CONV_W // 2
N_SEG = 8
LRU_OUT_STEPS = 128

VMEM_LIMIT_BYTES = 56 * 1024 * 1024


def _cparams(*sem):
    return pltpu.CompilerParams(dimension_semantics=sem, vmem_limit_bytes=VMEM_LIMIT_BYTES)


def _silu(x):
    hx = 0.5 * x
    return hx * jnp.tanh(hx) + hx


def _sqrt(x):
    return x * lax.rsqrt(jnp.maximum(x, 1e-30))


def _log_sigmoid(x):
    return jnp.minimum(x, 0.0) - jnp.log(1.0 + jnp.exp(-jnp.abs(x)))


def _softplus(x):
    return jnp.maximum(x, 0.0) + jnp.log1p(jnp.exp(-jnp.abs(x)))


def _split_bf16(x):
    hi = x.astype(BF16)
    lo = (x - hi.astype(F32)).astype(BF16)
    return hi, lo


def _dot(a, b):
    return jnp.dot(a, b, preferred_element_type=F32)


def _dot_x3(a, b):
    a_hi, a_lo = _split_bf16(a)
    b_hi, b_lo = _split_bf16(b)
    return _dot(a_hi, b_hi) + (_dot(a_hi, b_lo) + _dot(a_lo, b_hi))


def _pre_norm_modulate(x, gpre, sh, scl):
    ms = jnp.mean(x * x, axis=-1, keepdims=True)
    return (x * lax.rsqrt(ms + RMS_EPS)) * (gpre * (1.0 + scl)) + sh


def _post_norm_residual(x, y, gpost, gt):
    ms = jnp.mean(y * y, axis=-1, keepdims=True)
    return x + (y * lax.rsqrt(ms + RMS_EPS)) * (gt * gpost)


def _ada_kernel(c_ref, w_ref, b_ref, o_ref):
    sc = _silu(c_ref[...])
    o_ref[...] = _dot_x3(sc, w_ref[...]) + b_ref[...]


def _ada_modulation(cvec, ada_w, ada_b):
    depth, d, n3 = ada_w.shape
    tn = 1024
    return pl.pallas_call(
        _ada_kernel,
        grid=(depth, n3 // tn),
        in_specs=[pl.BlockSpec((8, d), lambda i, j: (0, 0)),
                  pl.BlockSpec((None, d, tn), lambda i, j: (i, 0, j)),
                  pl.BlockSpec((None, 1, tn), lambda i, j: (i, 0, j))],
        out_specs=pl.BlockSpec((None, 8, tn), lambda i, j: (i, 0, j)),
        out_shape=jax.ShapeDtypeStruct((depth, 8, n3), F32),
        compiler_params=_cparams("parallel", "parallel"),
        name="ada_modulation",
    )(cvec, ada_w, ada_b.reshape(depth, 1, n3))


def _gla_log_decay_sums(lr, wg_ref, bg_ref, cum_ref, reverse):
    c, p = GLA_CHUNK, GLA_SUM_PIECE
    row = lax.broadcasted_iota(jnp.int32, (p, 2 * p), 0)
    col = lax.broadcasted_iota(jnp.int32, (p, 2 * p), 1) % p
    tri = ((row <= col) if reverse else (row >= col)).astype(BF16)
    z = _dot(lr, wg_ref[...]) + bg_ref[...]
    log_a = _log_sigmoid(z) * (LOG2_E / GLA_GATE_NORM)
    hi, lo = _split_bf16(log_a)
    for i in range(lr.shape[0] // c):
        carry = None
        pieces = range(c // p)
        for j in (reversed(pieces) if reverse else pieces):
            rows = slice(i * c + j * p, i * c + (j + 1) * p)
            s = _dot(tri, jnp.concatenate([hi[rows], lo[rows]], axis=0))
            if carry is not None:
                s = s + carry
            cum_ref[rows, :] = s
            carry = s[0:1] if reverse else s[p - 1:p]


def _gla_in_kernel(x_ref, sh_ref, scl_ref, gpre_ref, w_ref, wlr_ref, wgf_ref, bgf_ref, wgb_ref, bgb_ref,
                   q_ref, k_ref, v_ref, g_ref, cumf_ref, cumb_ref):
    hb16 = _pre_norm_modulate(x_ref[...], gpre_ref[...], sh_ref[...], scl_ref[...]).astype(BF16)
    lr = _dot(hb16, wlr_ref[...]).astype(BF16)
    proj = _dot(hb16, w_ref[...])
    _gla_log_decay_sums(lr, wgf_ref, bgf_ref, cumf_ref, reverse=False)
    _gla_log_decay_sums(lr, wgb_ref, bgb_ref, cumb_ref, reverse=True)
    q_ref[...] = (proj[:, :GLA_DK] * (GLA_HEAD_K ** -0.5)).astype(BF16)
    k_ref[...] = proj[:, GLA_DK:2 * GLA_DK].astype(BF16)
    v_ref[...] = proj[:, 2 * GLA_DK:2 * GLA_DK + GLA_DV].astype(BF16)
    g_ref[...] = _silu(proj[:, 2 * GLA_DK + GLA_DV:]).astype(BF16)


def _gla_in(x, sh, scl, gpre, w, w_lr, wg, bg, tm):
    bsz, t, d = x.shape
    row = lambda wd: pl.BlockSpec((None, tm, wd), lambda b, i: (b, i, 0))
    vec = lambda wd: pl.BlockSpec((None, 1, wd), lambda b, i: (b, 0, 0))
    full = lambda a: pl.BlockSpec(a.shape, lambda b, i: (0,) * a.ndim)
    return pl.pallas_call(
        _gla_in_kernel,
        grid=(bsz, t // tm),
        in_specs=[row(d), vec(d), vec(d), full(gpre), full(w), full(w_lr),
                  full(wg[0]), full(bg[0]), full(wg[1]), full(bg[1])],
        out_specs=[row(GLA_DK), row(GLA_DK), row(GLA_DV), row(GLA_DV), row(GLA_DK), row(GLA_DK)],
        out_shape=[jax.ShapeDtypeStruct((bsz, t, GLA_DK), BF16),
                   jax.ShapeDtypeStruct((bsz, t, GLA_DK), BF16),
                   jax.ShapeDtypeStruct((bsz, t, GLA_DV), BF16),
                   jax.ShapeDtypeStruct((bsz, t, GLA_DV), BF16),
                   jax.ShapeDtypeStruct((bsz, t, GLA_DK), F32),
                   jax.ShapeDtypeStruct((bsz, t, GLA_DK), F32)],
        compiler_params=_cparams("parallel", "parallel"),
        name="gla_in",
    )(x, sh, scl, gpre, w, w_lr, wg[0], bg[0], wg[1], bg[1])


def _gla_scores_exact(q, k, cum, reverse, kf_s, cm_s, sc_s):
    c = q.shape[0]
    kf_s[...] = k
    cm_s[...] = cum
    sc_s[...] = jnp.zeros_like(sc_s)
    ridx = lax.broadcasted_iota(jnp.int32, (c, 1), 0)
    cidx = lax.broadcasted_iota(jnp.int32, (c, c), 1)

    def column(j, carry):
        kj = kf_s[pl.ds(j, 1), :]
        cj = cm_s[pl.ds(j, 1), :]
        live = (ridx <= j) if reverse else (ridx >= j)
        dec = jnp.exp2(jnp.where(live, jnp.minimum(cum - cj, 0.0), -1e30))
        sj = jnp.sum(q * kj * dec, axis=-1, keepdims=True)
        sc_s[...] += jnp.where(cidx == j, sj, 0.0)
        return carry

    lax.fori_loop(0, c, column, 0)
    return sc_s[...]


def _gla_chunks(q_ref, k_ref, v_ref, cum_ref, s_ref, put_o, kf_s, cm_s, sc_s, *, reverse, factorised):
    tb = q_ref.shape[0]
    c = GLA_CHUNK
    row = lax.broadcasted_iota(jnp.int32, (c, c), 0)
    col = lax.broadcasted_iota(jnp.int32, (c, c), 1)
    keep = (row <= col) if reverse else (row >= col)
    chunks = range(tb // c)
    for i in (reversed(chunks) if reverse else chunks):
        rows = slice(i * c, (i + 1) * c)
        for hd in range(GLA_HEADS):
            kc = slice(hd * GLA_HEAD_K, (hd + 1) * GLA_HEAD_K)
            vc = slice(hd * GLA_HEAD_V, (hd + 1) * GLA_HEAD_V)
            cum = cum_ref[rows, kc]
            last = cum[0:1] if reverse else cum[c - 1:c]
            q = q_ref[rows, kc].astype(F32)
            k = k_ref[rows, kc].astype(F32)
            v = v_ref[rows, vc]
            qe = (q * jnp.exp2(cum)).astype(BF16)
            kl = (k * jnp.exp2(last - cum)).astype(BF16)
            st = s_ref[hd]
            if factorised:
                ke = (k * jnp.exp2(-cum)).astype(BF16)
                scores = lax.dot_general(qe, ke, (((1,), (1,)), ((), ())), preferred_element_type=F32)
                scores = jnp.where(keep, scores, 0.0)
            else:
                scores = _gla_scores_exact(q, k, cum, reverse, kf_s, cm_s, sc_s)
            o = lax.dot_general(qe, st.astype(BF16), (((1,), (1,)), ((), ())),
                                preferred_element_type=F32)
            o = o + _dot(scores.astype(BF16), v)
            put_o(rows, vc, o)
            upd = lax.dot_general(v, kl, (((0,), (0,)), ((), ())), preferred_element_type=F32)
            s_ref[hd] = st * jnp.exp2(last) + upd


def _gla_scan_kernel(*refs, reverse, fuse_out):
    if fuse_out:
        (q_ref, k_ref, v_ref, cum_ref, s0_ref, of_ref, g_ref, x_ref, gt_ref, gn_ref, gpost_ref, wout_ref,
         y_ref, sfin_ref, s_ref, kf_s, cm_s, sc_s, o_s) = refs
    else:
        q_ref, k_ref, v_ref, cum_ref, s0_ref, y_ref, sfin_ref, s_ref, kf_s, cm_s, sc_s = refs
    n = pl.program_id(1)

    @pl.when(n == 0)
    def _():
        s_ref[...] = s0_ref[...]

    if fuse_out:
        def put_o(rows, vc, o):
            o_s[rows, vc] = o + of_ref[rows, vc].astype(F32)
    else:
        def put_o(rows, vc, o):
            y_ref[rows, vc] = o.astype(BF16)

    c = GLA_CHUNK
    ends = [cum_ref[i * c:i * c + 1, :] if reverse else cum_ref[(i + 1) * c - 1:(i + 1) * c, :]
            for i in range(q_ref.shape[0] // c)]
    safe = jnp.min(functools.reduce(jnp.minimum, ends)) >= GLA_FACTORISED_MIN_CUM
    scan = functools.partial(_gla_chunks, q_ref, k_ref, v_ref, cum_ref, s_ref, put_o, kf_s, cm_s, sc_s,
                             reverse=reverse)
    pl.when(safe)(functools.partial(scan, factorised=True))
    pl.when(jnp.logical_not(safe))(functools.partial(scan, factorised=False))

    if fuse_out:
        gn = gn_ref[...]
        parts = []
        for hd in range(GLA_HEADS):
            oh = o_s[:, hd * GLA_HEAD_V:(hd + 1) * GLA_HEAD_V]
            ms = jnp.mean(oh * oh, axis=-1, keepdims=True)
            parts.append(oh * lax.rsqrt(ms + RMS_EPS) * gn)
        on = jnp.concatenate(parts, axis=-1) * g_ref[...].astype(F32)
        y = _dot(on.astype(BF16), wout_ref[...])
        y_ref[...] = _post_norm_residual(x_ref[...], y, gpost_ref[...], gt_ref[...])

    @pl.when(n == pl.num_programs(1) - 1)
    def _():
        sfin_ref[...] = s_ref[...]


def _gla_scan(q, k, v, cum, s0, tb, reverse, out_args=None):
    bsz, t, _ = q.shape
    nblk = t // tb
    c = GLA_CHUNK
    pos = (lambda n: nblk - 1 - n) if reverse else (lambda n: n)
    row = lambda wd: pl.BlockSpec((None, tb, wd), lambda b, n: (b, pos(n), 0))
    vec = lambda wd: pl.BlockSpec((None, 1, wd), lambda b, n: (b, 0, 0))
    full = lambda a: pl.BlockSpec(a.shape, lambda b, n: (0,) * a.ndim)
    st = pl.BlockSpec((None, GLA_HEADS, GLA_HEAD_V, GLA_HEAD_K), lambda b, n: (b, 0, 0, 0))
    in_specs = [row(GLA_DK), row(GLA_DK), row(GLA_DV), row(GLA_DK), st]
    scratch = [pltpu.VMEM((GLA_HEADS, GLA_HEAD_V, GLA_HEAD_K), F32),
               pltpu.VMEM((c, GLA_HEAD_K), F32), pltpu.VMEM((c, GLA_HEAD_K), F32), pltpu.VMEM((c, c), F32)]
    args = [q, k, v, cum, s0]
    if out_args is None:
        y_spec, y_shape = row(GLA_DV), jax.ShapeDtypeStruct((bsz, t, GLA_DV), BF16)
    else:
        o_other, g, x, gt, gn, gpost, w_out = out_args
        d = x.shape[-1]
        in_specs += [row(GLA_DV), row(GLA_DV), row(d), vec(d), full(gn), full(gpost), full(w_out)]
        args += [o_other, g, x, gt, gn, gpost, w_out]
        scratch.append(pltpu.VMEM((tb, GLA_DV), F32))
        y_spec, y_shape = row(d), jax.ShapeDtypeStruct((bsz, t, d), F32)
    return pl.pallas_call(
        functools.partial(_gla_scan_kernel, reverse=reverse, fuse_out=out_args is not None),
        grid=(bsz, nblk),
        in_specs=in_specs,
        out_specs=[y_spec, st],
        out_shape=[y_shape, jax.ShapeDtypeStruct(s0.shape, F32)],
        scratch_shapes=scratch,
        compiler_params=_cparams("parallel", "arbitrary"),
        name="gla_scan_bwd" if reverse else "gla_scan_fwd",
    )(*args)


def _gla_layer(x, sh, scl, gt, gpre, gpost, w_in, w_lr, wg, bg, gn, w_out, s0_f, s0_b):
    t = x.shape[1]
    q, k, v, g, cum_f, cum_b = _gla_in(x, sh, scl, gpre, w_in, w_lr, wg, bg, min(GLA_IN_ROWS, t))
    tb = min(GLA_SCAN_ROWS, t)
    o_f, s_f = _gla_scan(q, k, v, cum_f, s0_f, min(2 * tb, t), reverse=False)
    x_new, s_b = _gla_scan(q, k, v, cum_b, s0_b, tb, reverse=True,
                           out_args=(o_f, g, x, gt, gn, gpost, w_out))
    return x_new, s_f, s_b


def _lru_scan_kernel(x_ref, xp_ref, xn_ref, sh_ref, scl_ref, gpre_ref, w_ref, cw_ref, cb_ref,
                     waf_ref, baf_ref, wxf_ref, bxf_ref, lamf_ref,
                     wab_ref, bab_ref, wxb_ref, bxb_ref, lamb_ref, h0_ref,
                     hsum_ref, pb_ref, g_ref, hbfin_ref, pbfin_ref, ffin_ref,
                     zext_s, hf_s, pf_s, hb_s, pbk_s, carry_s, cin_s, *pre_s):
    g = pl.program_id(1)
    ng = pl.num_programs(1)
    r, s, d = x_ref.shape
    rows = r * s
    w = LRU_WIDTH
    halo_l = CONV_LEFT * s
    n_right = CONV_W - 1 - CONV_LEFT

    @pl.when(g == 0)
    def _():
        cin_s[...] = h0_ref[...]

    sh, scl, gpre = sh_ref[...], scl_ref[...], gpre_ref[...]
    hb16 = _pre_norm_modulate(x_ref[...].reshape(rows, d), gpre, sh, scl).astype(BF16)
    xh = jnp.concatenate([xp_ref[...].reshape(halo_l, d), xn_ref[...].reshape(n_right * s, d)], axis=0)
    zh = _dot(_pre_norm_modulate(xh, gpre, sh, scl).astype(BF16), w_ref[:, :w])
    sub = lax.broadcasted_iota(jnp.int32, (s, LRU_BLOCK_W), 0)

    def in_proj(nb):
        cs = slice(nb * LRU_BLOCK_W, (nb + 1) * LRU_BLOCK_W)
        zext_s[halo_l:halo_l + rows, cs] = _dot(hb16, w_ref[:, cs])
        g_ref[:, cs] = _dot(hb16, w_ref[:, w + nb * LRU_BLOCK_W:w + (nb + 1) * LRU_BLOCK_W]).astype(BF16)
        for j in range(CONV_LEFT):
            src = halo_l + (r - CONV_LEFT + j) * s
            inner = pltpu.roll(zext_s[src:src + s, cs], 1, 0)
            edge = jnp.where(g == 0, 0.0, pltpu.roll(zh[j * s:(j + 1) * s, cs], 1, 0))
            zext_s[j * s:(j + 1) * s, cs] = jnp.where(sub == 0, edge, inner)
        for j in range(n_right):
            src = halo_l + j * s
            inner = pltpu.roll(zext_s[src:src + s, cs], s - 1, 0)
            edge = jnp.where(g == ng - 1, 0.0,
                             pltpu.roll(zh[halo_l + j * s:halo_l + (j + 1) * s, cs], s - 1, 0))
            dst = halo_l + rows + j * s
            zext_s[dst:dst + s, cs] = jnp.where(sub == s - 1, edge, inner)

    def gate_matmuls(nb):
        cs = slice(nb * LRU_BLOCK_W, (nb + 1) * LRU_BLOCK_W)
        hz = cb_ref[:, cs]
        for j in range(CONV_W):
            hz = hz + cw_ref[j:j + 1, cs] * zext_s[j * s:j * s + rows, cs]
        hzb = hz.astype(BF16)
        prf_s, pif_s, prb_s, pib_s, hzc_s = pre_s[5 * (nb % 2):5 * (nb % 2) + 5]
        prf_s[...] = _dot(hzb, waf_ref[nb]) + baf_ref[:, cs]
        pif_s[...] = _dot(hzb, wxf_ref[nb]) + bxf_ref[:, cs]
        prb_s[...] = _dot(hzb, wab_ref[nb]) + bab_ref[:, cs]
        pib_s[...] = _dot(hzb, wxb_ref[nb]) + bxb_ref[:, cs]
        hzc_s[...] = hz

    in_proj(0)
    gate_matmuls(0)
    for nb in range(LRU_BLOCKS):
        cs = slice(nb * LRU_BLOCK_W, (nb + 1) * LRU_BLOCK_W)
        if nb + 1 < LRU_BLOCKS:
            in_proj(nb + 1)
            gate_matmuls(nb + 1)
        prf_s, pif_s, prb_s, pib_s, hzc_s = pre_s[5 * (nb % 2):5 * (nb % 2) + 5]
        c1f = jnp.broadcast_to((-0.5 * LRU_C * LOG2_E) * _softplus(-lamf_ref[:, cs]), (s, LRU_BLOCK_W))
        c1b = jnp.broadcast_to((-0.5 * LRU_C * LOG2_E) * _softplus(-lamb_ref[:, cs]), (s, LRU_BLOCK_W))

        def gate(pr_s, pi_s, c1, rws):
            tr = jnp.tanh(pr_s[rws, :])
            ti = jnp.tanh(pi_s[rws, :])
            hz = hzc_s[rws, :]
            a = jnp.exp2(c1 * tr + c1)
            return a, _sqrt(1.0 - a * a) * (hz * ti + hz)

        hf = hb = jnp.zeros((s, LRU_BLOCK_W), F32)
        pf = pb = jnp.ones((s, LRU_BLOCK_W), F32)
        for t in range(r):
            rf = slice(t * s, (t + 1) * s)
            rb = slice((r - 1 - t) * s, (r - t) * s)
            af, uf = gate(prf_s, pif_s, c1f, rf)
            ab, ub = gate(prb_s, pib_s, c1b, rb)
            hf = af * hf + uf
            pf = af * pf
            hb = ab * hb + ub
            pb = ab * pb
            hf_s[rf, :] = hf
            pf_s[rf, :] = pf
            hb_s[rb, :] = hb
            pbk_s[rb, :] = pb

        cin = cin_s[:, cs]
        for c in range(s):
            carry_s[c:c + 1, :] = cin
            cin = hf[c:c + 1] + pf[c:c + 1] * cin
        cin_s[:, cs] = cin
        carry = carry_s[...][None]
        piece = min(r, 16)
        for i in range(r // piece):
            rws = slice(i * piece * s, (i + 1) * piece * s)
            shape3 = (piece, s, LRU_BLOCK_W)
            hsum = (hf_s[rws, :].reshape(shape3) + pf_s[rws, :].reshape(shape3) * carry
                    + hb_s[rws, :].reshape(shape3))
            hsum_ref[rws, cs] = hsum.reshape(piece * s, LRU_BLOCK_W).astype(BF16)
            pb_ref[rws, cs] = pbk_s[rws, :].astype(BF16)
        hbfin_ref[:, cs] = hb
        pbfin_ref[:, cs] = pb

    ffin_ref[...] = cin_s[...]


def _lru_scan(x5, sh, scl, gpre, w_in, conv_w, conv_b, p_f, p_b, h0_f):
    bsz, r, ng, s, d = x5.shape
    rows = r * s
    w = LRU_WIDTH
    n_right = CONV_W - 1 - CONV_LEFT
    once = pl.Buffered(1)
    cur = pl.BlockSpec((None, r, None, s, d), lambda b, g: (b, 0, g, 0, 0))
    prev = pl.BlockSpec((None, CONV_LEFT, None, s, d),
                        lambda b, g: (b, r // CONV_LEFT - 1, jnp.maximum(g - 1, 0), 0, 0))
    nxt = pl.BlockSpec((None, n_right, None, s, d), lambda b, g: (b, 0, jnp.minimum(g + 1, ng - 1), 0, 0))
    vec = lambda wd: pl.BlockSpec((None, 1, wd), lambda b, g: (b, 0, 0))
    full = lambda a: pl.BlockSpec(a.shape, lambda b, g: (0,) * a.ndim, pipeline_mode=once)
    row = pl.BlockSpec((None, rows, w), lambda b, g: (b, g, 0))
    fin = pl.BlockSpec((None, None, s, w), lambda b, g: (b, g, 0, 0))
    blk = lambda: pltpu.VMEM((rows, LRU_BLOCK_W), F32)
    return pl.pallas_call(
        _lru_scan_kernel,
        grid=(bsz, ng),
        in_specs=[cur, prev, nxt, vec(d), vec(d), full(gpre), full(w_in), full(conv_w), full(conv_b)]
                 + [full(a) for a in p_f] + [full(a) for a in p_b] + [vec(w)],
        out_specs=[row, row, row, fin, fin, vec(w)],
        out_shape=[jax.ShapeDtypeStruct((bsz, ng * rows, w), BF16),
                   jax.ShapeDtypeStruct((bsz, ng * rows, w), BF16),
                   jax.ShapeDtypeStruct((bsz, ng * rows, w), BF16),
                   jax.ShapeDtypeStruct((bsz, ng, s, w), F32),
                   jax.ShapeDtypeStruct((bsz, ng, s, w), F32),
                   jax.ShapeDtypeStruct((bsz, 1, w), F32)],
        scratch_shapes=[pltpu.VMEM((rows + (CONV_W - 1) * s, w), F32)] + [blk() for _ in range(4)]
                       + [pltpu.VMEM((s, LRU_BLOCK_W), F32), pltpu.VMEM((1, w), F32)]
                       + [blk() for _ in range(10)],
        compiler_params=_cparams("parallel", "arbitrary"),
        name="lru_scan",
    )(x5, x5, x5, sh, scl, gpre, w_in, conv_w, conv_b, *p_f, *p_b, h0_f)


def _lru_carry_kernel(h0_ref, hfin_ref, pfin_ref, carry_ref, final_ref):
    c = h0_ref[...]
    for k in range(hfin_ref.shape[0] - 1, -1, -1):
        carry_ref[k:k + 1, :] = c
        c = hfin_ref[k:k + 1, :] + pfin_ref[k:k + 1, :] * c
    final_ref[...] = c


def _lru_carry(h0, hfin, pfin):
    bsz, ng, s, w = hfin.shape
    vec = pl.BlockSpec((None, 1, w), lambda b: (b, 0, 0))
    runs = pl.BlockSpec((None, ng * s, w), lambda b: (b, 0, 0))
    carry, final = pl.pallas_call(
        _lru_carry_kernel,
        grid=(bsz,),
        in_specs=[vec, runs, runs],
        out_specs=[runs, vec],
        out_shape=[jax.ShapeDtypeStruct((bsz, ng * s, w), F32), jax.ShapeDtypeStruct((bsz, 1, w), F32)],
        compiler_params=_cparams("parallel"),
        name="lru_carry",
    )(h0, hfin.reshape(bsz, ng * s, w), pfin.reshape(bsz, ng * s, w))
    return carry.reshape(bsz, ng, s, w), final


def _lru_out_kernel(hsum_ref, pb_ref, cb_ref, g_ref, x_ref, gt_ref, gpost_ref, w_ref, o_ref):
    r, s, d = x_ref.shape
    w = hsum_ref.shape[-1]
    h = hsum_ref[...].astype(F32).reshape(r, s, w) + pb_ref[...].astype(F32).reshape(r, s, w) * cb_ref[...][None]
    y = _dot((h.reshape(r * s, w) * _silu(g_ref[...].astype(F32))).astype(BF16), w_ref[...])
    x = x_ref[...].reshape(r * s, d)
    o_ref[...] = _post_norm_residual(x, y, gpost_ref[...], gt_ref[...]).reshape(r, s, d)


def _lru_out(hsum, pb, cb, g, x5, gt, gpost, w, rb):
    bsz, r, ng, s, d = x5.shape
    nr = r // rb
    tm = rb * s
    width = hsum.shape[-1]
    xio = pl.BlockSpec((None, rb, None, s, d), lambda b, g, i: (b, i, g, 0, 0))
    row = pl.BlockSpec((None, tm, width), lambda b, g, i: (b, g * nr + i, 0))
    runs = pl.BlockSpec((None, None, s, width), lambda b, g, i: (b, g, 0, 0))
    vec = pl.BlockSpec((None, 1, d), lambda b, g, i: (b, 0, 0))
    full = lambda a: pl.BlockSpec(a.shape, lambda b, g, i: (0,) * a.ndim)
    return pl.pallas_call(
        _lru_out_kernel,
        grid=(bsz, ng, nr),
        in_specs=[row, row, runs, row, xio, vec, full(gpost), full(w)],
        out_specs=xio,
        out_shape=jax.ShapeDtypeStruct(x5.shape, F32),
        compiler_params=_cparams("parallel", "parallel", "parallel"),
        name="lru_out",
    )(hsum, pb, cb, g, x5, gt, gpost, w)


def kernel(x, c, ctx, c_ctx, ada_w, ada_b, norm_pre, norm_post, gla_w_in, gla_wg_f, gla_bg_f, gla_wg_b, gla_bg_b, gla_norm, gla_w_out, lru_w_in, lru_conv_w, lru_conv_b, lru_wa_f, lru_ba_f, lru_wx_f, lru_bx_f, lru_lam_f, lru_wa_b, lru_ba_b, lru_wx_b, lru_bx_b, lru_lam_b, lru_w_out):
    bsz, seq, d = x.shape
    ctx_len = ctx.shape[1]
    rows = seq // GRID_W

    cvec = jnp.concatenate([c, c_ctx[None], jnp.zeros((8 - bsz - 1, d), F32)], axis=0)
    mod = _ada_modulation(cvec, ada_w, ada_b)

    def mods(i):
        lat = [mod[i, :bsz, None, j * d:(j + 1) * d] for j in range(3)]
        con = [jnp.broadcast_to(mod[i, bsz, None, None, j * d:(j + 1) * d], (bsz, 1, d)) for j in range(3)]
        return lat, con

    (sh, scl, gt), (sh_c, scl_c, gt_c) = mods(0)
    gpre, gpost = norm_pre[0][None], norm_post[0][None]
    rk = GLA_GATE_RANK
    n_main = 2 * GLA_DK + 2 * GLA_DV
    w_in = gla_w_in[0][:, :n_main].astype(BF16)
    w_lr = jnp.pad(gla_w_in[0][:, n_main:], ((0, 0), (0, 128 - 2 * rk))).astype(BF16)
    wg = (jnp.pad(gla_wg_f[0], ((0, 128 - rk), (0, 0))).astype(BF16),
          jnp.pad(gla_wg_b[0], ((rk, 128 - 2 * rk), (0, 0))).astype(BF16))
    bg = (gla_bg_f[0][None], gla_bg_b[0][None])
    gla_args = (gpre, gpost, w_in, w_lr, wg, bg, gla_norm[0][None], gla_w_out[0].astype(BF16))
    s0 = jnp.zeros((bsz, GLA_HEADS, GLA_HEAD_V, GLA_HEAD_K), F32)
    ctx, s_f, s_b = _gla_layer(ctx, sh_c, scl_c, gt_c, *gla_args, s0, s0)
    x, _, _ = _gla_layer(x, sh, scl, gt, *gla_args, s_f, s_b)

    (sh, scl, gt), (sh_c, scl_c, _) = mods(1)
    gpre, gpost = norm_pre[1][None], norm_post[1][None]
    vec = lambda a: a[None]
    gate = lambda wgt, bias: (wgt.astype(BF16), vec(0.5 * bias))
    p_f = (*gate(lru_wa_f[0], lru_ba_f[0]), *gate(lru_wx_f[0], lru_bx_f[0]), vec(lru_lam_f[0]))
    p_b = (*gate(lru_wa_b[0], lru_ba_b[0]), *gate(lru_wx_b[0], lru_bx_b[0]), vec(lru_lam_b[0]))
    scan_args = (gpre, lru_w_in[0].astype(BF16), 0.5 * lru_conv_w[0], vec(0.5 * lru_conv_b[0]), p_f, p_b)
    h0 = jnp.zeros((bsz, 1, LRU_WIDTH), F32)
    cstep = ctx_len // N_SEG
    ctx5 = ctx.reshape(bsz, N_SEG, cstep, d).transpose(0, 2, 1, 3).reshape(bsz, cstep, 1, N_SEG, d)
    _, _, _, hbfin, pbfin, s_f = _lru_scan(ctx5, sh_c, scl_c, *scan_args, h0)
    _, s_b = _lru_carry(h0, hbfin, pbfin)
    x5 = x.reshape(bsz, rows, GRID_W // N_SEG, N_SEG, d)
    hsum, pb, g, hbfin, pbfin, _ = _lru_scan(x5, sh, scl, *scan_args, s_f)
    cb, _ = _lru_carry(s_b, hbfin, pbfin)
    out5 = _lru_out(hsum, pb, cb, g, x5, gt, gpost, lru_w_out[0].astype(BF16), rb=min(LRU_OUT_STEPS, rows))
    return out5.reshape(bsz, seq, d)
```

```python
import functools

import jax
import jax.numpy as jnp
from jax import lax
from jax.experimental import pallas as pl
from jax.experimental.pallas import tpu as pltpu

F32 = jnp.float32
BF16 = jnp.bfloat16

RMS_EPS = 1e-6
LOG2_E = 1.4426950408889634
GRID_W = 64
GLA_HEADS = 4
GLA_HEAD_K = 128
GLA_HEAD_V = 256
GLA_DK = GLA_HEADS * GLA_HEAD_K
GLA_DV = GLA_HEADS * GLA_HEAD_V
GLA_GATE_RANK = 16
GLA_GATE_NORM = 16.0
GLA_CHUNK = 256
GLA_FACTORISED_MIN_CUM = -86.0
GLA_SUM_PIECE = 128
GLA_IN_ROWS = 1024
GLA_SCAN_ROWS = 1024
LRU_BLOCKS = 5
LRU_BLOCK_W = 256
LRU_WIDTH = LRU_BLOCKS * LRU_BLOCK_W
LRU_C = 8.0
CONV_W = 4
CONV_LEFT = CONV_W // 2
N_SEG = 8
LRU_OUT_STEPS = 128

VMEM_LIMIT_BYTES = 56 * 1024 * 1024


def _cparams(*sem):
    return pltpu.CompilerParams(dimension_semantics=sem, vmem_limit_bytes=VMEM_LIMIT_BYTES)


def _silu(x):
    hx = 0.5 * x
    return hx * jnp.tanh(hx) + hx


def _sqrt(x):
    return x * lax.rsqrt(jnp.maximum(x, 1e-30))


def _log_sigmoid(x):
    return jnp.minimum(x, 0.0) - jnp.log(1.0 + jnp.exp(-jnp.abs(x)))


def _softplus(x):
    return jnp.maximum(x, 0.0) + jnp.log1p(jnp.exp(-jnp.abs(x)))


def _split_bf16(x):
    hi = x.astype(BF16)
    lo = (x - hi.astype(F32)).astype(BF16)
    return hi, lo


def _dot(a, b):
    return jnp.dot(a, b, preferred_element_type=F32)


def _dot_x3(a, b):
    a_hi, a_lo = _split_bf16(a)
    b_hi, b_lo = _split_bf16(b)
    return _dot(a_hi, b_hi) + (_dot(a_hi, b_lo) + _dot(a_lo, b_hi))


def _pre_norm_modulate(x, gpre, sh, scl):
    ms = jnp.mean(x * x, axis=-1, keepdims=True)
    return (x * lax.rsqrt(ms + RMS_EPS)) * (gpre * (1.0 + scl)) + sh


def _post_norm_residual(x, y, gpost, gt):
    ms = jnp.mean(y * y, axis=-1, keepdims=True)
    return x + (y * lax.rsqrt(ms + RMS_EPS)) * (gt * gpost)


def _ada_kernel(c_ref, w_ref, b_ref, o_ref):
    sc = _silu(c_ref[...])
    o_ref[...] = _dot_x3(sc, w_ref[...]) + b_ref[...]


def _ada_modulation(cvec, ada_w, ada_b):
    depth, d, n3 = ada_w.shape
    tn = 1024
    return pl.pallas_call(
        _ada_kernel,
        grid=(depth, n3 // tn),
        in_specs=[pl.BlockSpec((8, d), lambda i, j: (0, 0)),
                  pl.BlockSpec((None, d, tn), lambda i, j: (i, 0, j)),
                  pl.BlockSpec((None, 1, tn), lambda i, j: (i, 0, j))],
        out_specs=pl.BlockSpec((None, 8, tn), lambda i, j: (i, 0, j)),
        out_shape=jax.ShapeDtypeStruct((depth, 8, n3), F32),
        compiler_params=_cparams("parallel", "parallel"),
        name="ada_modulation",
    )(cvec, ada_w, ada_b.reshape(depth, 1, n3))


def _gla_log_decay_sums(lr, wg_ref, bg_ref, cum_ref, reverse):
    c, p = GLA_CHUNK, GLA_SUM_PIECE
    row = lax.broadcasted_iota(jnp.int32, (p, 2 * p), 0)
    col = lax.broadcasted_iota(jnp.int32, (p, 2 * p), 1) % p
    tri = ((row <= col) if reverse else (row >= col)).astype(BF16)
    z = _dot(lr, wg_ref[...]) + bg_ref[...]
    log_a = _log_sigmoid(z) * (LOG2_E / GLA_GATE_NORM)
    hi, lo = _split_bf16(log_a)
    for i in range(lr.shape[0] // c):
        carry = None
        pieces = range(c // p)
        for j in (reversed(pieces) if reverse else pieces):
            rows = slice(i * c + j * p, i * c + (j + 1) * p)
            s = _dot(tri, jnp.concatenate([hi[rows], lo[rows]], axis=0))
            if carry is not None:
                s = s + carry
            cum_ref[rows, :] = s
            carry = s[0:1] if reverse else s[p - 1:p]


def _gla_in_kernel(x_ref, sh_ref, scl_ref, gpre_ref, w_ref, wlr_ref, wgf_ref, bgf_ref, wgb_ref, bgb_ref,
                   q_ref, k_ref, v_ref, g_ref, cumf_ref, cumb_ref):
    hb16 = _pre_norm_modulate(x_ref[...], gpre_ref[...], sh_ref[...], scl_ref[...]).astype(BF16)
    lr = _dot(hb16, wlr_ref[...]).astype(BF16)
    proj = _dot(hb16, w_ref[...])
    _gla_log_decay_sums(lr, wgf_ref, bgf_ref, cumf_ref, reverse=False)
    _gla_log_decay_sums(lr, wgb_ref, bgb_ref, cumb_ref, reverse=True)
    q_ref[...] = (proj[:, :GLA_DK] * (GLA_HEAD_K ** -0.5)).astype(BF16)
    k_ref[...] = proj[:, GLA_DK:2 * GLA_DK].astype(BF16)
    v_ref[...] = proj[:, 2 * GLA_DK:2 * GLA_DK + GLA_DV].astype(BF16)
    g_ref[...] = _silu(proj[:, 2 * GLA_DK + GLA_DV:]).astype(BF16)


def _gla_in(x, sh, scl, gpre, w, w_lr, wg, bg, tm):
    bsz, t, d = x.shape
    row = lambda wd: pl.BlockSpec((None, tm, wd), lambda b, i: (b, i, 0))
    vec = lambda wd: pl.BlockSpec((None, 1, wd), lambda b, i: (b, 0, 0))
    full = lambda a: pl.BlockSpec(a.shape, lambda b, i: (0,) * a.ndim)
    return pl.pallas_call(
        _gla_in_kernel,
        grid=(bsz, t // tm),
        in_specs=[row(d), vec(d), vec(d), full(gpre), full(w), full(w_lr),
                  full(wg[0]), full(bg[0]), full(wg[1]), full(bg[1])],
        out_specs=[row(GLA_DK), row(GLA_DK), row(GLA_DV), row(GLA_DV), row(GLA_DK), row(GLA_DK)],
        out_shape=[jax.ShapeDtypeStruct((bsz, t, GLA_DK), BF16),
                   jax.ShapeDtypeStruct((bsz, t, GLA_DK), BF16),
                   jax.ShapeDtypeStruct((bsz, t, GLA_DV), BF16),
                   jax.ShapeDtypeStruct((bsz, t, GLA_DV), BF16),
                   jax.ShapeDtypeStruct((bsz, t, GLA_DK), F32),
                   jax.ShapeDtypeStruct((bsz, t, GLA_DK), F32)],
        compiler_params=_cparams("parallel", "parallel"),
        name="gla_in",
    )(x, sh, scl, gpre, w, w_lr, wg[0], bg[0], wg[1], bg[1])


def _gla_scores_exact(q, k, cum, reverse, kf_s, cm_s, sc_s):
    c = q.shape[0]
    kf_s[...] = k
    cm_s[...] = cum
    sc_s[...] = jnp.zeros_like(sc_s)
    ridx = lax.broadcasted_iota(jnp.int32, (c, 1), 0)
    cidx = lax.broadcasted_iota(jnp.int32, (c, c), 1)

    def column(j, carry):
        kj = kf_s[pl.ds(j, 1), :]
        cj = cm_s[pl.ds(j, 1), :]
        live = (ridx <= j) if reverse else (ridx >= j)
        dec = jnp.exp2(jnp.where(live, jnp.minimum(cum - cj, 0.0), -1e30))
        sj = jnp.sum(q * kj * dec, axis=-1, keepdims=True)
        sc_s[...] += jnp.where(cidx == j, sj, 0.0)
        return carry

    lax.fori_loop(0, c, column, 0)
    return sc_s[...]


def _gla_chunks(q_ref, k_ref, v_ref, cum_ref, s_ref, put_o, kf_s, cm_s, sc_s, *, reverse, factorised):
    tb = q_ref.shape[0]
    c = GLA_CHUNK
    row = lax.broadcasted_iota(jnp.int32, (c, c), 0)
    col = lax.broadcasted_iota(jnp.int32, (c, c), 1)
    keep = (row <= col) if reverse else (row >= col)
    chunks = range(tb // c)
    for i in (reversed(chunks) if reverse else chunks):
        rows = slice(i * c, (i + 1) * c)
        for hd in range(GLA_HEADS):
            kc = slice(hd * GLA_HEAD_K, (hd + 1) * GLA_HEAD_K)
            vc = slice(hd * GLA_HEAD_V, (hd + 1) * GLA_HEAD_V)
            cum = cum_ref[rows, kc]
            last = cum[0:1] if reverse else cum[c - 1:c]
            q = q_ref[rows, kc].astype(F32)
            k = k_ref[rows, kc].astype(F32)
            v = v_ref[rows, vc]
            qe = (q * jnp.exp2(cum)).astype(BF16)
            kl = (k * jnp.exp2(last - cum)).astype(BF16)
            st = s_ref[hd]
            if factorised:
                ke = (k * jnp.exp2(-cum)).astype(BF16)
                scores = lax.dot_general(qe, ke, (((1,), (1,)), ((), ())), preferred_element_type=F32)
                scores = jnp.where(keep, scores, 0.0)
            else:
                scores = _gla_scores_exact(q, k, cum, reverse, kf_s, cm_s, sc_s)
            o = lax.dot_general(qe, st.astype(BF16), (((1,), (1,)), ((), ())),
                                preferred_element_type=F32)
            o = o + _dot(scores.astype(BF16), v)
            put_o(rows, vc, o)
            upd = lax.dot_general(v, kl, (((0,), (0,)), ((), ())), preferred_element_type=F32)
            s_ref[hd] = st * jnp.exp2(last) + upd


def _gla_scan_kernel(*refs, reverse, fuse_out):
    if fuse_out:
        (q_ref, k_ref, v_ref, cum_ref, s0_ref, of_ref, g_ref, x_ref, gt_ref, gn_ref, gpost_ref, wout_ref,
         y_ref, sfin_ref, s_ref, kf_s, cm_s, sc_s, o_s) = refs
    else:
        q_ref, k_ref, v_ref, cum_ref, s0_ref, y_ref, sfin_ref, s_ref, kf_s, cm_s, sc_s = refs
    n = pl.program_id(1)

    @pl.when(n == 0)
    def _():
        s_ref[...] = s0_ref[...]

    if fuse_out:
        def put_o(rows, vc, o):
            o_s[rows, vc] = o + of_ref[rows, vc].astype(F32)
    else:
        def put_o(rows, vc, o):
            y_ref[rows, vc] = o.astype(BF16)

    c = GLA_CHUNK
    ends = [cum_ref[i * c:i * c + 1, :] if reverse else cum_ref[(i + 1) * c - 1:(i + 1) * c, :]
            for i in range(q_ref.shape[0] // c)]
    safe = jnp.min(functools.reduce(jnp.minimum, ends)) >= GLA_FACTORISED_MIN_CUM
    scan = functools.partial(_gla_chunks, q_ref, k_ref, v_ref, cum_ref, s_ref, put_o, kf_s, cm_s, sc_s,
                             reverse=reverse)
    pl.when(safe)(functools.partial(scan, factorised=True))
    pl.when(jnp.logical_not(safe))(functools.partial(scan, factorised=False))

    if fuse_out:
        gn = gn_ref[...]
        parts = []
        for hd in range(GLA_HEADS):
            oh = o_s[:, hd * GLA_HEAD_V:(hd + 1) * GLA_HEAD_V]
            ms = jnp.mean(oh * oh, axis=-1, keepdims=True)
            parts.append(oh * lax.rsqrt(ms + RMS_EPS) * gn)
        on = jnp.concatenate(parts, axis=-1) * g_ref[...].astype(F32)
        y = _dot(on.astype(BF16), wout_ref[...])
        y_ref[...] = _post_norm_residual(x_ref[...], y, gpost_ref[...], gt_ref[...])

    @pl.when(n == pl.num_programs(1) - 1)
    def _():
        sfin_ref[...] = s_ref[...]


def _gla_scan(q, k, v, cum, s0, tb, reverse, out_args=None):
    bsz, t, _ = q.shape
    nblk = t // tb
    c = GLA_CHUNK
    pos = (lambda n: nblk - 1 - n) if reverse else (lambda n: n)
    row = lambda wd: pl.BlockSpec((None, tb, wd), lambda b, n: (b, pos(n), 0))
    vec = lambda wd: pl.BlockSpec((None, 1, wd), lambda b, n: (b, 0, 0))
    full = lambda a: pl.BlockSpec(a.shape, lambda b, n: (0,) * a.ndim)
    st = pl.BlockSpec((None, GLA_HEADS, GLA_HEAD_V, GLA_HEAD_K), lambda b, n: (b, 0, 0, 0))
    in_specs = [row(GLA_DK), row(GLA_DK), row(GLA_DV), row(GLA_DK), st]
    scratch = [pltpu.VMEM((GLA_HEADS, GLA_HEAD_V, GLA_HEAD_K), F32),
               pltpu.VMEM((c, GLA_HEAD_K), F32), pltpu.VMEM((c, GLA_HEAD_K), F32), pltpu.VMEM((c, c), F32)]
    args = [q, k, v, cum, s0]
    if out_args is None:
        y_spec, y_shape = row(GLA_DV), jax.ShapeDtypeStruct((bsz, t, GLA_DV), BF16)
    else:
        o_other, g, x, gt, gn, gpost, w_out = out_args
        d = x.shape[-1]
        in_specs += [row(GLA_DV), row(GLA_DV), row(d), vec(d), full(gn), full(gpost), full(w_out)]
        args += [o_other, g, x, gt, gn, gpost, w_out]
        scratch.append(pltpu.VMEM((tb, GLA_DV), F32))
        y_spec, y_shape = row(d), jax.ShapeDtypeStruct((bsz, t, d), F32)
    return pl.pallas_call(
        functools.partial(_gla_scan_kernel, reverse=reverse, fuse_out=out_args is not None),
        grid=(bsz, nblk),
        in_specs=in_specs,
        out_specs=[y_spec, st],
        out_shape=[y_shape, jax.ShapeDtypeStruct(s0.shape, F32)],
        scratch_shapes=scratch,
        compiler_params=_cparams("parallel", "arbitrary"),
        name="gla_scan_bwd" if reverse else "gla_scan_fwd",
    )(*args)


def _gla_layer(x, sh, scl, gt, gpre, gpost, w_in, w_lr, wg, bg, gn, w_out, s0_f, s0_b, same_modulation=False):
    bsz, t, d = x.shape
    if same_modulation and t % GLA_CHUNK == 0:
        outs = _gla_in(x.reshape(1, bsz * t, d), sh[:1], scl[:1], gpre, w_in, w_lr, wg, bg,
                       min(GLA_IN_ROWS, bsz * t))
        q, k, v, g, cum_f, cum_b = (a.reshape(bsz, t, a.shape[-1]) for a in outs)
    else:
        q, k, v, g, cum_f, cum_b = _gla_in(x, sh, scl, gpre, w_in, w_lr, wg, bg, min(GLA_IN_ROWS, t))
    tb = min(GLA_SCAN_ROWS, t)
    o_f, s_f = _gla_scan(q, k, v, cum_f, s0_f, min(2 * tb, t), reverse=False)
    x_new, s_b = _gla_scan(q, k, v, cum_b, s0_b, tb, reverse=True,
                           out_args=(o_f, g, x, gt, gn, gpost, w_out))
    return x_new, s_f, s_b


def _lru_scan_kernel(x_ref, xp_ref, xn_ref, sh_ref, scl_ref, gpre_ref, w_ref, cw_ref, cb_ref,
                     waf_ref, baf_ref, wxf_ref, bxf_ref, lamf_ref,
                     wab_ref, bab_ref, wxb_ref, bxb_ref, lamb_ref, h0_ref,
                     hsum_ref, pb_ref, g_ref, hbfin_ref, pbfin_ref, ffin_ref,
                     zext_s, hf_s, pf_s, hb_s, pbk_s, carry_s, cin_s, *pre_s):
    g = pl.program_id(1)
    ng = pl.num_programs(1)
    r, s, d = x_ref.shape
    rows = r * s
    w = LRU_WIDTH
    halo_l = CONV_LEFT * s
    n_right = CONV_W - 1 - CONV_LEFT

    @pl.when(g == 0)
    def _():
        cin_s[...] = h0_ref[...]

    sh, scl, gpre = sh_ref[...], scl_ref[...], gpre_ref[...]
    hb16 = _pre_norm_modulate(x_ref[...].reshape(rows, d), gpre, sh, scl).astype(BF16)
    xh = jnp.concatenate([xp_ref[...].reshape(halo_l, d), xn_ref[...].reshape(n_right * s, d)], axis=0)
    zh = _dot(_pre_norm_modulate(xh, gpre, sh, scl).astype(BF16), w_ref[:, :w])
    sub = lax.broadcasted_iota(jnp.int32, (s, LRU_BLOCK_W), 0)

    def in_proj(nb):
        cs = slice(nb * LRU_BLOCK_W, (nb + 1) * LRU_BLOCK_W)
        zext_s[halo_l:halo_l + rows, cs] = _dot(hb16, w_ref[:, cs])
        g_ref[:, cs] = _dot(hb16, w_ref[:, w + nb * LRU_BLOCK_W:w + (nb + 1) * LRU_BLOCK_W]).astype(BF16)
        for j in range(CONV_LEFT):
            src = halo_l + (r - CONV_LEFT + j) * s
            inner = pltpu.roll(zext_s[src:src + s, cs], 1, 0)
            edge = jnp.where(g == 0, 0.0, pltpu.roll(zh[j * s:(j + 1) * s, cs], 1, 0))
            zext_s[j * s:(j + 1) * s, cs] = jnp.where(sub == 0, edge, inner)
        for j in range(n_right):
            src = halo_l + j * s
            inner = pltpu.roll(zext_s[src:src + s, cs], s - 1, 0)
            edge = jnp.where(g == ng - 1, 0.0,
                             pltpu.roll(zh[halo_l + j * s:halo_l + (j + 1) * s, cs], s - 1, 0))
            dst = halo_l + rows + j * s
            zext_s[dst:dst + s, cs] = jnp.where(sub == s - 1, edge, inner)

    def gate_matmuls(nb):
        cs = slice(nb * LRU_BLOCK_W, (nb + 1) * LRU_BLOCK_W)
        hz = cb_ref[:, cs]
        for j in range(CONV_W):
            hz = hz + cw_ref[j:j + 1, cs] * zext_s[j * s:j * s + rows, cs]
        hzb = hz.astype(BF16)
        prf_s, pif_s, prb_s, pib_s, hzc_s = pre_s[5 * (nb % 2):5 * (nb % 2) + 5]
        prf_s[...] = _dot(hzb, waf_ref[nb]) + baf_ref[:, cs]
        pif_s[...] = _dot(hzb, wxf_ref[nb]) + bxf_ref[:, cs]
        prb_s[...] = _dot(hzb, wab_ref[nb]) + bab_ref[:, cs]
        pib_s[...] = _dot(hzb, wxb_ref[nb]) + bxb_ref[:, cs]
        hzc_s[...] = hz

    in_proj(0)
    gate_matmuls(0)
    for nb in range(LRU_BLOCKS):
        cs = slice(nb * LRU_BLOCK_W, (nb + 1) * LRU_BLOCK_W)
        if nb + 1 < LRU_BLOCKS:
            in_proj(nb + 1)
            gate_matmuls(nb + 1)
        prf_s, pif_s, prb_s, pib_s, hzc_s = pre_s[5 * (nb % 2):5 * (nb % 2) + 5]
        c1f = jnp.broadcast_to((-0.5 * LRU_C * LOG2_E) * _softplus(-lamf_ref[:, cs]), (s, LRU_BLOCK_W))
        c1b = jnp.broadcast_to((-0.5 * LRU_C * LOG2_E) * _softplus(-lamb_ref[:, cs]), (s, LRU_BLOCK_W))

        def gate(pr_s, pi_s, c1, rws):
            tr = jnp.tanh(pr_s[rws, :])
            ti = jnp.tanh(pi_s[rws, :])
            hz = hzc_s[rws, :]
            a = jnp.exp2(c1 * tr + c1)
            return a, _sqrt(1.0 - a * a) * (hz * ti + hz)

        hf = hb = jnp.zeros((s, LRU_BLOCK_W), F32)
        pf = pb = jnp.ones((s, LRU_BLOCK_W), F32)
        for t in range(r):
            rf = slice(t * s, (t + 1) * s)
            rb = slice((r - 1 - t) * s, (r - t) * s)
            af, uf = gate(prf_s, pif_s, c1f, rf)
            ab, ub = gate(prb_s, pib_s, c1b, rb)
            hf = af * hf + uf
            pf = af * pf
            hb = ab * hb + ub
            pb = ab * pb
            hf_s[rf, :] = hf
            pf_s[rf, :] = pf
            hb_s[rb, :] = hb
            pbk_s[rb, :] = pb

        cin = cin_s[:, cs]
        for c in range(s):
            carry_s[c:c + 1, :] = cin
            cin = hf[c:c + 1] + pf[c:c + 1] * cin
        cin_s[:, cs] = cin
        carry = carry_s[...][None]
        piece = min(r, 16)
        for i in range(r // piece):
            rws = slice(i * piece * s, (i + 1) * piece * s)
            shape3 = (piece, s, LRU_BLOCK_W)
            hsum = (hf_s[rws, :].reshape(shape3) + pf_s[rws, :].reshape(shape3) * carry
                    + hb_s[rws, :].reshape(shape3))
            hsum_ref[rws, cs] = hsum.reshape(piece * s, LRU_BLOCK_W).astype(BF16)
            pb_ref[rws, cs] = pbk_s[rws, :].astype(BF16)
        hbfin_ref[:, cs] = hb
        pbfin_ref[:, cs] = pb

    ffin_ref[...] = cin_s[...]


def _lru_scan(x5, sh, scl, gpre, w_in, conv_w, conv_b, p_f, p_b, h0_f):
    bsz, r, ng, s, d = x5.shape
    rows = r * s
    w = LRU_WIDTH
    n_right = CONV_W - 1 - CONV_LEFT
    once = pl.Buffered(1)
    cur = pl.BlockSpec((None, r, None, s, d), lambda b, g: (b, 0, g, 0, 0))
    prev = pl.BlockSpec((None, CONV_LEFT, None, s, d),
                        lambda b, g: (b, r // CONV_LEFT - 1, jnp.maximum(g - 1, 0), 0, 0))
    nxt = pl.BlockSpec((None, n_right, None, s, d), lambda b, g: (b, 0, jnp.minimum(g + 1, ng - 1), 0, 0))
    vec = lambda wd: pl.BlockSpec((None, 1, wd), lambda b, g: (b, 0, 0))
    full = lambda a: pl.BlockSpec(a.shape, lambda b, g: (0,) * a.ndim, pipeline_mode=once)
    row = pl.BlockSpec((None, rows, w), lambda b, g: (b, g, 0))
    fin = pl.BlockSpec((None, None, s, w), lambda b, g: (b, g, 0, 0))
    blk = lambda: pltpu.VMEM((rows, LRU_BLOCK_W), F32)
    return pl.pallas_call(
        _lru_scan_kernel,
        grid=(bsz, ng),
        in_specs=[cur, prev, nxt, vec(d), vec(d), full(gpre), full(w_in), full(conv_w), full(conv_b)]
                 + [full(a) for a in p_f] + [full(a) for a in p_b] + [vec(w)],
        out_specs=[row, row, row, fin, fin, vec(w)],
        out_shape=[jax.ShapeDtypeStruct((bsz, ng * rows, w), BF16),
                   jax.ShapeDtypeStruct((bsz, ng * rows, w), BF16),
                   jax.ShapeDtypeStruct((bsz, ng * rows, w), BF16),
                   jax.ShapeDtypeStruct((bsz, ng, s, w), F32),
                   jax.ShapeDtypeStruct((bsz, ng, s, w), F32),
                   jax.ShapeDtypeStruct((bsz, 1, w), F32)],
        scratch_shapes=[pltpu.VMEM((rows + (CONV_W - 1) * s, w), F32)] + [blk() for _ in range(4)]
                       + [pltpu.VMEM((s, LRU_BLOCK_W), F32), pltpu.VMEM((1, w), F32)]
                       + [blk() for _ in range(10)],
        compiler_params=_cparams("parallel", "arbitrary"),
        name="lru_scan",
    )(x5, x5, x5, sh, scl, gpre, w_in, conv_w, conv_b, *p_f, *p_b, h0_f)


def _lru_carry_kernel(h0_ref, hfin_ref, pfin_ref, carry_ref, final_ref):
    c = h0_ref[...]
    for k in range(hfin_ref.shape[0] - 1, -1, -1):
        carry_ref[k:k + 1, :] = c
        c = hfin_ref[k:k + 1, :] + pfin_ref[k:k + 1, :] * c
    final_ref[...] = c


def _lru_carry(h0, hfin, pfin):
    bsz, ng, s, w = hfin.shape
    vec = pl.BlockSpec((None, 1, w), lambda b: (b, 0, 0))
    runs = pl.BlockSpec((None, ng * s, w), lambda b: (b, 0, 0))
    carry, final = pl.pallas_call(
        _lru_carry_kernel,
        grid=(bsz,),
        in_specs=[vec, runs, runs],
        out_specs=[runs, vec],
        out_shape=[jax.ShapeDtypeStruct((bsz, ng * s, w), F32), jax.ShapeDtypeStruct((bsz, 1, w), F32)],
        compiler_params=_cparams("parallel"),
        name="lru_carry",
    )(h0, hfin.reshape(bsz, ng * s, w), pfin.reshape(bsz, ng * s, w))
    return carry.reshape(bsz, ng, s, w), final


def _lru_out_kernel(hsum_ref, pb_ref, cb_ref, g_ref, x_ref, gt_ref, gpost_ref, w_ref, o_ref):
    r, s, d = x_ref.shape
    w = hsum_ref.shape[-1]
    h = hsum_ref[...].astype(F32).reshape(r, s, w) + pb_ref[...].astype(F32).reshape(r, s, w) * cb_ref[...][None]
    y = _dot((h.reshape(r * s, w) * _silu(g_ref[...].astype(F32))).astype(BF16), w_ref[...])
    x = x_ref[...].reshape(r * s, d)
    o_ref[...] = _post_norm_residual(x, y, gpost_ref[...], gt_ref[...]).reshape(r, s, d)


def _lru_out(hsum, pb, cb, g, x5, gt, gpost, w, rb):
    bsz, r, ng, s, d = x5.shape
    nr = r // rb
    tm = rb * s
    width = hsum.shape[-1]
    xio = pl.BlockSpec((None, rb, None, s, d), lambda b, g, i: (b, i, g, 0, 0))
    row = pl.BlockSpec((None, tm, width), lambda b, g, i: (b, g * nr + i, 0))
    runs = pl.BlockSpec((None, None, s, width), lambda b, g, i: (b, g, 0, 0))
    vec = pl.BlockSpec((None, 1, d), lambda b, g, i: (b, 0, 0))
    full = lambda a: pl.BlockSpec(a.shape, lambda b, g, i: (0,) * a.ndim)
    return pl.pallas_call(
        _lru_out_kernel,
        grid=(bsz, ng, nr),
        in_specs=[row, row, runs, row, xio, vec, full(gpost), full(w)],
        out_specs=xio,
        out_shape=jax.ShapeDtypeStruct(x5.shape, F32),
        compiler_params=_cparams("parallel", "parallel", "parallel"),
        name="lru_out",
    )(hsum, pb, cb, g, x5, gt, gpost, w)


def kernel(x, c, ctx, c_ctx, ada_w, ada_b, norm_pre, norm_post, gla_w_in, gla_wg_f, gla_bg_f, gla_wg_b, gla_bg_b, gla_norm, gla_w_out, lru_w_in, lru_conv_w, lru_conv_b, lru_wa_f, lru_ba_f, lru_wx_f, lru_bx_f, lru_lam_f, lru_wa_b, lru_ba_b, lru_wx_b, lru_bx_b, lru_lam_b, lru_w_out):
    bsz, seq, d = x.shape
    ctx_len = ctx.shape[1]
    rows = seq // GRID_W

    cvec = jnp.concatenate([c, c_ctx[None], jnp.zeros((8 - bsz - 1, d), F32)], axis=0)
    mod = _ada_modulation(cvec, ada_w, ada_b)

    def mods(i):
        lat = [mod[i, :bsz, None, j * d:(j + 1) * d] for j in range(3)]
        con = [jnp.broadcast_to(mod[i, bsz, None, None, j * d:(j + 1) * d], (bsz, 1, d)) for j in range(3)]
        return lat, con

    (sh, scl, gt), (sh_c, scl_c, gt_c) = mods(0)
    gpre, gpost = norm_pre[0][None], norm_post[0][None]
    rk = GLA_GATE_RANK
    n_main = 2 * GLA_DK + 2 * GLA_DV
    w_in = gla_w_in[0][:, :n_main].astype(BF16)
    w_lr = jnp.pad(gla_w_in[0][:, n_main:], ((0, 0), (0, 128 - 2 * rk))).astype(BF16)
    wg = (jnp.pad(gla_wg_f[0], ((0, 128 - rk), (0, 0))).astype(BF16),
          jnp.pad(gla_wg_b[0], ((rk, 128 - 2 * rk), (0, 0))).astype(BF16))
    bg = (gla_bg_f[0][None], gla_bg_b[0][None])
    gla_args = (gpre, gpost, w_in, w_lr, wg, bg, gla_norm[0][None], gla_w_out[0].astype(BF16))
    s0 = jnp.zeros((bsz, GLA_HEADS, GLA_HEAD_V, GLA_HEAD_K), F32)
    ctx, s_f, s_b = _gla_layer(ctx, sh_c, scl_c, gt_c, *gla_args, s0, s0, same_modulation=True)
    x, _, _ = _gla_layer(x, sh, scl, gt, *gla_args, s_f, s_b)

    (sh, scl, gt), (sh_c, scl_c, _) = mods(1)
    gpre, gpost = norm_pre[1][None], norm_post[1][None]
    vec = lambda a: a[None]
    gate = lambda wgt, bias: (wgt.astype(BF16), vec(0.5 * bias))
    p_f = (*gate(lru_wa_f[0], lru_ba_f[0]), *gate(lru_wx_f[0], lru_bx_f[0]), vec(lru_lam_f[0]))
    p_b = (*gate(lru_wa_b[0], lru_ba_b[0]), *gate(lru_wx_b[0], lru_bx_b[0]), vec(lru_lam_b[0]))
    scan_args = (gpre, lru_w_in[0].astype(BF16), 0.5 * lru_conv_w[0], vec(0.5 * lru_conv_b[0]), p_f, p_b)
    h0 = jnp.zeros((bsz, 1, LRU_WIDTH), F32)
    cstep = ctx_len // N_SEG
    ctx5 = ctx.reshape(bsz, N_SEG, cstep, d).transpose(0, 2, 1, 3).reshape(bsz, cstep, 1, N_SEG, d)
    _, _, _, hbfin, pbfin, s_f = _lru_scan(ctx5, sh_c, scl_c, *scan_args, h0)
    _, s_b = _lru_carry(h0, hbfin, pbfin)
    x5 = x.reshape(bsz, rows, GRID_W // N_SEG, N_SEG, d)
    hsum, pb, g, hbfin, pbfin, _ = _lru_scan(x5, sh, scl, *scan_args, s_f)
    cb, _ = _lru_carry(s_b, hbfin, pbfin)
    out5 = _lru_out(hsum, pb, cb, g, x5, gt, gpost, lru_w_out[0].astype(BF16), rb=min(LRU_OUT_STEPS, rows))
    return out5.reshape(bsz, seq, d)
```

```python
import functools

import jax
import jax.numpy as jnp
from jax import lax
from jax.experimental import pallas as pl
from jax.experimental.pallas import tpu as pltpu

F32 = jnp.float32
BF16 = jnp.bfloat16

RMS_EPS = 1e-6
LOG2_E = 1.4426950408889634
GRID_W = 64
GLA_HEADS = 4
GLA_HEAD_K = 128
GLA_HEAD_V = 256
GLA_DK = GLA_HEADS * GLA_HEAD_K
GLA_DV = GLA_HEADS * GLA_HEAD_V
GLA_GATE_RANK = 16
GLA_GATE_NORM = 16.0
GLA_CHUNK = 256
GLA_FACTORISED_MIN_CUM = -86.0
GLA_SUM_PIECE = 128
GLA_IN_ROWS = 1024
GLA_SCAN_ROWS = 1024
LRU_BLOCKS = 5
LRU_BLOCK_W = 256
LRU_WIDTH = LRU_BLOCKS * LRU_BLOCK_W
LRU_C = 8.0
CONV_W = 4
CONV_LEFT = CONV_W // 2
N_SEG = 8
LRU_OUT_STEPS = 128

VMEM_LIMIT_BYTES = 56 * 1024 * 1024


def _cparams(*sem):
    return pltpu.CompilerParams(dimension_semantics=sem, vmem_limit_bytes=VMEM_LIMIT_BYTES)


def _silu(x):
    hx = 0.5 * x
    return hx * jnp.tanh(hx) + hx


def _sqrt(x):
    return x * lax.rsqrt(jnp.maximum(x, 1e-30))


def _log_sigmoid(x):
    return jnp.minimum(x, 0.0) - jnp.log(1.0 + jnp.exp(-jnp.abs(x)))


def _softplus(x):
    return jnp.maximum(x, 0.0) + jnp.log1p(jnp.exp(-jnp.abs(x)))


def _split_bf16(x):
    hi = x.astype(BF16)
    lo = (x - hi.astype(F32)).astype(BF16)
    return hi, lo


def _dot(a, b):
    return jnp.dot(a, b, preferred_element_type=F32)


def _dot_x3(a, b):
    a_hi, a_lo = _split_bf16(a)
    b_hi, b_lo = _split_bf16(b)
    return _dot(a_hi, b_hi) + (_dot(a_hi, b_lo) + _dot(a_lo, b_hi))


def _pre_norm_modulate(x, gpre, sh, scl):
    ms = jnp.mean(x * x, axis=-1, keepdims=True)
    return (x * lax.rsqrt(ms + RMS_EPS)) * (gpre * (1.0 + scl)) + sh


def _post_norm_residual(x, y, gpost, gt):
    ms = jnp.mean(y * y, axis=-1, keepdims=True)
    return x + (y * lax.rsqrt(ms + RMS_EPS)) * (gt * gpost)


def _ada_kernel(c_ref, w_ref, b_ref, o_ref):
    sc = _silu(c_ref[...])
    o_ref[...] = _dot_x3(sc, w_ref[...]) + b_ref[...]


def _ada_modulation(cvec, ada_w, ada_b):
    depth, d, n3 = ada_w.shape
    tn = 1024
    return pl.pallas_call(
        _ada_kernel,
        grid=(depth, n3 // tn),
        in_specs=[pl.BlockSpec((8, d), lambda i, j: (0, 0)),
                  pl.BlockSpec((None, d, tn), lambda i, j: (i, 0, j)),
                  pl.BlockSpec((None, 1, tn), lambda i, j: (i, 0, j))],
        out_specs=pl.BlockSpec((None, 8, tn), lambda i, j: (i, 0, j)),
        out_shape=jax.ShapeDtypeStruct((depth, 8, n3), F32),
        compiler_params=_cparams("parallel", "parallel"),
        name="ada_modulation",
    )(cvec, ada_w, ada_b.reshape(depth, 1, n3))


def _gla_log_decay_sums(lr, wg_ref, bg_ref, cum_ref, reverse):
    c, p = GLA_CHUNK, GLA_SUM_PIECE
    row = lax.broadcasted_iota(jnp.int32, (p, 2 * p), 0)
    col = lax.broadcasted_iota(jnp.int32, (p, 2 * p), 1) % p
    tri = ((row <= col) if reverse else (row >= col)).astype(BF16)
    z = _dot(lr, wg_ref[...]) + bg_ref[...]
    log_a = _log_sigmoid(z) * (LOG2_E / GLA_GATE_NORM)
    hi, lo = _split_bf16(log_a)
    for i in range(lr.shape[0] // c):
        carry = None
        pieces = range(c // p)
        for j in (reversed(pieces) if reverse else pieces):
            rows = slice(i * c + j * p, i * c + (j + 1) * p)
            s = _dot(tri, jnp.concatenate([hi[rows], lo[rows]], axis=0))
            if carry is not None:
                s = s + carry
            cum_ref[rows, :] = s
            carry = s[0:1] if reverse else s[p - 1:p]


def _gla_in_kernel(x_ref, sh_ref, scl_ref, gpre_ref, w_ref, wlr_ref, wgf_ref, bgf_ref, wgb_ref, bgb_ref, s0_ref,
                   q_ref, k_ref, v_ref, g_ref, cumb_ref, of_ref, sfin_ref,
                   s_ref, cumf_ref, kf_s, cm_s, sc_s):
    n = pl.program_id(1)

    @pl.when(n == 0)
    def _():
        s_ref[...] = s0_ref[...]

    hb16 = _pre_norm_modulate(x_ref[...], gpre_ref[...], sh_ref[...], scl_ref[...]).astype(BF16)
    lr = _dot(hb16, wlr_ref[...]).astype(BF16)
    proj = _dot(hb16, w_ref[...])
    _gla_log_decay_sums(lr, wgf_ref, bgf_ref, cumf_ref, reverse=False)
    _gla_log_decay_sums(lr, wgb_ref, bgb_ref, cumb_ref, reverse=True)
    q_ref[...] = (proj[:, :GLA_DK] * (GLA_HEAD_K ** -0.5)).astype(BF16)
    k_ref[...] = proj[:, GLA_DK:2 * GLA_DK].astype(BF16)
    v_ref[...] = proj[:, 2 * GLA_DK:2 * GLA_DK + GLA_DV].astype(BF16)
    g_ref[...] = _silu(proj[:, 2 * GLA_DK + GLA_DV:]).astype(BF16)

    def put_o(rows, vc, o):
        of_ref[rows, vc] = o.astype(BF16)

    _gla_guarded_scan(q_ref, k_ref, v_ref, cumf_ref, s_ref, put_o, kf_s, cm_s, sc_s, reverse=False)

    @pl.when(n == pl.num_programs(1) - 1)
    def _():
        sfin_ref[...] = s_ref[...]


def _gla_in(x, sh, scl, gpre, w, w_lr, wg, bg, s0, tm):
    bsz, t, d = x.shape
    c = GLA_CHUNK
    row = lambda wd: pl.BlockSpec((None, tm, wd), lambda b, i: (b, i, 0))
    vec = lambda wd: pl.BlockSpec((None, 1, wd), lambda b, i: (b, 0, 0))
    full = lambda a: pl.BlockSpec(a.shape, lambda b, i: (0,) * a.ndim)
    st = pl.BlockSpec((None, GLA_HEADS, GLA_HEAD_V, GLA_HEAD_K), lambda b, i: (b, 0, 0, 0))
    return pl.pallas_call(
        _gla_in_kernel,
        grid=(bsz, t // tm),
        in_specs=[row(d), vec(d), vec(d), full(gpre), full(w), full(w_lr),
                  full(wg[0]), full(bg[0]), full(wg[1]), full(bg[1]), st],
        out_specs=[row(GLA_DK), row(GLA_DK), row(GLA_DV), row(GLA_DV), row(GLA_DK), row(GLA_DV), st],
        out_shape=[jax.ShapeDtypeStruct((bsz, t, GLA_DK), BF16),
                   jax.ShapeDtypeStruct((bsz, t, GLA_DK), BF16),
                   jax.ShapeDtypeStruct((bsz, t, GLA_DV), BF16),
                   jax.ShapeDtypeStruct((bsz, t, GLA_DV), BF16),
                   jax.ShapeDtypeStruct((bsz, t, GLA_DK), F32),
                   jax.ShapeDtypeStruct((bsz, t, GLA_DV), BF16),
                   jax.ShapeDtypeStruct(s0.shape, F32)],
        scratch_shapes=[pltpu.VMEM((GLA_HEADS, GLA_HEAD_V, GLA_HEAD_K), F32), pltpu.VMEM((tm, GLA_DK), F32),
                        pltpu.VMEM((c, GLA_HEAD_K), F32), pltpu.VMEM((c, GLA_HEAD_K), F32),
                        pltpu.VMEM((c, c), F32)],
        compiler_params=_cparams("parallel", "arbitrary"),
        name="gla_in_fwd",
    )(x, sh, scl, gpre, w, w_lr, wg[0], bg[0], wg[1], bg[1], s0)


def _gla_scores_exact(q, k, cum, reverse, kf_s, cm_s, sc_s):
    c = q.shape[0]
    kf_s[...] = k
    cm_s[...] = cum
    sc_s[...] = jnp.zeros_like(sc_s)
    ridx = lax.broadcasted_iota(jnp.int32, (c, 1), 0)
    cidx = lax.broadcasted_iota(jnp.int32, (c, c), 1)

    def column(j, carry):
        kj = kf_s[pl.ds(j, 1), :]
        cj = cm_s[pl.ds(j, 1), :]
        live = (ridx <= j) if reverse else (ridx >= j)
        dec = jnp.exp2(jnp.where(live, jnp.minimum(cum - cj, 0.0), -1e30))
        sj = jnp.sum(q * kj * dec, axis=-1, keepdims=True)
        sc_s[...] += jnp.where(cidx == j, sj, 0.0)
        return carry

    lax.fori_loop(0, c, column, 0)
    return sc_s[...]


def _gla_chunks(q_ref, k_ref, v_ref, cum_ref, s_ref, put_o, kf_s, cm_s, sc_s, *, reverse, factorised):
    tb = q_ref.shape[0]
    c = GLA_CHUNK
    row = lax.broadcasted_iota(jnp.int32, (c, c), 0)
    col = lax.broadcasted_iota(jnp.int32, (c, c), 1)
    keep = (row <= col) if reverse else (row >= col)
    chunks = range(tb // c)
    for i in (reversed(chunks) if reverse else chunks):
        rows = slice(i * c, (i + 1) * c)
        for hd in range(GLA_HEADS):
            kc = slice(hd * GLA_HEAD_K, (hd + 1) * GLA_HEAD_K)
            vc = slice(hd * GLA_HEAD_V, (hd + 1) * GLA_HEAD_V)
            cum = cum_ref[rows, kc]
            last = cum[0:1] if reverse else cum[c - 1:c]
            q = q_ref[rows, kc].astype(F32)
            k = k_ref[rows, kc].astype(F32)
            v = v_ref[rows, vc]
            qe = (q * jnp.exp2(cum)).astype(BF16)
            kl = (k * jnp.exp2(last - cum)).astype(BF16)
            st = s_ref[hd]
            if factorised:
                ke = (k * jnp.exp2(-cum)).astype(BF16)
                scores = lax.dot_general(qe, ke, (((1,), (1,)), ((), ())), preferred_element_type=F32)
                scores = jnp.where(keep, scores, 0.0)
            else:
                scores = _gla_scores_exact(q, k, cum, reverse, kf_s, cm_s, sc_s)
            o = lax.dot_general(qe, st.astype(BF16), (((1,), (1,)), ((), ())),
                                preferred_element_type=F32)
            o = o + _dot(scores.astype(BF16), v)
            put_o(rows, vc, o)
            upd = lax.dot_general(v, kl, (((0,), (0,)), ((), ())), preferred_element_type=F32)
            s_ref[hd] = st * jnp.exp2(last) + upd


def _gla_guarded_scan(q_ref, k_ref, v_ref, cum_ref, s_ref, put_o, kf_s, cm_s, sc_s, *, reverse):
    c = GLA_CHUNK
    ends = [cum_ref[i * c:i * c + 1, :] if reverse else cum_ref[(i + 1) * c - 1:(i + 1) * c, :]
            for i in range(q_ref.shape[0] // c)]
    safe = jnp.min(functools.reduce(jnp.minimum, ends)) >= GLA_FACTORISED_MIN_CUM
    scan = functools.partial(_gla_chunks, q_ref, k_ref, v_ref, cum_ref, s_ref, put_o, kf_s, cm_s, sc_s,
                             reverse=reverse)
    pl.when(safe)(functools.partial(scan, factorised=True))
    pl.when(jnp.logical_not(safe))(functools.partial(scan, factorised=False))


def _gla_bwd_out_kernel(q_ref, k_ref, v_ref, cum_ref, s0_ref, of_ref, g_ref, x_ref, gt_ref, gn_ref, gpost_ref,
                        wout_ref, y_ref, sfin_ref, s_ref, kf_s, cm_s, sc_s, o_s):
    n = pl.program_id(1)

    @pl.when(n == 0)
    def _():
        s_ref[...] = s0_ref[...]

    def put_o(rows, vc, o):
        o_s[rows, vc] = o + of_ref[rows, vc].astype(F32)

    _gla_guarded_scan(q_ref, k_ref, v_ref, cum_ref, s_ref, put_o, kf_s, cm_s, sc_s, reverse=True)

    gn = gn_ref[...]
    parts = []
    for hd in range(GLA_HEADS):
        oh = o_s[:, hd * GLA_HEAD_V:(hd + 1) * GLA_HEAD_V]
        ms = jnp.mean(oh * oh, axis=-1, keepdims=True)
        parts.append(oh * lax.rsqrt(ms + RMS_EPS) * gn)
    on = jnp.concatenate(parts, axis=-1) * g_ref[...].astype(F32)
    y = _dot(on.astype(BF16), wout_ref[...])
    y_ref[...] = _post_norm_residual(x_ref[...], y, gpost_ref[...], gt_ref[...])

    @pl.when(n == pl.num_programs(1) - 1)
    def _():
        sfin_ref[...] = s_ref[...]


def _gla_bwd_out(q, k, v, cum, s0, o_f, g, x, gt, gn, gpost, w_out, tb):
    bsz, t, d = x.shape
    nblk = t // tb
    c = GLA_CHUNK
    row = lambda wd: pl.BlockSpec((None, tb, wd), lambda b, n: (b, nblk - 1 - n, 0))
    vec = lambda wd: pl.BlockSpec((None, 1, wd), lambda b, n: (b, 0, 0))
    full = lambda a: pl.BlockSpec(a.shape, lambda b, n: (0,) * a.ndim)
    st = pl.BlockSpec((None, GLA_HEADS, GLA_HEAD_V, GLA_HEAD_K), lambda b, n: (b, 0, 0, 0))
    return pl.pallas_call(
        _gla_bwd_out_kernel,
        grid=(bsz, nblk),
        in_specs=[row(GLA_DK), row(GLA_DK), row(GLA_DV), row(GLA_DK), st,
                  row(GLA_DV), row(GLA_DV), row(d), vec(d), full(gn), full(gpost), full(w_out)],
        out_specs=[row(d), st],
        out_shape=[jax.ShapeDtypeStruct((bsz, t, d), F32), jax.ShapeDtypeStruct(s0.shape, F32)],
        scratch_shapes=[pltpu.VMEM((GLA_HEADS, GLA_HEAD_V, GLA_HEAD_K), F32),
                        pltpu.VMEM((c, GLA_HEAD_K), F32), pltpu.VMEM((c, GLA_HEAD_K), F32), pltpu.VMEM((c, c), F32),
                        pltpu.VMEM((tb, GLA_DV), F32)],
        compiler_params=_cparams("parallel", "arbitrary"),
        name="gla_bwd_out",
    )(q, k, v, cum, s0, o_f, g, x, gt, gn, gpost, w_out)


def _gla_layer(x, sh, scl, gt, gpre, gpost, w_in, w_lr, wg, bg, gn, w_out, s0_f, s0_b):
    t = x.shape[1]
    q, k, v, g, cum_b, o_f, s_f = _gla_in(x, sh, scl, gpre, w_in, w_lr, wg, bg, s0_f, min(GLA_IN_ROWS, t))
    x_new, s_b = _gla_bwd_out(q, k, v, cum_b, s0_b, o_f, g, x, gt, gn, gpost, w_out, min(GLA_SCAN_ROWS, t))
    return x_new, s_f, s_b


def _lru_scan_kernel(x_ref, xp_ref, xn_ref, sh_ref, scl_ref, gpre_ref, w_ref, cw_ref, cb_ref,
                     waf_ref, baf_ref, wxf_ref, bxf_ref, lamf_ref,
                     wab_ref, bab_ref, wxb_ref, bxb_ref, lamb_ref, h0_ref,
                     hsum_ref, pb_ref, g_ref, hbfin_ref, pbfin_ref, ffin_ref,
                     zext_s, hf_s, pf_s, hb_s, pbk_s, carry_s, cin_s, *pre_s):
    g = pl.program_id(1)
    ng = pl.num_programs(1)
    r, s, d = x_ref.shape
    rows = r * s
    w = LRU_WIDTH
    halo_l = CONV_LEFT * s
    n_right = CONV_W - 1 - CONV_LEFT

    @pl.when(g == 0)
    def _():
        cin_s[...] = h0_ref[...]

    sh, scl, gpre = sh_ref[...], scl_ref[...], gpre_ref[...]
    hb16 = _pre_norm_modulate(x_ref[...].reshape(rows, d), gpre, sh, scl).astype(BF16)
    xh = jnp.concatenate([xp_ref[...].reshape(halo_l, d), xn_ref[...].reshape(n_right * s, d)], axis=0)
    zh = _dot(_pre_norm_modulate(xh, gpre, sh, scl).astype(BF16), w_ref[:, :w])
    sub = lax.broadcasted_iota(jnp.int32, (s, LRU_BLOCK_W), 0)

    def in_proj(nb):
        cs = slice(nb * LRU_BLOCK_W, (nb + 1) * LRU_BLOCK_W)
        zext_s[halo_l:halo_l + rows, cs] = _dot(hb16, w_ref[:, cs])
        g_ref[:, cs] = _dot(hb16, w_ref[:, w + nb * LRU_BLOCK_W:w + (nb + 1) * LRU_BLOCK_W]).astype(BF16)
        for j in range(CONV_LEFT):
            src = halo_l + (r - CONV_LEFT + j) * s
            inner = pltpu.roll(zext_s[src:src + s, cs], 1, 0)
            edge = jnp.where(g == 0, 0.0, pltpu.roll(zh[j * s:(j + 1) * s, cs], 1, 0))
            zext_s[j * s:(j + 1) * s, cs] = jnp.where(sub == 0, edge, inner)
        for j in range(n_right):
            src = halo_l + j * s
            inner = pltpu.roll(zext_s[src:src + s, cs], s - 1, 0)
            edge = jnp.where(g == ng - 1, 0.0,
                             pltpu.roll(zh[halo_l + j * s:halo_l + (j + 1) * s, cs], s - 1, 0))
            dst = halo_l + rows + j * s
            zext_s[dst:dst + s, cs] = jnp.where(sub == s - 1, edge, inner)

    def gate_matmuls(nb):
        cs = slice(nb * LRU_BLOCK_W, (nb + 1) * LRU_BLOCK_W)
        hz = cb_ref[:, cs]
        for j in range(CONV_W):
            hz = hz + cw_ref[j:j + 1, cs] * zext_s[j * s:j * s + rows, cs]
        hzb = hz.astype(BF16)
        prf_s, pif_s, prb_s, pib_s, hzc_s = pre_s[5 * (nb % 2):5 * (nb % 2) + 5]
        prf_s[...] = _dot(hzb, waf_ref[nb]) + baf_ref[:, cs]
        pif_s[...] = _dot(hzb, wxf_ref[nb]) + bxf_ref[:, cs]
        prb_s[...] = _dot(hzb, wab_ref[nb]) + bab_ref[:, cs]
        pib_s[...] = _dot(hzb, wxb_ref[nb]) + bxb_ref[:, cs]
        hzc_s[...] = hz

    in_proj(0)
    gate_matmuls(0)
    for nb in range(LRU_BLOCKS):
        cs = slice(nb * LRU_BLOCK_W, (nb + 1) * LRU_BLOCK_W)
        if nb + 1 < LRU_BLOCKS:
            in_proj(nb + 1)
            gate_matmuls(nb + 1)
        prf_s, pif_s, prb_s, pib_s, hzc_s = pre_s[5 * (nb % 2):5 * (nb % 2) + 5]
        c1f = jnp.broadcast_to((-0.5 * LRU_C * LOG2_E) * _softplus(-lamf_ref[:, cs]), (s, LRU_BLOCK_W))
        c1b = jnp.broadcast_to((-0.5 * LRU_C * LOG2_E) * _softplus(-lamb_ref[:, cs]), (s, LRU_BLOCK_W))

        def gate(pr_s, pi_s, c1, rws):
            tr = jnp.tanh(pr_s[rws, :])
            ti = jnp.tanh(pi_s[rws, :])
            hz = hzc_s[rws, :]
            a = jnp.exp2(c1 * tr + c1)
            return a, _sqrt(1.0 - a * a) * (hz * ti + hz)

        hf = hb = jnp.zeros((s, LRU_BLOCK_W), F32)
        pf = pb = jnp.ones((s, LRU_BLOCK_W), F32)
        for t in range(r):
            rf = slice(t * s, (t + 1) * s)
            rb = slice((r - 1 - t) * s, (r - t) * s)
            af, uf = gate(prf_s, pif_s, c1f, rf)
            ab, ub = gate(prb_s, pib_s, c1b, rb)
            hf = af * hf + uf
            pf = af * pf
            hb = ab * hb + ub
            pb = ab * pb
            hf_s[rf, :] = hf
            pf_s[rf, :] = pf
            hb_s[rb, :] = hb
            pbk_s[rb, :] = pb

        cin = cin_s[:, cs]
        for c in range(s):
            carry_s[c:c + 1, :] = cin
            cin = hf[c:c + 1] + pf[c:c + 1] * cin
        cin_s[:, cs] = cin
        carry = carry_s[...][None]
        piece = min(r, 16)
        for i in range(r // piece):
            rws = slice(i * piece * s, (i + 1) * piece * s)
            shape3 = (piece, s, LRU_BLOCK_W)
            hsum = (hf_s[rws, :].reshape(shape3) + pf_s[rws, :].reshape(shape3) * carry
                    + hb_s[rws, :].reshape(shape3))
            hsum_ref[rws, cs] = hsum.reshape(piece * s, LRU_BLOCK_W).astype(BF16)
            pb_ref[rws, cs] = pbk_s[rws, :].astype(BF16)
        hbfin_ref[:, cs] = hb
        pbfin_ref[:, cs] = pb

    ffin_ref[...] = cin_s[...]


def _lru_scan(x5, sh, scl, gpre, w_in, conv_w, conv_b, p_f, p_b, h0_f):
    bsz, r, ng, s, d = x5.shape
    rows = r * s
    w = LRU_WIDTH
    n_right = CONV_W - 1 - CONV_LEFT
    once = pl.Buffered(1)
    cur = pl.BlockSpec((None, r, None, s, d), lambda b, g: (b, 0, g, 0, 0))
    prev = pl.BlockSpec((None, CONV_LEFT, None, s, d),
                        lambda b, g: (b, r // CONV_LEFT - 1, jnp.maximum(g - 1, 0), 0, 0))
    nxt = pl.BlockSpec((None, n_right, None, s, d), lambda b, g: (b, 0, jnp.minimum(g + 1, ng - 1), 0, 0))
    vec = lambda wd: pl.BlockSpec((None, 1, wd), lambda b, g: (b, 0, 0))
    full = lambda a: pl.BlockSpec(a.shape, lambda b, g: (0,) * a.ndim, pipeline_mode=once)
    row = pl.BlockSpec((None, rows, w), lambda b, g: (b, g, 0))
    fin = pl.BlockSpec((None, None, s, w), lambda b, g: (b, g, 0, 0))
    blk = lambda: pltpu.VMEM((rows, LRU_BLOCK_W), F32)
    return pl.pallas_call(
        _lru_scan_kernel,
        grid=(bsz, ng),
        in_specs=[cur, prev, nxt, vec(d), vec(d), full(gpre), full(w_in), full(conv_w), full(conv_b)]
                 + [full(a) for a in p_f] + [full(a) for a in p_b] + [vec(w)],
        out_specs=[row, row, row, fin, fin, vec(w)],
        out_shape=[jax.ShapeDtypeStruct((bsz, ng * rows, w), BF16),
                   jax.ShapeDtypeStruct((bsz, ng * rows, w), BF16),
                   jax.ShapeDtypeStruct((bsz, ng * rows, w), BF16),
                   jax.ShapeDtypeStruct((bsz, ng, s, w), F32),
                   jax.ShapeDtypeStruct((bsz, ng, s, w), F32),
                   jax.ShapeDtypeStruct((bsz, 1, w), F32)],
        scratch_shapes=[pltpu.VMEM((rows + (CONV_W - 1) * s, w), F32)] + [blk() for _ in range(4)]
                       + [pltpu.VMEM((s, LRU_BLOCK_W), F32), pltpu.VMEM((1, w), F32)]
                       + [blk() for _ in range(10)],
        compiler_params=_cparams("parallel", "arbitrary"),
        name="lru_scan",
    )(x5, x5, x5, sh, scl, gpre, w_in, conv_w, conv_b, *p_f, *p_b, h0_f)


def _lru_carry_kernel(h0_ref, hfin_ref, pfin_ref, carry_ref, final_ref):
    c = h0_ref[...]
    for k in range(hfin_ref.shape[0] - 1, -1, -1):
        carry_ref[k:k + 1, :] = c
        c = hfin_ref[k:k + 1, :] + pfin_ref[k:k + 1, :] * c
    final_ref[...] = c


def _lru_carry(h0, hfin, pfin):
    bsz, ng, s, w = hfin.shape
    vec = pl.BlockSpec((None, 1, w), lambda b: (b, 0, 0))
    runs = pl.BlockSpec((None, ng * s, w), lambda b: (b, 0, 0))
    carry, final = pl.pallas_call(
        _lru_carry_kernel,
        grid=(bsz,),
        in_specs=[vec, runs, runs],
        out_specs=[runs, vec],
        out_shape=[jax.ShapeDtypeStruct((bsz, ng * s, w), F32), jax.ShapeDtypeStruct((bsz, 1, w), F32)],
        compiler_params=_cparams("parallel"),
        name="lru_carry",
    )(h0, hfin.reshape(bsz, ng * s, w), pfin.reshape(bsz, ng * s, w))
    return carry.reshape(bsz, ng, s, w), final


def _lru_out_kernel(hsum_ref, pb_ref, cb_ref, g_ref, x_ref, gt_ref, gpost_ref, w_ref, o_ref):
    r, s, d = x_ref.shape
    w = hsum_ref.shape[-1]
    h = hsum_ref[...].astype(F32).reshape(r, s, w) + pb_ref[...].astype(F32).reshape(r, s, w) * cb_ref[...][None]
    y = _dot((h.reshape(r * s, w) * _silu(g_ref[...].astype(F32))).astype(BF16), w_ref[...])
    x = x_ref[...].reshape(r * s, d)
    o_ref[...] = _post_norm_residual(x, y, gpost_ref[...], gt_ref[...]).reshape(r, s, d)


def _lru_out(hsum, pb, cb, g, x5, gt, gpost, w, rb):
    bsz, r, ng, s, d = x5.shape
    nr = r // rb
    tm = rb * s
    width = hsum.shape[-1]
    xio = pl.BlockSpec((None, rb, None, s, d), lambda b, g, i: (b, i, g, 0, 0))
    row = pl.BlockSpec((None, tm, width), lambda b, g, i: (b, g * nr + i, 0))
    runs = pl.BlockSpec((None, None, s, width), lambda b, g, i: (b, g, 0, 0))
    vec = pl.BlockSpec((None, 1, d), lambda b, g, i: (b, 0, 0))
    full = lambda a: pl.BlockSpec(a.shape, lambda b, g, i: (0,) * a.ndim)
    return pl.pallas_call(
        _lru_out_kernel,
        grid=(bsz, ng, nr),
        in_specs=[row, row, runs, row, xio, vec, full(gpost), full(w)],
        out_specs=xio,
        out_shape=jax.ShapeDtypeStruct(x5.shape, F32),
        compiler_params=_cparams("parallel", "parallel", "parallel"),
        name="lru_out",
    )(hsum, pb, cb, g, x5, gt, gpost, w)


def kernel(x, c, ctx, c_ctx, ada_w, ada_b, norm_pre, norm_post, gla_w_in, gla_wg_f, gla_bg_f, gla_wg_b, gla_bg_b, gla_norm, gla_w_out, lru_w_in, lru_conv_w, lru_conv_b, lru_wa_f, lru_ba_f, lru_wx_f, lru_bx_f, lru_lam_f, lru_wa_b, lru_ba_b, lru_wx_b, lru_bx_b, lru_lam_b, lru_w_out):
    bsz, seq, d = x.shape
    ctx_len = ctx.shape[1]
    rows = seq // GRID_W

    cvec = jnp.concatenate([c, c_ctx[None], jnp.zeros((8 - bsz - 1, d), F32)], axis=0)
    mod = _ada_modulation(cvec, ada_w, ada_b)

    def mods(i):
        lat = [mod[i, :bsz, None, j * d:(j + 1) * d] for j in range(3)]
        con = [jnp.broadcast_to(mod[i, bsz, None, None, j * d:(j + 1) * d], (bsz, 1, d)) for j in range(3)]
        return lat, con

    (sh, scl, gt), (sh_c, scl_c, gt_c) = mods(0)
    gpre, gpost = norm_pre[0][None], norm_post[0][None]
    rk = GLA_GATE_RANK
    n_main = 2 * GLA_DK + 2 * GLA_DV
    w_in = gla_w_in[0][:, :n_main].astype(BF16)
    w_lr = jnp.pad(gla_w_in[0][:, n_main:], ((0, 0), (0, 128 - 2 * rk))).astype(BF16)
    wg = (jnp.pad(gla_wg_f[0], ((0, 128 - rk), (0, 0))).astype(BF16),
          jnp.pad(gla_wg_b[0], ((rk, 128 - 2 * rk), (0, 0))).astype(BF16))
    bg = (gla_bg_f[0][None], gla_bg_b[0][None])
    gla_args = (gpre, gpost, w_in, w_lr, wg, bg, gla_norm[0][None], gla_w_out[0].astype(BF16))
    s0 = jnp.zeros((bsz, GLA_HEADS, GLA_HEAD_V, GLA_HEAD_K), F32)
    ctx, s_f, s_b = _gla_layer(ctx, sh_c, scl_c, gt_c, *gla_args, s0, s0)
    x, _, _ = _gla_layer(x, sh, scl, gt, *gla_args, s_f, s_b)

    (sh, scl, gt), (sh_c, scl_c, _) = mods(1)
    gpre, gpost = norm_pre[1][None], norm_post[1][None]
    vec = lambda a: a[None]
    gate = lambda wgt, bias: (wgt.astype(BF16), vec(0.5 * bias))
    p_f = (*gate(lru_wa_f[0], lru_ba_f[0]), *gate(lru_wx_f[0], lru_bx_f[0]), vec(lru_lam_f[0]))
    p_b = (*gate(lru_wa_b[0], lru_ba_b[0]), *gate(lru_wx_b[0], lru_bx_b[0]), vec(lru_lam_b[0]))
    scan_args = (gpre, lru_w_in[0].astype(BF16), 0.5 * lru_conv_w[0], vec(0.5 * lru_conv_b[0]), p_f, p_b)
    h0 = jnp.zeros((bsz, 1, LRU_WIDTH), F32)
    cstep = ctx_len // N_SEG
    ctx5 = ctx.reshape(bsz, N_SEG, cstep, d).transpose(0, 2, 1, 3).reshape(bsz, cstep, 1, N_SEG, d)
    _, _, _, hbfin, pbfin, s_f = _lru_scan(ctx5, sh_c, scl_c, *scan_args, h0)
    _, s_b = _lru_carry(h0, hbfin, pbfin)
    x5 = x.reshape(bsz, rows, GRID_W // N_SEG, N_SEG, d)
    hsum, pb, g, hbfin, pbfin, _ = _lru_scan(x5, sh, scl, *scan_args, s_f)
    cb, _ = _lru_carry(s_b, hbfin, pbfin)
    out5 = _lru_out(hsum, pb, cb, g, x5, gt, gpost, lru_w_out[0].astype(BF16), rb=min(LRU_OUT_STEPS, rows))
    return out5.reshape(bsz, seq, d)
```

```python
import functools

import jax
import jax.numpy as jnp
from jax import lax
from jax.experimental import pallas as pl
from jax.experimental.pallas import tpu as pltpu

F32 = jnp.float32
BF16 = jnp.bfloat16
SUBLANES, LANES = 8, 128

RMS_EPS = 1e-6
LOG2_E = 1.4426950408889634
GRID_W = 64
GLA_HEADS = 4
GLA_HEAD_K = 128
GLA_HEAD_V = 256
GLA_DK = GLA_HEADS * GLA_HEAD_K
GLA_DV = GLA_HEADS * GLA_HEAD_V
GLA_GATE_RANK = 16
GLA_GATE_NORM = 16.0
GLA_CHUNK = 256
GLA_FACTORISED_MIN_CUM = -86.0
GLA_SUM_PIECE = 128
GLA_IN_ROWS = 1024
GLA_SCAN_ROWS = 1024
LRU_BLOCKS = 5
LRU_BLOCK_W = 256
LRU_WIDTH = LRU_BLOCKS * LRU_BLOCK_W
LRU_C = 8.0
CONV_W = 4
CONV_LEFT = CONV_W // 2
N_SEG = SUBLANES
LRU_OUT_STEPS = 128

VMEM_LIMIT_BYTES = 56 * 1024 * 1024


def _cparams(*sem):
    return pltpu.CompilerParams(dimension_semantics=sem, vmem_limit_bytes=VMEM_LIMIT_BYTES)


def _silu(x):
    hx = 0.5 * x
    return hx * jnp.tanh(hx) + hx


def _sqrt(x):
    return x * lax.rsqrt(jnp.maximum(x, 1e-30))


def _log_sigmoid(x):
    return jnp.minimum(x, 0.0) - jnp.log(1.0 + jnp.exp(-jnp.abs(x)))


def _softplus(x):
    return jnp.maximum(x, 0.0) + jnp.log1p(jnp.exp(-jnp.abs(x)))


def _split_bf16(x):
    hi = x.astype(BF16)
    lo = (x - hi.astype(F32)).astype(BF16)
    return hi, lo


def _dot(a, b):
    return jnp.dot(a, b, preferred_element_type=F32)


def _dot_x3(a, b):
    a_hi, a_lo = _split_bf16(a)
    b_hi, b_lo = _split_bf16(b)
    return _dot(a_hi, b_hi) + (_dot(a_hi, b_lo) + _dot(a_lo, b_hi))


def _pre_norm_modulate(x, gpre, sh, scl):
    ms = jnp.mean(x * x, axis=-1, keepdims=True)
    return (x * lax.rsqrt(ms + RMS_EPS)) * (gpre * (1.0 + scl)) + sh


def _post_norm_residual(x, y, gpost, gt):
    ms = jnp.mean(y * y, axis=-1, keepdims=True)
    return x + (y * lax.rsqrt(ms + RMS_EPS)) * (gt * gpost)


def _ada_kernel(c_ref, w_ref, b_ref, o_ref):
    sc = _silu(c_ref[...])
    o_ref[...] = _dot_x3(sc, w_ref[...]) + b_ref[...]


def _ada_modulation(cvec, ada_w, ada_b):
    depth, d, n3 = ada_w.shape
    tn = 1024
    return pl.pallas_call(
        _ada_kernel,
        grid=(depth, n3 // tn),
        in_specs=[pl.BlockSpec((SUBLANES, d), lambda i, j: (0, 0)),
                  pl.BlockSpec((None, d, tn), lambda i, j: (i, 0, j)),
                  pl.BlockSpec((None, 1, tn), lambda i, j: (i, 0, j))],
        out_specs=pl.BlockSpec((None, SUBLANES, tn), lambda i, j: (i, 0, j)),
        out_shape=jax.ShapeDtypeStruct((depth, SUBLANES, n3), F32),
        compiler_params=_cparams("parallel", "parallel"),
        name="ada_modulation",
    )(cvec, ada_w, ada_b.reshape(depth, 1, n3))


def _gla_log_decay_sums(lr, wg_ref, bg_ref, cum_ref, reverse):
    c, p = GLA_CHUNK, GLA_SUM_PIECE
    row = lax.broadcasted_iota(jnp.int32, (p, 2 * p), 0)
    col = lax.broadcasted_iota(jnp.int32, (p, 2 * p), 1) % p
    tri = ((row <= col) if reverse else (row >= col)).astype(BF16)
    z = _dot(lr, wg_ref[...]) + bg_ref[...]
    log_a = _log_sigmoid(z) * (LOG2_E / GLA_GATE_NORM)
    hi, lo = _split_bf16(log_a)
    for i in range(lr.shape[0] // c):
        carry = None
        pieces = range(c // p)
        for j in (reversed(pieces) if reverse else pieces):
            rows = slice(i * c + j * p, i * c + (j + 1) * p)
            s = _dot(tri, jnp.concatenate([hi[rows], lo[rows]], axis=0))
            if carry is not None:
                s = s + carry
            cum_ref[rows, :] = s
            carry = s[0:1] if reverse else s[p - 1:p]


def _gla_in_kernel(x_ref, sh_ref, scl_ref, gpre_ref, w_ref, wlr_ref, wgf_ref, bgf_ref, wgb_ref, bgb_ref, s0_ref,
                   q_ref, k_ref, v_ref, g_ref, cumb_ref, of_ref, sfin_ref,
                   s_ref, cumf_ref, kf_s, cm_s, sc_s):
    n = pl.program_id(1)

    @pl.when(n == 0)
    def _():
        s_ref[...] = s0_ref[...]

    hb16 = _pre_norm_modulate(x_ref[...], gpre_ref[...], sh_ref[...], scl_ref[...]).astype(BF16)
    lr = _dot(hb16, wlr_ref[...]).astype(BF16)
    proj = _dot(hb16, w_ref[...])
    _gla_log_decay_sums(lr, wgf_ref, bgf_ref, cumf_ref, reverse=False)
    _gla_log_decay_sums(lr, wgb_ref, bgb_ref, cumb_ref, reverse=True)
    q_ref[...] = (proj[:, :GLA_DK] * (GLA_HEAD_K ** -0.5)).astype(BF16)
    k_ref[...] = proj[:, GLA_DK:2 * GLA_DK].astype(BF16)
    v_ref[...] = proj[:, 2 * GLA_DK:2 * GLA_DK + GLA_DV].astype(BF16)
    g_ref[...] = _silu(proj[:, 2 * GLA_DK + GLA_DV:]).astype(BF16)

    def put_o(rows, vc, o):
        of_ref[rows, vc] = o.astype(BF16)

    _gla_guarded_scan(q_ref, k_ref, v_ref, cumf_ref, s_ref, put_o, kf_s, cm_s, sc_s, reverse=False)

    @pl.when(n == pl.num_programs(1) - 1)
    def _():
        sfin_ref[...] = s_ref[...]


def _gla_in(x, sh, scl, gpre, w, w_lr, wg, bg, s0, tm):
    bsz, t, d = x.shape
    c = GLA_CHUNK
    row = lambda wd: pl.BlockSpec((None, tm, wd), lambda b, i: (b, i, 0))
    vec = lambda wd: pl.BlockSpec((None, 1, wd), lambda b, i: (b, 0, 0))
    full = lambda a: pl.BlockSpec(a.shape, lambda b, i: (0,) * a.ndim)
    st = pl.BlockSpec((None, GLA_HEADS, GLA_HEAD_V, GLA_HEAD_K), lambda b, i: (b, 0, 0, 0))
    return pl.pallas_call(
        _gla_in_kernel,
        grid=(bsz, t // tm),
        in_specs=[row(d), vec(d), vec(d), full(gpre), full(w), full(w_lr),
                  full(wg[0]), full(bg[0]), full(wg[1]), full(bg[1]), st],
        out_specs=[row(GLA_DK), row(GLA_DK), row(GLA_DV), row(GLA_DV), row(GLA_DK), row(GLA_DV), st],
        out_shape=[jax.ShapeDtypeStruct((bsz, t, GLA_DK), BF16),
                   jax.ShapeDtypeStruct((bsz, t, GLA_DK), BF16),
                   jax.ShapeDtypeStruct((bsz, t, GLA_DV), BF16),
                   jax.ShapeDtypeStruct((bsz, t, GLA_DV), BF16),
                   jax.ShapeDtypeStruct((bsz, t, GLA_DK), F32),
                   jax.ShapeDtypeStruct((bsz, t, GLA_DV), BF16),
                   jax.ShapeDtypeStruct(s0.shape, F32)],
        scratch_shapes=[pltpu.VMEM((GLA_HEADS, GLA_HEAD_V, GLA_HEAD_K), F32), pltpu.VMEM((tm, GLA_DK), F32),
                        pltpu.VMEM((c, GLA_HEAD_K), F32), pltpu.VMEM((c, GLA_HEAD_K), F32),
                        pltpu.VMEM((c, c), F32)],
        compiler_params=_cparams("parallel", "arbitrary"),
        name="gla_in_fwd",
    )(x, sh, scl, gpre, w, w_lr, wg[0], bg[0], wg[1], bg[1], s0)


def _gla_scores_exact(q, k, cum, reverse, kf_s, cm_s, sc_s):
    c = q.shape[0]
    kf_s[...] = k
    cm_s[...] = cum
    sc_s[...] = jnp.zeros_like(sc_s)
    ridx = lax.broadcasted_iota(jnp.int32, (c, 1), 0)
    cidx = lax.broadcasted_iota(jnp.int32, (c, c), 1)

    def column(j, carry):
        kj = kf_s[pl.ds(j, 1), :]
        cj = cm_s[pl.ds(j, 1), :]
        live = (ridx <= j) if reverse else (ridx >= j)
        dec = jnp.exp2(jnp.where(live, jnp.minimum(cum - cj, 0.0), -1e30))
        sj = jnp.sum(q * kj * dec, axis=-1, keepdims=True)
        sc_s[...] += jnp.where(cidx == j, sj, 0.0)
        return carry

    lax.fori_loop(0, c, column, 0)
    return sc_s[...]


def _gla_chunks(q_ref, k_ref, v_ref, cum_ref, s_ref, put_o, kf_s, cm_s, sc_s, *, reverse, factorised):
    tb = q_ref.shape[0]
    c = GLA_CHUNK
    row = lax.broadcasted_iota(jnp.int32, (c, c), 0)
    col = lax.broadcasted_iota(jnp.int32, (c, c), 1)
    keep = (row <= col) if reverse else (row >= col)
    chunks = range(tb // c)
    for i in (reversed(chunks) if reverse else chunks):
        rows = slice(i * c, (i + 1) * c)
        for hd in range(GLA_HEADS):
            kc = slice(hd * GLA_HEAD_K, (hd + 1) * GLA_HEAD_K)
            vc = slice(hd * GLA_HEAD_V, (hd + 1) * GLA_HEAD_V)
            cum = cum_ref[rows, kc]
            last = cum[0:1] if reverse else cum[c - 1:c]
            q = q_ref[rows, kc].astype(F32)
            k = k_ref[rows, kc].astype(F32)
            v = v_ref[rows, vc]
            qe = (q * jnp.exp2(cum)).astype(BF16)
            kl = (k * jnp.exp2(last - cum)).astype(BF16)
            st = s_ref[hd]
            if factorised:
                ke = (k * jnp.exp2(-cum)).astype(BF16)
                scores = lax.dot_general(qe, ke, (((1,), (1,)), ((), ())), preferred_element_type=F32)
                scores = jnp.where(keep, scores, 0.0)
            else:
                scores = _gla_scores_exact(q, k, cum, reverse, kf_s, cm_s, sc_s)
            o = lax.dot_general(qe, st.astype(BF16), (((1,), (1,)), ((), ())),
                                preferred_element_type=F32)
            o = o + _dot(scores.astype(BF16), v)
            put_o(rows, vc, o)
            upd = lax.dot_general(v, kl, (((0,), (0,)), ((), ())), preferred_element_type=F32)
            s_ref[hd] = st * jnp.exp2(last) + upd


def _gla_guarded_scan(q_ref, k_ref, v_ref, cum_ref, s_ref, put_o, kf_s, cm_s, sc_s, *, reverse):
    c = GLA_CHUNK
    ends = [cum_ref[i * c:i * c + 1, :] if reverse else cum_ref[(i + 1) * c - 1:(i + 1) * c, :]
            for i in range(q_ref.shape[0] // c)]
    safe = jnp.min(functools.reduce(jnp.minimum, ends)) >= GLA_FACTORISED_MIN_CUM
    scan = functools.partial(_gla_chunks, q_ref, k_ref, v_ref, cum_ref, s_ref, put_o, kf_s, cm_s, sc_s,
                             reverse=reverse)
    pl.when(safe)(functools.partial(scan, factorised=True))
    pl.when(jnp.logical_not(safe))(functools.partial(scan, factorised=False))


def _gla_bwd_out_kernel(q_ref, k_ref, v_ref, cum_ref, s0_ref, of_ref, g_ref, x_ref, gt_ref, gn_ref, gpost_ref,
                        wout_ref, y_ref, sfin_ref, s_ref, kf_s, cm_s, sc_s, o_s):
    n = pl.program_id(1)

    @pl.when(n == 0)
    def _():
        s_ref[...] = s0_ref[...]

    def put_o(rows, vc, o):
        o_s[rows, vc] = o + of_ref[rows, vc].astype(F32)

    _gla_guarded_scan(q_ref, k_ref, v_ref, cum_ref, s_ref, put_o, kf_s, cm_s, sc_s, reverse=True)

    gn = gn_ref[...]
    parts = []
    for hd in range(GLA_HEADS):
        oh = o_s[:, hd * GLA_HEAD_V:(hd + 1) * GLA_HEAD_V]
        ms = jnp.mean(oh * oh, axis=-1, keepdims=True)
        parts.append(oh * lax.rsqrt(ms + RMS_EPS) * gn)
    on = jnp.concatenate(parts, axis=-1) * g_ref[...].astype(F32)
    y = _dot(on.astype(BF16), wout_ref[...])
    y_ref[...] = _post_norm_residual(x_ref[...], y, gpost_ref[...], gt_ref[...])

    @pl.when(n == pl.num_programs(1) - 1)
    def _():
        sfin_ref[...] = s_ref[...]


def _gla_bwd_out(q, k, v, cum, s0, o_f, g, x, gt, gn, gpost, w_out, tb):
    bsz, t, d = x.shape
    nblk = t // tb
    c = GLA_CHUNK
    row = lambda wd: pl.BlockSpec((None, tb, wd), lambda b, n: (b, nblk - 1 - n, 0))
    vec = lambda wd: pl.BlockSpec((None, 1, wd), lambda b, n: (b, 0, 0))
    full = lambda a: pl.BlockSpec(a.shape, lambda b, n: (0,) * a.ndim)
    st = pl.BlockSpec((None, GLA_HEADS, GLA_HEAD_V, GLA_HEAD_K), lambda b, n: (b, 0, 0, 0))
    return pl.pallas_call(
        _gla_bwd_out_kernel,
        grid=(bsz, nblk),
        in_specs=[row(GLA_DK), row(GLA_DK), row(GLA_DV), row(GLA_DK), st,
                  row(GLA_DV), row(GLA_DV), row(d), vec(d), full(gn), full(gpost), full(w_out)],
        out_specs=[row(d), st],
        out_shape=[jax.ShapeDtypeStruct((bsz, t, d), F32), jax.ShapeDtypeStruct(s0.shape, F32)],
        scratch_shapes=[pltpu.VMEM((GLA_HEADS, GLA_HEAD_V, GLA_HEAD_K), F32),
                        pltpu.VMEM((c, GLA_HEAD_K), F32), pltpu.VMEM((c, GLA_HEAD_K), F32), pltpu.VMEM((c, c), F32),
                        pltpu.VMEM((tb, GLA_DV), F32)],
        compiler_params=_cparams("parallel", "arbitrary"),
        name="gla_bwd_out",
    )(q, k, v, cum, s0, o_f, g, x, gt, gn, gpost, w_out)


def _gla_layer(x, sh, scl, gt, gpre, gpost, w_in, w_lr, wg, bg, gn, w_out, s0_f, s0_b):
    t = x.shape[1]
    q, k, v, g, cum_b, o_f, s_f = _gla_in(x, sh, scl, gpre, w_in, w_lr, wg, bg, s0_f, min(GLA_IN_ROWS, t))
    x_new, s_b = _gla_bwd_out(q, k, v, cum_b, s0_b, o_f, g, x, gt, gn, gpost, w_out, min(GLA_SCAN_ROWS, t))
    return x_new, s_f, s_b


def _lru_scan_kernel(x_ref, xp_ref, xn_ref, sh_ref, scl_ref, gpre_ref, w_ref, cw_ref, cb_ref,
                     waf_ref, baf_ref, wxf_ref, bxf_ref, lamf_ref,
                     wab_ref, bab_ref, wxb_ref, bxb_ref, lamb_ref, h0_ref,
                     hsum_ref, pb_ref, g_ref, hbfin_ref, pbfin_ref, ffin_ref,
                     zext_s, hf_s, pf_s, hb_s, pbk_s, carry_s, cin_s, *pre_s):
    g = pl.program_id(1)
    ng = pl.num_programs(1)
    r, s, d = x_ref.shape
    rows = r * s
    w = LRU_WIDTH
    halo_l = CONV_LEFT * s
    n_right = CONV_W - 1 - CONV_LEFT

    @pl.when(g == 0)
    def _():
        cin_s[...] = h0_ref[...]

    sh, scl, gpre = sh_ref[...], scl_ref[...], gpre_ref[...]
    hb16 = _pre_norm_modulate(x_ref[...].reshape(rows, d), gpre, sh, scl).astype(BF16)
    xh = jnp.concatenate([xp_ref[...].reshape(halo_l, d), xn_ref[...].reshape(n_right * s, d)], axis=0)
    zh = _dot(_pre_norm_modulate(xh, gpre, sh, scl).astype(BF16), w_ref[:, :w])
    sub = lax.broadcasted_iota(jnp.int32, (s, LRU_BLOCK_W), 0)

    def in_proj(nb):
        cs = slice(nb * LRU_BLOCK_W, (nb + 1) * LRU_BLOCK_W)
        zext_s[halo_l:halo_l + rows, cs] = _dot(hb16, w_ref[:, cs])
        g_ref[:, cs] = _dot(hb16, w_ref[:, w + nb * LRU_BLOCK_W:w + (nb + 1) * LRU_BLOCK_W]).astype(BF16)
        for j in range(CONV_LEFT):
            src = halo_l + (r - CONV_LEFT + j) * s
            inner = pltpu.roll(zext_s[src:src + s, cs], 1, 0)
            edge = jnp.where(g == 0, 0.0, pltpu.roll(zh[j * s:(j + 1) * s, cs], 1, 0))
            zext_s[j * s:(j + 1) * s, cs] = jnp.where(sub == 0, edge, inner)
        for j in range(n_right):
            src = halo_l + j * s
            inner = pltpu.roll(zext_s[src:src + s, cs], s - 1, 0)
            edge = jnp.where(g == ng - 1, 0.0,
                             pltpu.roll(zh[halo_l + j * s:halo_l + (j + 1) * s, cs], s - 1, 0))
            dst = halo_l + rows + j * s
            zext_s[dst:dst + s, cs] = jnp.where(sub == s - 1, edge, inner)

    def gate_matmuls(nb):
        cs = slice(nb * LRU_BLOCK_W, (nb + 1) * LRU_BLOCK_W)
        hz = cb_ref[:, cs]
        for j in range(CONV_W):
            hz = hz + cw_ref[j:j + 1, cs] * zext_s[j * s:j * s + rows, cs]
        hzb = hz.astype(BF16)
        prf_s, pif_s, prb_s, pib_s, hzc_s = pre_s[5 * (nb % 2):5 * (nb % 2) + 5]
        prf_s[...] = _dot(hzb, waf_ref[nb]) + baf_ref[:, cs]
        pif_s[...] = _dot(hzb, wxf_ref[nb]) + bxf_ref[:, cs]
        prb_s[...] = _dot(hzb, wab_ref[nb]) + bab_ref[:, cs]
        pib_s[...] = _dot(hzb, wxb_ref[nb]) + bxb_ref[:, cs]
        hzc_s[...] = hz

    in_proj(0)
    gate_matmuls(0)
    for nb in range(LRU_BLOCKS):
        cs = slice(nb * LRU_BLOCK_W, (nb + 1) * LRU_BLOCK_W)
        if nb + 1 < LRU_BLOCKS:
            in_proj(nb + 1)
            gate_matmuls(nb + 1)
        prf_s, pif_s, prb_s, pib_s, hzc_s = pre_s[5 * (nb % 2):5 * (nb % 2) + 5]
        c1f = jnp.broadcast_to((-0.5 * LRU_C * LOG2_E) * _softplus(-lamf_ref[:, cs]), (s, LRU_BLOCK_W))
        c1b = jnp.broadcast_to((-0.5 * LRU_C * LOG2_E) * _softplus(-lamb_ref[:, cs]), (s, LRU_BLOCK_W))

        def gate(pr_s, pi_s, c1, rws):
            tr = jnp.tanh(pr_s[rws, :])
            ti = jnp.tanh(pi_s[rws, :])
            hz = hzc_s[rws, :]
            a = jnp.exp2(c1 * tr + c1)
            return a, _sqrt(1.0 - a * a) * (hz * ti + hz)

        hf = hb = jnp.zeros((s, LRU_BLOCK_W), F32)
        pf = pb = jnp.ones((s, LRU_BLOCK_W), F32)
        for t in range(r):
            rf = slice(t * s, (t + 1) * s)
            rb = slice((r - 1 - t) * s, (r - t) * s)
            af, uf = gate(prf_s, pif_s, c1f, rf)
            ab, ub = gate(prb_s, pib_s, c1b, rb)
            hf = af * hf + uf
            pf = af * pf
            hb = ab * hb + ub
            pb = ab * pb
            hf_s[rf, :] = hf
            pf_s[rf, :] = pf
            hb_s[rb, :] = hb
            pbk_s[rb, :] = pb

        cin = cin_s[:, cs]
        for c in range(s):
            carry_s[c:c + 1, :] = cin
            cin = hf[c:c + 1] + pf[c:c + 1] * cin
        cin_s[:, cs] = cin
        carry = carry_s[...][None]
        piece = min(r, 16)
        for i in range(r // piece):
            rws = slice(i * piece * s, (i + 1) * piece * s)
            shape3 = (piece, s, LRU_BLOCK_W)
            hsum = (hf_s[rws, :].reshape(shape3) + pf_s[rws, :].reshape(shape3) * carry
                    + hb_s[rws, :].reshape(shape3))
            hsum_ref[rws, cs] = hsum.reshape(piece * s, LRU_BLOCK_W).astype(BF16)
            pb_ref[rws, cs] = pbk_s[rws, :].astype(BF16)
        hbfin_ref[:, cs] = hb
        pbfin_ref[:, cs] = pb

    ffin_ref[...] = cin_s[...]


def _lru_scan(x5, sh, scl, gpre, w_in, conv_w, conv_b, p_f, p_b, h0_f):
    bsz, r, ng, s, d = x5.shape
    rows = r * s
    w = LRU_WIDTH
    n_right = CONV_W - 1 - CONV_LEFT
    once = pl.Buffered(1)
    cur = pl.BlockSpec((None, r, None, s, d), lambda b, g: (b, 0, g, 0, 0))
    prev = pl.BlockSpec((None, CONV_LEFT, None, s, d),
                        lambda b, g: (b, r // CONV_LEFT - 1, jnp.maximum(g - 1, 0), 0, 0))
    nxt = pl.BlockSpec((None, n_right, None, s, d), lambda b, g: (b, 0, jnp.minimum(g + 1, ng - 1), 0, 0))
    vec = lambda wd: pl.BlockSpec((None, 1, wd), lambda b, g: (b, 0, 0))
    full = lambda a: pl.BlockSpec(a.shape, lambda b, g: (0,) * a.ndim, pipeline_mode=once)
    row = pl.BlockSpec((None, rows, w), lambda b, g: (b, g, 0))
    fin = pl.BlockSpec((None, None, s, w), lambda b, g: (b, g, 0, 0))
    blk = lambda: pltpu.VMEM((rows, LRU_BLOCK_W), F32)
    return pl.pallas_call(
        _lru_scan_kernel,
        grid=(bsz, ng),
        in_specs=[cur, prev, nxt, vec(d), vec(d), full(gpre), full(w_in), full(conv_w), full(conv_b)]
                 + [full(a) for a in p_f] + [full(a) for a in p_b] + [vec(w)],
        out_specs=[row, row, row, fin, fin, vec(w)],
        out_shape=[jax.ShapeDtypeStruct((bsz, ng * rows, w), BF16),
                   jax.ShapeDtypeStruct((bsz, ng * rows, w), BF16),
                   jax.ShapeDtypeStruct((bsz, ng * rows, w), BF16),
                   jax.ShapeDtypeStruct((bsz, ng, s, w), F32),
                   jax.ShapeDtypeStruct((bsz, ng, s, w), F32),
                   jax.ShapeDtypeStruct((bsz, 1, w), F32)],
        scratch_shapes=[pltpu.VMEM((rows + (CONV_W - 1) * s, w), F32)] + [blk() for _ in range(4)]
                       + [pltpu.VMEM((s, LRU_BLOCK_W), F32), pltpu.VMEM((1, w), F32)]
                       + [blk() for _ in range(10)],
        compiler_params=_cparams("parallel", "arbitrary"),
        name="lru_scan",
    )(x5, x5, x5, sh, scl, gpre, w_in, conv_w, conv_b, *p_f, *p_b, h0_f)


def _lru_carry_kernel(h0_ref, hfin_ref, pfin_ref, carry_ref, final_ref):
    c = h0_ref[...]
    for k in range(hfin_ref.shape[0] - 1, -1, -1):
        carry_ref[k:k + 1, :] = c
        c = hfin_ref[k:k + 1, :] + pfin_ref[k:k + 1, :] * c
    final_ref[...] = c


def _lru_carry(h0, hfin, pfin):
    bsz, ng, s, w = hfin.shape
    vec = pl.BlockSpec((None, 1, w), lambda b: (b, 0, 0))
    runs = pl.BlockSpec((None, ng * s, w), lambda b: (b, 0, 0))
    carry, final = pl.pallas_call(
        _lru_carry_kernel,
        grid=(bsz,),
        in_specs=[vec, runs, runs],
        out_specs=[runs, vec],
        out_shape=[jax.ShapeDtypeStruct((bsz, ng * s, w), F32), jax.ShapeDtypeStruct((bsz, 1, w), F32)],
        compiler_params=_cparams("parallel"),
        name="lru_carry",
    )(h0, hfin.reshape(bsz, ng * s, w), pfin.reshape(bsz, ng * s, w))
    return carry.reshape(bsz, ng, s, w), final


def _lru_out_kernel(hsum_ref, pb_ref, cb_ref, g_ref, x_ref, gt_ref, gpost_ref, w_ref, o_ref):
    r, s, d = x_ref.shape
    w = hsum_ref.shape[-1]
    h = hsum_ref[...].astype(F32).reshape(r, s, w) + pb_ref[...].astype(F32).reshape(r, s, w) * cb_ref[...][None]
    y = _dot((h.reshape(r * s, w) * _silu(g_ref[...].astype(F32))).astype(BF16), w_ref[...])
    x = x_ref[...].reshape(r * s, d)
    o_ref[...] = _post_norm_residual(x, y, gpost_ref[...], gt_ref[...]).reshape(r, s, d)


def _lru_out(hsum, pb, cb, g, x5, gt, gpost, w, rb):
    bsz, r, ng, s, d = x5.shape
    nr = r // rb
    tm = rb * s
    width = hsum.shape[-1]
    xio = pl.BlockSpec((None, rb, None, s, d), lambda b, g, i: (b, i, g, 0, 0))
    row = pl.BlockSpec((None, tm, width), lambda b, g, i: (b, g * nr + i, 0))
    runs = pl.BlockSpec((None, None, s, width), lambda b, g, i: (b, g, 0, 0))
    vec = pl.BlockSpec((None, 1, d), lambda b, g, i: (b, 0, 0))
    full = lambda a: pl.BlockSpec(a.shape, lambda b, g, i: (0,) * a.ndim)
    return pl.pallas_call(
        _lru_out_kernel,
        grid=(bsz, ng, nr),
        in_specs=[row, row, runs, row, xio, vec, full(gpost), full(w)],
        out_specs=xio,
        out_shape=jax.ShapeDtypeStruct(x5.shape, F32),
        compiler_params=_cparams("parallel", "parallel", "parallel"),
        name="lru_out",
    )(hsum, pb, cb, g, x5, gt, gpost, w)


def kernel(x, c, ctx, c_ctx, ada_w, ada_b, norm_pre, norm_post, gla_w_in, gla_wg_f, gla_bg_f, gla_wg_b, gla_bg_b, gla_norm, gla_w_out, lru_w_in, lru_conv_w, lru_conv_b, lru_wa_f, lru_ba_f, lru_wx_f, lru_bx_f, lru_lam_f, lru_wa_b, lru_ba_b, lru_wx_b, lru_bx_b, lru_lam_b, lru_w_out):
    bsz, seq, d = x.shape
    ctx_len = ctx.shape[1]
    rows = seq // GRID_W

    cvec = jnp.concatenate([c, c_ctx[None], jnp.zeros((SUBLANES - bsz - 1, d), F32)], axis=0)
    mod = _ada_modulation(cvec, ada_w, ada_b)

    def mods(i):
        lat = [mod[i, :bsz, None, j * d:(j + 1) * d] for j in range(3)]
        con = [jnp.broadcast_to(mod[i, bsz, None, None, j * d:(j + 1) * d], (bsz, 1, d)) for j in range(3)]
        return lat, con

    (sh, scl, gt), (sh_c, scl_c, gt_c) = mods(0)
    gpre, gpost = norm_pre[0][None], norm_post[0][None]
    rk = GLA_GATE_RANK
    n_main = 2 * GLA_DK + 2 * GLA_DV
    w_in = gla_w_in[0][:, :n_main].astype(BF16)
    w_lr = jnp.pad(gla_w_in[0][:, n_main:], ((0, 0), (0, LANES - 2 * rk))).astype(BF16)
    wg = (jnp.pad(gla_wg_f[0], ((0, LANES - rk), (0, 0))).astype(BF16),
          jnp.pad(gla_wg_b[0], ((rk, LANES - 2 * rk), (0, 0))).astype(BF16))
    bg = (gla_bg_f[0][None], gla_bg_b[0][None])
    gla_args = (gpre, gpost, w_in, w_lr, wg, bg, gla_norm[0][None], gla_w_out[0].astype(BF16))
    s0 = jnp.zeros((bsz, GLA_HEADS, GLA_HEAD_V, GLA_HEAD_K), F32)
    ctx, s_f, s_b = _gla_layer(ctx, sh_c, scl_c, gt_c, *gla_args, s0, s0)
    x, _, _ = _gla_layer(x, sh, scl, gt, *gla_args, s_f, s_b)

    (sh, scl, gt), (sh_c, scl_c, _) = mods(1)
    gpre, gpost = norm_pre[1][None], norm_post[1][None]
    vec = lambda a: a[None]
    gate = lambda wgt, bias: (wgt.astype(BF16), vec(0.5 * bias))
    p_f = (*gate(lru_wa_f[0], lru_ba_f[0]), *gate(lru_wx_f[0], lru_bx_f[0]), vec(lru_lam_f[0]))
    p_b = (*gate(lru_wa_b[0], lru_ba_b[0]), *gate(lru_wx_b[0], lru_bx_b[0]), vec(lru_lam_b[0]))
    scan_args = (gpre, lru_w_in[0].astype(BF16), 0.5 * lru_conv_w[0], vec(0.5 * lru_conv_b[0]), p_f, p_b)
    h0 = jnp.zeros((bsz, 1, LRU_WIDTH), F32)
    cstep = ctx_len // N_SEG
    ctx5 = ctx.reshape(bsz, N_SEG, cstep, d).transpose(0, 2, 1, 3).reshape(bsz, cstep, 1, N_SEG, d)
    _, _, _, hbfin, pbfin, s_f = _lru_scan(ctx5, sh_c, scl_c, *scan_args, h0)
    _, s_b = _lru_carry(h0, hbfin, pbfin)
    x5 = x.reshape(bsz, rows, GRID_W // N_SEG, N_SEG, d)
    hsum, pb, g, hbfin, pbfin, _ = _lru_scan(x5, sh, scl, *scan_args, s_f)
    cb, _ = _lru_carry(s_b, hbfin, pbfin)
    out5 = _lru_out(hsum, pb, cb, g, x5, gt, gpost, lru_w_out[0].astype(BF16), rb=min(LRU_OUT_STEPS, rows))
    return out5.reshape(bsz, seq, d)
```

```python
import functools

import jax
import jax.numpy as jnp
from jax import lax
from jax.experimental import pallas as pl
from jax.experimental.pallas import tpu as pltpu

F32 = jnp.float32
BF16 = jnp.bfloat16
SUBLANES, LANES = 8, 128

RMS_EPS = 1e-6
LOG2_E = 1.4426950408889634
GRID_W = 64
GLA_HEADS = 4
GLA_HEAD_K = 128
GLA_HEAD_V = 256
GLA_DK = GLA_HEADS * GLA_HEAD_K
GLA_DV = GLA_HEADS * GLA_HEAD_V
GLA_GATE_RANK = 16
GLA_GATE_NORM = 16.0
GLA_CHUNK = 256
GLA_FACTORISED_MIN_CUM = -86.0
GLA_SUM_PIECE = 128
GLA_IN_ROWS = 1024
GLA_SCAN_ROWS = 1024
LRU_BLOCKS = 5
LRU_BLOCK_W = 256
LRU_WIDTH = LRU_BLOCKS * LRU_BLOCK_W
LRU_C = 8.0
CONV_W = 4
CONV_LEFT = CONV_W // 2
N_SEG = SUBLANES
LRU_OUT_STEPS = 128

VMEM_LIMIT_BYTES = 56 * 1024 * 1024


def _cparams(*sem):
    return pltpu.CompilerParams(dimension_semantics=sem, vmem_limit_bytes=VMEM_LIMIT_BYTES)


def _silu(x):
    hx = 0.5 * x
    return hx * jnp.tanh(hx) + hx


def _sqrt(x):
    return x * lax.rsqrt(jnp.maximum(x, 1e-30))


def _log_sigmoid(x):
    return jnp.minimum(x, 0.0) - jnp.log(1.0 + jnp.exp(-jnp.abs(x)))


def _softplus(x):
    return jnp.maximum(x, 0.0) + jnp.log1p(jnp.exp(-jnp.abs(x)))


def _split_bf16(x):
    hi = x.astype(BF16)
    lo = (x - hi.astype(F32)).astype(BF16)
    return hi, lo


def _dot(a, b):
    return jnp.dot(a, b, preferred_element_type=F32)


def _dot_x3(a, b):
    a_hi, a_lo = _split_bf16(a)
    b_hi, b_lo = _split_bf16(b)
    return _dot(a_hi, b_hi) + (_dot(a_hi, b_lo) + _dot(a_lo, b_hi))


def _pre_norm_modulate(x, gpre, sh, scl):
    ms = jnp.mean(x * x, axis=-1, keepdims=True)
    return (x * lax.rsqrt(ms + RMS_EPS)) * (gpre * (1.0 + scl)) + sh


def _post_norm_residual(x, y, gpost, gt):
    ms = jnp.mean(y * y, axis=-1, keepdims=True)
    return x + (y * lax.rsqrt(ms + RMS_EPS)) * (gt * gpost)


def _ada_kernel(c_ref, w_ref, b_ref, o_ref):
    sc = _silu(c_ref[...])
    o_ref[...] = _dot_x3(sc, w_ref[...]) + b_ref[...]


def _ada_modulation(cvec, ada_w, ada_b):
    depth, d, n3 = ada_w.shape
    tn = 1024
    return pl.pallas_call(
        _ada_kernel,
        grid=(depth, n3 // tn),
        in_specs=[pl.BlockSpec((SUBLANES, d), lambda i, j: (0, 0)),
                  pl.BlockSpec((None, d, tn), lambda i, j: (i, 0, j)),
                  pl.BlockSpec((None, 1, tn), lambda i, j: (i, 0, j))],
        out_specs=pl.BlockSpec((None, SUBLANES, tn), lambda i, j: (i, 0, j)),
        out_shape=jax.ShapeDtypeStruct((depth, SUBLANES, n3), F32),
        compiler_params=_cparams("parallel", "parallel"),
        name="ada_modulation",
    )(cvec, ada_w, ada_b.reshape(depth, 1, n3))


def _gla_log_decay_sums(lr, wg_ref, bg_ref, cum_ref, reverse):
    c, p = GLA_CHUNK, GLA_SUM_PIECE
    row = lax.broadcasted_iota(jnp.int32, (p, 2 * p), 0)
    col = lax.broadcasted_iota(jnp.int32, (p, 2 * p), 1) % p
    tri = ((row <= col) if reverse else (row >= col)).astype(BF16)
    z = _dot(lr, wg_ref[...]) + bg_ref[...]
    log_a = _log_sigmoid(z) * (LOG2_E / GLA_GATE_NORM)
    hi, lo = _split_bf16(log_a)
    for i in range(lr.shape[0] // c):
        carry = None
        pieces = range(c // p)
        for j in (reversed(pieces) if reverse else pieces):
            rows = slice(i * c + j * p, i * c + (j + 1) * p)
            s = _dot(tri, jnp.concatenate([hi[rows], lo[rows]], axis=0))
            if carry is not None:
                s = s + carry
            cum_ref[rows, :] = s
            carry = s[0:1] if reverse else s[p - 1:p]


def _gla_in_kernel(x_ref, sh_ref, scl_ref, gpre_ref, w_ref, wlr_ref, wgf_ref, bgf_ref, wgb_ref, bgb_ref, s0_ref,
                   q_ref, k_ref, v_ref, g_ref, cumb_ref, of_ref, sfin_ref,
                   s_ref, cumf_ref, kf_s, cm_s, sc_s):
    n = pl.program_id(1)

    @pl.when(n == 0)
    def _():
        s_ref[...] = s0_ref[...]

    hb16 = _pre_norm_modulate(x_ref[...], gpre_ref[...], sh_ref[...], scl_ref[...]).astype(BF16)
    lr = _dot(hb16, wlr_ref[...]).astype(BF16)
    proj = _dot(hb16, w_ref[...])
    _gla_log_decay_sums(lr, wgf_ref, bgf_ref, cumf_ref, reverse=False)
    _gla_log_decay_sums(lr, wgb_ref, bgb_ref, cumb_ref, reverse=True)
    q_ref[...] = (proj[:, :GLA_DK] * (GLA_HEAD_K ** -0.5)).astype(BF16)
    k_ref[...] = proj[:, GLA_DK:2 * GLA_DK].astype(BF16)
    v_ref[...] = proj[:, 2 * GLA_DK:2 * GLA_DK + GLA_DV].astype(BF16)
    g_ref[...] = _silu(proj[:, 2 * GLA_DK + GLA_DV:]).astype(BF16)

    def put_o(rows, vc, o):
        of_ref[rows, vc] = o.astype(BF16)

    _gla_guarded_scan(q_ref, k_ref, v_ref, cumf_ref, s_ref, put_o, kf_s, cm_s, sc_s, reverse=False)

    @pl.when(n == pl.num_programs(1) - 1)
    def _():
        sfin_ref[...] = s_ref[...]


def _gla_in(x, sh, scl, gpre, w, w_lr, wg, bg, s0, tm):
    bsz, t, d = x.shape
    c = GLA_CHUNK
    row = lambda wd: pl.BlockSpec((None, tm, wd), lambda b, i: (b, i, 0))
    vec = lambda wd: pl.BlockSpec((None, 1, wd), lambda b, i: (b, 0, 0))
    full = lambda a: pl.BlockSpec(a.shape, lambda b, i: (0,) * a.ndim)
    st = pl.BlockSpec((None, GLA_HEADS, GLA_HEAD_V, GLA_HEAD_K), lambda b, i: (b, 0, 0, 0))
    return pl.pallas_call(
        _gla_in_kernel,
        grid=(bsz, t // tm),
        in_specs=[row(d), vec(d), vec(d), full(gpre), full(w), full(w_lr),
                  full(wg[0]), full(bg[0]), full(wg[1]), full(bg[1]), st],
        out_specs=[row(GLA_DK), row(GLA_DK), row(GLA_DV), row(GLA_DV), row(GLA_DK), row(GLA_DV), st],
        out_shape=[jax.ShapeDtypeStruct((bsz, t, GLA_DK), BF16),
                   jax.ShapeDtypeStruct((bsz, t, GLA_DK), BF16),
                   jax.ShapeDtypeStruct((bsz, t, GLA_DV), BF16),
                   jax.ShapeDtypeStruct((bsz, t, GLA_DV), BF16),
                   jax.ShapeDtypeStruct((bsz, t, GLA_DK), F32),
                   jax.ShapeDtypeStruct((bsz, t, GLA_DV), BF16),
                   jax.ShapeDtypeStruct(s0.shape, F32)],
        scratch_shapes=[pltpu.VMEM((GLA_HEADS, GLA_HEAD_V, GLA_HEAD_K), F32), pltpu.VMEM((tm, GLA_DK), F32),
                        pltpu.VMEM((c, GLA_HEAD_K), F32), pltpu.VMEM((c, GLA_HEAD_K), F32),
                        pltpu.VMEM((c, c), F32)],
        compiler_params=_cparams("parallel", "arbitrary"),
        name="gla_in_fwd",
    )(x, sh, scl, gpre, w, w_lr, wg[0], bg[0], wg[1], bg[1], s0)


def _gla_scores_exact(q, k, cum, reverse, kf_s, cm_s, sc_s):
    c = q.shape[0]
    kf_s[...] = k
    cm_s[...] = cum
    sc_s[...] = jnp.zeros_like(sc_s)
    ridx = lax.broadcasted_iota(jnp.int32, (c, 1), 0)
    cidx = lax.broadcasted_iota(jnp.int32, (c, c), 1)

    def column(j, carry):
        kj = kf_s[pl.ds(j, 1), :]
        cj = cm_s[pl.ds(j, 1), :]
        live = (ridx <= j) if reverse else (ridx >= j)
        dec = jnp.exp2(jnp.where(live, jnp.minimum(cum - cj, 0.0), -1e30))
        sj = jnp.sum(q * kj * dec, axis=-1, keepdims=True)
        sc_s[...] += jnp.where(cidx == j, sj, 0.0)
        return carry

    lax.fori_loop(0, c, column, 0)
    return sc_s[...]


def _gla_chunks(q_ref, k_ref, v_ref, cum_ref, s_ref, put_o, kf_s, cm_s, sc_s, *, reverse, factorised):
    tb = q_ref.shape[0]
    c = GLA_CHUNK
    row = lax.broadcasted_iota(jnp.int32, (c, c), 0)
    col = lax.broadcasted_iota(jnp.int32, (c, c), 1)
    keep = (row <= col) if reverse else (row >= col)
    chunks = range(tb // c)
    for i in (reversed(chunks) if reverse else chunks):
        rows = slice(i * c, (i + 1) * c)
        for hd in range(GLA_HEADS):
            kc = slice(hd * GLA_HEAD_K, (hd + 1) * GLA_HEAD_K)
            vc = slice(hd * GLA_HEAD_V, (hd + 1) * GLA_HEAD_V)
            cum = cum_ref[rows, kc]
            last = cum[0:1] if reverse else cum[c - 1:c]
            q = q_ref[rows, kc].astype(F32)
            k = k_ref[rows, kc].astype(F32)
            v = v_ref[rows, vc]
            qe = (q * jnp.exp2(cum)).astype(BF16)
            kl = (k * jnp.exp2(last - cum)).astype(BF16)
            st = s_ref[hd]
            if factorised:
                ke = (k * jnp.exp2(-cum)).astype(BF16)
                scores = lax.dot_general(qe, ke, (((1,), (1,)), ((), ())), preferred_element_type=F32)
                scores = jnp.where(keep, scores, 0.0)
            else:
                scores = _gla_scores_exact(q, k, cum, reverse, kf_s, cm_s, sc_s)
            o = lax.dot_general(qe, st.astype(BF16), (((1,), (1,)), ((), ())),
                                preferred_element_type=F32)
            o = o + _dot(scores.astype(BF16), v)
            put_o(rows, vc, o)
            upd = lax.dot_general(v, kl, (((0,), (0,)), ((), ())), preferred_element_type=F32)
            s_ref[hd] = st * jnp.exp2(last) + upd


def _gla_guarded_scan(q_ref, k_ref, v_ref, cum_ref, s_ref, put_o, kf_s, cm_s, sc_s, *, reverse):
    c = GLA_CHUNK
    ends = [cum_ref[i * c:i * c + 1, :] if reverse else cum_ref[(i + 1) * c - 1:(i + 1) * c, :]
            for i in range(q_ref.shape[0] // c)]
    safe = jnp.min(functools.reduce(jnp.minimum, ends)) >= GLA_FACTORISED_MIN_CUM
    scan = functools.partial(_gla_chunks, q_ref, k_ref, v_ref, cum_ref, s_ref, put_o, kf_s, cm_s, sc_s,
                             reverse=reverse)
    pl.when(safe)(functools.partial(scan, factorised=True))
    pl.when(jnp.logical_not(safe))(functools.partial(scan, factorised=False))


def _gla_bwd_out_kernel(q_ref, k_ref, v_ref, cum_ref, s0_ref, of_ref, g_ref, x_ref, gt_ref, gn_ref, gpost_ref,
                        wout_ref, y_ref, sfin_ref, s_ref, kf_s, cm_s, sc_s, o_s):
    n = pl.program_id(1)

    @pl.when(n == 0)
    def _():
        s_ref[...] = s0_ref[...]

    def put_o(rows, vc, o):
        o_s[rows, vc] = o + of_ref[rows, vc].astype(F32)

    _gla_guarded_scan(q_ref, k_ref, v_ref, cum_ref, s_ref, put_o, kf_s, cm_s, sc_s, reverse=True)

    gn = gn_ref[...]
    parts = []
    for hd in range(GLA_HEADS):
        oh = o_s[:, hd * GLA_HEAD_V:(hd + 1) * GLA_HEAD_V]
        ms = jnp.mean(oh * oh, axis=-1, keepdims=True)
        parts.append(oh * lax.rsqrt(ms + RMS_EPS) * gn)
    on = jnp.concatenate(parts, axis=-1) * g_ref[...].astype(F32)
    y = _dot(on.astype(BF16), wout_ref[...])
    y_ref[...] = _post_norm_residual(x_ref[...], y, gpost_ref[...], gt_ref[...])

    @pl.when(n == pl.num_programs(1) - 1)
    def _():
        sfin_ref[...] = s_ref[...]


def _gla_bwd_out(q, k, v, cum, s0, o_f, g, x, gt, gn, gpost, w_out, tb):
    bsz, t, d = x.shape
    nblk = t // tb
    c = GLA_CHUNK
    row = lambda wd: pl.BlockSpec((None, tb, wd), lambda b, n: (b, nblk - 1 - n, 0))
    vec = lambda wd: pl.BlockSpec((None, 1, wd), lambda b, n: (b, 0, 0))
    full = lambda a: pl.BlockSpec(a.shape, lambda b, n: (0,) * a.ndim)
    st = pl.BlockSpec((None, GLA_HEADS, GLA_HEAD_V, GLA_HEAD_K), lambda b, n: (b, 0, 0, 0))
    return pl.pallas_call(
        _gla_bwd_out_kernel,
        grid=(bsz, nblk),
        in_specs=[row(GLA_DK), row(GLA_DK), row(GLA_DV), row(GLA_DK), st,
                  row(GLA_DV), row(GLA_DV), row(d), vec(d), full(gn), full(gpost), full(w_out)],
        out_specs=[row(d), st],
        out_shape=[jax.ShapeDtypeStruct((bsz, t, d), F32), jax.ShapeDtypeStruct(s0.shape, F32)],
        scratch_shapes=[pltpu.VMEM((GLA_HEADS, GLA_HEAD_V, GLA_HEAD_K), F32),
                        pltpu.VMEM((c, GLA_HEAD_K), F32), pltpu.VMEM((c, GLA_HEAD_K), F32), pltpu.VMEM((c, c), F32),
                        pltpu.VMEM((tb, GLA_DV), F32)],
        compiler_params=_cparams("parallel", "arbitrary"),
        name="gla_bwd_out",
    )(q, k, v, cum, s0, o_f, g, x, gt, gn, gpost, w_out)


def _gla_layer(x, sh, scl, gt, gpre, gpost, w_in, w_lr, wg, bg, gn, w_out, s0_f, s0_b):
    t = x.shape[1]
    tm, tb = min(GLA_IN_ROWS, t), min(GLA_SCAN_ROWS, t)
    assert t % tm == 0 and t % tb == 0 and tm % GLA_CHUNK == 0 and tb % GLA_CHUNK == 0, (t, tm, tb)
    q, k, v, g, cum_b, o_f, s_f = _gla_in(x, sh, scl, gpre, w_in, w_lr, wg, bg, s0_f, tm)
    x_new, s_b = _gla_bwd_out(q, k, v, cum_b, s0_b, o_f, g, x, gt, gn, gpost, w_out, tb)
    return x_new, s_f, s_b


def _lru_scan_kernel(x_ref, xp_ref, xn_ref, sh_ref, scl_ref, gpre_ref, w_ref, cw_ref, cb_ref,
                     waf_ref, baf_ref, wxf_ref, bxf_ref, lamf_ref,
                     wab_ref, bab_ref, wxb_ref, bxb_ref, lamb_ref, h0_ref,
                     hsum_ref, pb_ref, g_ref, hbfin_ref, pbfin_ref, ffin_ref,
                     zext_s, hf_s, pf_s, hb_s, pbk_s, carry_s, cin_s, *pre_s):
    g = pl.program_id(1)
    ng = pl.num_programs(1)
    r, s, d = x_ref.shape
    rows = r * s
    w = LRU_WIDTH
    halo_l = CONV_LEFT * s
    n_right = CONV_W - 1 - CONV_LEFT

    @pl.when(g == 0)
    def _():
        cin_s[...] = h0_ref[...]

    sh, scl, gpre = sh_ref[...], scl_ref[...], gpre_ref[...]
    hb16 = _pre_norm_modulate(x_ref[...].reshape(rows, d), gpre, sh, scl).astype(BF16)
    xh = jnp.concatenate([xp_ref[...].reshape(halo_l, d), xn_ref[...].reshape(n_right * s, d)], axis=0)
    zh = _dot(_pre_norm_modulate(xh, gpre, sh, scl).astype(BF16), w_ref[:, :w])
    sub = lax.broadcasted_iota(jnp.int32, (s, LRU_BLOCK_W), 0)

    def in_proj(nb):
        cs = slice(nb * LRU_BLOCK_W, (nb + 1) * LRU_BLOCK_W)
        zext_s[halo_l:halo_l + rows, cs] = _dot(hb16, w_ref[:, cs])
        g_ref[:, cs] = _dot(hb16, w_ref[:, w + nb * LRU_BLOCK_W:w + (nb + 1) * LRU_BLOCK_W]).astype(BF16)
        for j in range(CONV_LEFT):
            src = halo_l + (r - CONV_LEFT + j) * s
            inner = pltpu.roll(zext_s[src:src + s, cs], 1, 0)
            edge = jnp.where(g == 0, 0.0, pltpu.roll(zh[j * s:(j + 1) * s, cs], 1, 0))
            zext_s[j * s:(j + 1) * s, cs] = jnp.where(sub == 0, edge, inner)
        for j in range(n_right):
            src = halo_l + j * s
            inner = pltpu.roll(zext_s[src:src + s, cs], s - 1, 0)
            edge = jnp.where(g == ng - 1, 0.0,
                             pltpu.roll(zh[halo_l + j * s:halo_l + (j + 1) * s, cs], s - 1, 0))
            dst = halo_l + rows + j * s
            zext_s[dst:dst + s, cs] = jnp.where(sub == s - 1, edge, inner)

    def gate_matmuls(nb):
        cs = slice(nb * LRU_BLOCK_W, (nb + 1) * LRU_BLOCK_W)
        hz = cb_ref[:, cs]
        for j in range(CONV_W):
            hz = hz + cw_ref[j:j + 1, cs] * zext_s[j * s:j * s + rows, cs]
        hzb = hz.astype(BF16)
        prf_s, pif_s, prb_s, pib_s, hzc_s = pre_s[5 * (nb % 2):5 * (nb % 2) + 5]
        prf_s[...] = _dot(hzb, waf_ref[nb]) + baf_ref[:, cs]
        pif_s[...] = _dot(hzb, wxf_ref[nb]) + bxf_ref[:, cs]
        prb_s[...] = _dot(hzb, wab_ref[nb]) + bab_ref[:, cs]
        pib_s[...] = _dot(hzb, wxb_ref[nb]) + bxb_ref[:, cs]
        hzc_s[...] = hz

    in_proj(0)
    gate_matmuls(0)
    for nb in range(LRU_BLOCKS):
        cs = slice(nb * LRU_BLOCK_W, (nb + 1) * LRU_BLOCK_W)
        if nb + 1 < LRU_BLOCKS:
            in_proj(nb + 1)
            gate_matmuls(nb + 1)
        prf_s, pif_s, prb_s, pib_s, hzc_s = pre_s[5 * (nb % 2):5 * (nb % 2) + 5]
        c1f = jnp.broadcast_to((-0.5 * LRU_C * LOG2_E) * _softplus(-lamf_ref[:, cs]), (s, LRU_BLOCK_W))
        c1b = jnp.broadcast_to((-0.5 * LRU_C * LOG2_E) * _softplus(-lamb_ref[:, cs]), (s, LRU_BLOCK_W))

        def gate(pr_s, pi_s, c1, rws):
            tr = jnp.tanh(pr_s[rws, :])
            ti = jnp.tanh(pi_s[rws, :])
            hz = hzc_s[rws, :]
            a = jnp.exp2(c1 * tr + c1)
            return a, _sqrt(1.0 - a * a) * (hz * ti + hz)

        hf = hb = jnp.zeros((s, LRU_BLOCK_W), F32)
        pf = pb = jnp.ones((s, LRU_BLOCK_W), F32)
        for t in range(r):
            rf = slice(t * s, (t + 1) * s)
            rb = slice((r - 1 - t) * s, (r - t) * s)
            af, uf = gate(prf_s, pif_s, c1f, rf)
            ab, ub = gate(prb_s, pib_s, c1b, rb)
            hf = af * hf + uf
            pf = af * pf
            hb = ab * hb + ub
            pb = ab * pb
            hf_s[rf, :] = hf
            pf_s[rf, :] = pf
            hb_s[rb, :] = hb
            pbk_s[rb, :] = pb

        cin = cin_s[:, cs]
        for c in range(s):
            carry_s[c:c + 1, :] = cin
            cin = hf[c:c + 1] + pf[c:c + 1] * cin
        cin_s[:, cs] = cin
        carry = carry_s[...][None]
        piece = min(r, 16)
        for i in range(r // piece):
            rws = slice(i * piece * s, (i + 1) * piece * s)
            shape3 = (piece, s, LRU_BLOCK_W)
            hsum = (hf_s[rws, :].reshape(shape3) + pf_s[rws, :].reshape(shape3) * carry
                    + hb_s[rws, :].reshape(shape3))
            hsum_ref[rws, cs] = hsum.reshape(piece * s, LRU_BLOCK_W).astype(BF16)
            pb_ref[rws, cs] = pbk_s[rws, :].astype(BF16)
        hbfin_ref[:, cs] = hb
        pbfin_ref[:, cs] = pb

    ffin_ref[...] = cin_s[...]


def _lru_scan(x5, sh, scl, gpre, w_in, conv_w, conv_b, p_f, p_b, h0_f):
    bsz, r, ng, s, d = x5.shape
    rows = r * s
    w = LRU_WIDTH
    n_right = CONV_W - 1 - CONV_LEFT
    once = pl.Buffered(1)
    cur = pl.BlockSpec((None, r, None, s, d), lambda b, g: (b, 0, g, 0, 0))
    prev = pl.BlockSpec((None, CONV_LEFT, None, s, d),
                        lambda b, g: (b, r // CONV_LEFT - 1, jnp.maximum(g - 1, 0), 0, 0))
    nxt = pl.BlockSpec((None, n_right, None, s, d), lambda b, g: (b, 0, jnp.minimum(g + 1, ng - 1), 0, 0))
    vec = lambda wd: pl.BlockSpec((None, 1, wd), lambda b, g: (b, 0, 0))
    full = lambda a: pl.BlockSpec(a.shape, lambda b, g: (0,) * a.ndim, pipeline_mode=once)
    row = pl.BlockSpec((None, rows, w), lambda b, g: (b, g, 0))
    fin = pl.BlockSpec((None, None, s, w), lambda b, g: (b, g, 0, 0))
    blk = lambda: pltpu.VMEM((rows, LRU_BLOCK_W), F32)
    return pl.pallas_call(
        _lru_scan_kernel,
        grid=(bsz, ng),
        in_specs=[cur, prev, nxt, vec(d), vec(d), full(gpre), full(w_in), full(conv_w), full(conv_b)]
                 + [full(a) for a in p_f] + [full(a) for a in p_b] + [vec(w)],
        out_specs=[row, row, row, fin, fin, vec(w)],
        out_shape=[jax.ShapeDtypeStruct((bsz, ng * rows, w), BF16),
                   jax.ShapeDtypeStruct((bsz, ng * rows, w), BF16),
                   jax.ShapeDtypeStruct((bsz, ng * rows, w), BF16),
                   jax.ShapeDtypeStruct((bsz, ng, s, w), F32),
                   jax.ShapeDtypeStruct((bsz, ng, s, w), F32),
                   jax.ShapeDtypeStruct((bsz, 1, w), F32)],
        scratch_shapes=[pltpu.VMEM((rows + (CONV_W - 1) * s, w), F32)] + [blk() for _ in range(4)]
                       + [pltpu.VMEM((s, LRU_BLOCK_W), F32), pltpu.VMEM((1, w), F32)]
                       + [blk() for _ in range(10)],
        compiler_params=_cparams("parallel", "arbitrary"),
        name="lru_scan",
    )(x5, x5, x5, sh, scl, gpre, w_in, conv_w, conv_b, *p_f, *p_b, h0_f)


def _lru_carry_kernel(h0_ref, hfin_ref, pfin_ref, carry_ref, final_ref):
    c = h0_ref[...]
    for k in range(hfin_ref.shape[0] - 1, -1, -1):
        carry_ref[k:k + 1, :] = c
        c = hfin_ref[k:k + 1, :] + pfin_ref[k:k + 1, :] * c
    final_ref[...] = c


def _lru_carry(h0, hfin, pfin):
    bsz, ng, s, w = hfin.shape
    vec = pl.BlockSpec((None, 1, w), lambda b: (b, 0, 0))
    runs = pl.BlockSpec((None, ng * s, w), lambda b: (b, 0, 0))
    carry, final = pl.pallas_call(
        _lru_carry_kernel,
        grid=(bsz,),
        in_specs=[vec, runs, runs],
        out_specs=[runs, vec],
        out_shape=[jax.ShapeDtypeStruct((bsz, ng * s, w), F32), jax.ShapeDtypeStruct((bsz, 1, w), F32)],
        compiler_params=_cparams("parallel"),
        name="lru_carry",
    )(h0, hfin.reshape(bsz, ng * s, w), pfin.reshape(bsz, ng * s, w))
    return carry.reshape(bsz, ng, s, w), final


def _lru_out_kernel(hsum_ref, pb_ref, cb_ref, g_ref, x_ref, gt_ref, gpost_ref, w_ref, o_ref):
    r, s, d = x_ref.shape
    w = hsum_ref.shape[-1]
    h = hsum_ref[...].astype(F32).reshape(r, s, w) + pb_ref[...].astype(F32).reshape(r, s, w) * cb_ref[...][None]
    y = _dot((h.reshape(r * s, w) * _silu(g_ref[...].astype(F32))).astype(BF16), w_ref[...])
    x = x_ref[...].reshape(r * s, d)
    o_ref[...] = _post_norm_residual(x, y, gpost_ref[...], gt_ref[...]).reshape(r, s, d)


def _lru_out(hsum, pb, cb, g, x5, gt, gpost, w, rb):
    bsz, r, ng, s, d = x5.shape
    nr = r // rb
    tm = rb * s
    width = hsum.shape[-1]
    xio = pl.BlockSpec((None, rb, None, s, d), lambda b, g, i: (b, i, g, 0, 0))
    row = pl.BlockSpec((None, tm, width), lambda b, g, i: (b, g * nr + i, 0))
    runs = pl.BlockSpec((None, None, s, width), lambda b, g, i: (b, g, 0, 0))
    vec = pl.BlockSpec((None, 1, d), lambda b, g, i: (b, 0, 0))
    full = lambda a: pl.BlockSpec(a.shape, lambda b, g, i: (0,) * a.ndim)
    return pl.pallas_call(
        _lru_out_kernel,
        grid=(bsz, ng, nr),
        in_specs=[row, row, runs, row, xio, vec, full(gpost), full(w)],
        out_specs=xio,
        out_shape=jax.ShapeDtypeStruct(x5.shape, F32),
        compiler_params=_cparams("parallel", "parallel", "parallel"),
        name="lru_out",
    )(hsum, pb, cb, g, x5, gt, gpost, w)


def kernel(x, c, ctx, c_ctx, ada_w, ada_b, norm_pre, norm_post, gla_w_in, gla_wg_f, gla_bg_f, gla_wg_b, gla_bg_b, gla_norm, gla_w_out, lru_w_in, lru_conv_w, lru_conv_b, lru_wa_f, lru_ba_f, lru_wx_f, lru_bx_f, lru_lam_f, lru_wa_b, lru_ba_b, lru_wx_b, lru_bx_b, lru_lam_b, lru_w_out):
    bsz, seq, d = x.shape
    ctx_len = ctx.shape[1]
    rows = seq // GRID_W
    assert seq % GRID_W == 0 and GRID_W % N_SEG == 0 and ctx_len % N_SEG == 0, (seq, ctx_len)
    assert rows % CONV_LEFT == 0 and (ctx_len // N_SEG) % CONV_LEFT == 0 and bsz < SUBLANES, (rows, ctx_len, bsz)

    cvec = jnp.concatenate([c, c_ctx[None], jnp.zeros((SUBLANES - bsz - 1, d), F32)], axis=0)
    mod = _ada_modulation(cvec, ada_w, ada_b)

    def mods(i):
        lat = [mod[i, :bsz, None, j * d:(j + 1) * d] for j in range(3)]
        con = [jnp.broadcast_to(mod[i, bsz, None, None, j * d:(j + 1) * d], (bsz, 1, d)) for j in range(3)]
        return lat, con

    (sh, scl, gt), (sh_c, scl_c, gt_c) = mods(0)
    gpre, gpost = norm_pre[0][None], norm_post[0][None]
    rk = GLA_GATE_RANK
    n_main = 2 * GLA_DK + 2 * GLA_DV
    w_in = gla_w_in[0][:, :n_main].astype(BF16)
    w_lr = jnp.pad(gla_w_in[0][:, n_main:], ((0, 0), (0, LANES - 2 * rk))).astype(BF16)
    wg = (jnp.pad(gla_wg_f[0], ((0, LANES - rk), (0, 0))).astype(BF16),
          jnp.pad(gla_wg_b[0], ((rk, LANES - 2 * rk), (0, 0))).astype(BF16))
    bg = (gla_bg_f[0][None], gla_bg_b[0][None])
    gla_args = (gpre, gpost, w_in, w_lr, wg, bg, gla_norm[0][None], gla_w_out[0].astype(BF16))
    s0 = jnp.zeros((bsz, GLA_HEADS, GLA_HEAD_V, GLA_HEAD_K), F32)
    ctx, s_f, s_b = _gla_layer(ctx, sh_c, scl_c, gt_c, *gla_args, s0, s0)
    x, _, _ = _gla_layer(x, sh, scl, gt, *gla_args, s_f, s_b)

    (sh, scl, gt), (sh_c, scl_c, _) = mods(1)
    gpre, gpost = norm_pre[1][None], norm_post[1][None]
    vec = lambda a: a[None]
    gate = lambda wgt, bias: (wgt.astype(BF16), vec(0.5 * bias))
    p_f = (*gate(lru_wa_f[0], lru_ba_f[0]), *gate(lru_wx_f[0], lru_bx_f[0]), vec(lru_lam_f[0]))
    p_b = (*gate(lru_wa_b[0], lru_ba_b[0]), *gate(lru_wx_b[0], lru_bx_b[0]), vec(lru_lam_b[0]))
    scan_args = (gpre, lru_w_in[0].astype(BF16), 0.5 * lru_conv_w[0], vec(0.5 * lru_conv_b[0]), p_f, p_b)
    h0 = jnp.zeros((bsz, 1, LRU_WIDTH), F32)
    cstep = ctx_len // N_SEG
    ctx5 = ctx.reshape(bsz, N_SEG, cstep, d).transpose(0, 2, 1, 3).reshape(bsz, cstep, 1, N_SEG, d)
    _, _, _, hbfin, pbfin, s_f = _lru_scan(ctx5, sh_c, scl_c, *scan_args, h0)
    _, s_b = _lru_carry(h0, hbfin, pbfin)
    x5 = x.reshape(bsz, rows, GRID_W // N_SEG, N_SEG, d)
    hsum, pb, g, hbfin, pbfin, _ = _lru_scan(x5, sh, scl, *scan_args, s_f)
    cb, _ = _lru_carry(s_b, hbfin, pbfin)
    out5 = _lru_out(hsum, pb, cb, g, x5, gt, gpost, lru_w_out[0].astype(BF16), rb=min(LRU_OUT_STEPS, rows))
    return out5.reshape(bsz, seq, d)
```

```python
import functools

import jax
import jax.numpy as jnp
from jax import lax
from jax.experimental import pallas as pl
from jax.experimental.pallas import tpu as pltpu

F32 = jnp.float32
BF16 = jnp.bfloat16
SUBLANES, LANES = 8, 128

RMS_EPS = 1e-6
LOG2_E = 1.4426950408889634
GRID_W = 64
GLA_HEADS = 4
GLA_HEAD_K = 128
GLA_HEAD_V = 256
GLA_DK = GLA_HEADS * GLA_HEAD_K
GLA_DV = GLA_HEADS * GLA_HEAD_V
GLA_GATE_RANK = 16
GLA_GATE_NORM = 16.0
GLA_CHUNK = 256
GLA_FACTORISED_MIN_CUM = -86.0
GLA_SUM_PIECE = 128
GLA_IN_ROWS = 1024
GLA_SCAN_ROWS = 1024
LRU_BLOCKS = 5
LRU_BLOCK_W = 256
LRU_WIDTH = LRU_BLOCKS * LRU_BLOCK_W
LRU_C = 8.0
CONV_W = 4
CONV_LEFT = CONV_W // 2
N_SEG = SUBLANES
LRU_OUT_STEPS = 128

VMEM_LIMIT_BYTES = 56 * 1024 * 1024


def _cparams(*sem):
    return pltpu.CompilerParams(dimension_semantics=sem, vmem_limit_bytes=VMEM_LIMIT_BYTES)


def _silu(x):
    hx = 0.5 * x
    return hx * jnp.tanh(hx) + hx


def _sqrt(x):
    return x * lax.rsqrt(jnp.maximum(x, 1e-30))


def _log_sigmoid(x):
    return jnp.minimum(x, 0.0) - jnp.log(1.0 + jnp.exp(-jnp.abs(x)))


def _softplus(x):
    return jnp.maximum(x, 0.0) + jnp.log1p(jnp.exp(-jnp.abs(x)))


def _split_bf16(x):
    hi = x.astype(BF16)
    lo = (x - hi.astype(F32)).astype(BF16)
    return hi, lo


def _dot(a, b):
    return jnp.dot(a, b, preferred_element_type=F32)


def _dot_x3(a, b):
    a_hi, a_lo = _split_bf16(a)
    b_hi, b_lo = _split_bf16(b)
    return _dot(a_hi, b_hi) + (_dot(a_hi, b_lo) + _dot(a_lo, b_hi))


def _pre_norm_modulate(x, gpre, sh, scl):
    ms = jnp.mean(x * x, axis=-1, keepdims=True)
    return (x * lax.rsqrt(ms + RMS_EPS)) * (gpre * (1.0 + scl)) + sh


def _post_norm_residual(x, y, gpost, gt):
    ms = jnp.mean(y * y, axis=-1, keepdims=True)
    return x + (y * lax.rsqrt(ms + RMS_EPS)) * (gt * gpost)


def _ada_kernel(c_ref, w_ref, b_ref, o_ref):
    sc = _silu(c_ref[...])
    o_ref[...] = _dot_x3(sc, w_ref[...]) + b_ref[...]


def _ada_modulation(cvec, ada_w, ada_b):
    depth, d, n3 = ada_w.shape
    tn = 1024
    return pl.pallas_call(
        _ada_kernel,
        grid=(depth, n3 // tn),
        in_specs=[pl.BlockSpec((SUBLANES, d), lambda i, j: (0, 0)),
                  pl.BlockSpec((None, d, tn), lambda i, j: (i, 0, j)),
                  pl.BlockSpec((None, 1, tn), lambda i, j: (i, 0, j))],
        out_specs=pl.BlockSpec((None, SUBLANES, tn), lambda i, j: (i, 0, j)),
        out_shape=jax.ShapeDtypeStruct((depth, SUBLANES, n3), F32),
        compiler_params=_cparams("parallel", "parallel"),
        name="ada_modulation",
    )(cvec, ada_w, ada_b.reshape(depth, 1, n3))


def _gla_log_decay_sums(lr, wg_ref, bg_ref, cum_ref, reverse):
    c, p = GLA_CHUNK, GLA_SUM_PIECE
    row = lax.broadcasted_iota(jnp.int32, (p, 2 * p), 0)
    col = lax.broadcasted_iota(jnp.int32, (p, 2 * p), 1) % p
    tri = ((row <= col) if reverse else (row >= col)).astype(BF16)
    z = _dot(lr, wg_ref[...]) + bg_ref[...]
    log_a = _log_sigmoid(z) * (LOG2_E / GLA_GATE_NORM)
    hi, lo = _split_bf16(log_a)
    for i in range(lr.shape[0] // c):
        carry = None
        pieces = range(c // p)
        for j in (reversed(pieces) if reverse else pieces):
            rows = slice(i * c + j * p, i * c + (j + 1) * p)
            s = _dot(tri, jnp.concatenate([hi[rows], lo[rows]], axis=0))
            if carry is not None:
                s = s + carry
            cum_ref[rows, :] = s
            carry = s[0:1] if reverse else s[p - 1:p]


def _gla_in_kernel(x_ref, sh_ref, scl_ref, gpre_ref, w_ref, wlr_ref, wgf_ref, bgf_ref, wgb_ref, bgb_ref, s0_ref,
                   q_ref, k_ref, v_ref, g_ref, cumb_ref, of_ref, sfin_ref,
                   s_ref, cumf_ref, kf_s, cm_s, sc_s):
    n = pl.program_id(1)

    @pl.when(n == 0)
    def _():
        s_ref[...] = s0_ref[...]

    hb16 = _pre_norm_modulate(x_ref[...], gpre_ref[...], sh_ref[...], scl_ref[...]).astype(BF16)
    lr = _dot(hb16, wlr_ref[...]).astype(BF16)
    proj = _dot(hb16, w_ref[...])
    _gla_log_decay_sums(lr, wgf_ref, bgf_ref, cumf_ref, reverse=False)
    _gla_log_decay_sums(lr, wgb_ref, bgb_ref, cumb_ref, reverse=True)
    q_ref[...] = (proj[:, :GLA_DK] * (GLA_HEAD_K ** -0.5)).astype(BF16)
    k_ref[...] = proj[:, GLA_DK:2 * GLA_DK].astype(BF16)
    v_ref[...] = proj[:, 2 * GLA_DK:2 * GLA_DK + GLA_DV].astype(BF16)
    g_ref[...] = _silu(proj[:, 2 * GLA_DK + GLA_DV:]).astype(BF16)

    def put_o(rows, vc, o):
        of_ref[rows, vc] = o.astype(BF16)

    _gla_guarded_scan(q_ref, k_ref, v_ref, cumf_ref, s_ref, put_o, kf_s, cm_s, sc_s, reverse=False)

    @pl.when(n == pl.num_programs(1) - 1)
    def _():
        sfin_ref[...] = s_ref[...]


def _gla_in(x, sh, scl, gpre, w, w_lr, wg, bg, s0, tm):
    bsz, t, d = x.shape
    c = GLA_CHUNK
    row = lambda wd: pl.BlockSpec((None, tm, wd), lambda b, i: (b, i, 0))
    vec = lambda wd: pl.BlockSpec((None, 1, wd), lambda b, i: (b, 0, 0))
    full = lambda a: pl.BlockSpec(a.shape, lambda b, i: (0,) * a.ndim)
    st = pl.BlockSpec((None, GLA_HEADS, GLA_HEAD_V, GLA_HEAD_K), lambda b, i: (b, 0, 0, 0))
    return pl.pallas_call(
        _gla_in_kernel,
        grid=(bsz, t // tm),
        in_specs=[row(d), vec(d), vec(d), full(gpre), full(w), full(w_lr),
                  full(wg[0]), full(bg[0]), full(wg[1]), full(bg[1]), st],
        out_specs=[row(GLA_DK), row(GLA_DK), row(GLA_DV), row(GLA_DV), row(GLA_DK), row(GLA_DV), st],
        out_shape=[jax.ShapeDtypeStruct((bsz, t, GLA_DK), BF16),
                   jax.ShapeDtypeStruct((bsz, t, GLA_DK), BF16),
                   jax.ShapeDtypeStruct((bsz, t, GLA_DV), BF16),
                   jax.ShapeDtypeStruct((bsz, t, GLA_DV), BF16),
                   jax.ShapeDtypeStruct((bsz, t, GLA_DK), F32),
                   jax.ShapeDtypeStruct((bsz, t, GLA_DV), BF16),
                   jax.ShapeDtypeStruct(s0.shape, F32)],
        scratch_shapes=[pltpu.VMEM((GLA_HEADS, GLA_HEAD_V, GLA_HEAD_K), F32), pltpu.VMEM((tm, GLA_DK), F32),
                        pltpu.VMEM((c, GLA_HEAD_K), F32), pltpu.VMEM((c, GLA_HEAD_K), F32),
                        pltpu.VMEM((c, c), F32)],
        compiler_params=_cparams("parallel", "arbitrary"),
        name="gla_in_fwd",
    )(x, sh, scl, gpre, w, w_lr, wg[0], bg[0], wg[1], bg[1], s0)


def _gla_scores_exact(q, k, cum, reverse, kf_s, cm_s, sc_s):
    c = q.shape[0]
    kf_s[...] = k
    cm_s[...] = cum
    sc_s[...] = jnp.zeros_like(sc_s)
    ridx = lax.broadcasted_iota(jnp.int32, (c, 1), 0)
    cidx = lax.broadcasted_iota(jnp.int32, (c, c), 1)

    def column(j, carry):
        kj = kf_s[pl.ds(j, 1), :]
        cj = cm_s[pl.ds(j, 1), :]
        live = (ridx <= j) if reverse else (ridx >= j)
        dec = jnp.exp2(jnp.where(live, jnp.minimum(cum - cj, 0.0), -1e30))
        sj = jnp.sum(q * kj * dec, axis=-1, keepdims=True)
        sc_s[...] += jnp.where(cidx == j, sj, 0.0)
        return carry

    lax.fori_loop(0, c, column, 0)
    return sc_s[...]


def _gla_chunks(q_ref, k_ref, v_ref, cum_ref, s_ref, put_o, kf_s, cm_s, sc_s, *, reverse, factorised):
    tb = q_ref.shape[0]
    c = GLA_CHUNK
    row = lax.broadcasted_iota(jnp.int32, (c, c), 0)
    col = lax.broadcasted_iota(jnp.int32, (c, c), 1)
    keep = (row <= col) if reverse else (row >= col)
    chunks = range(tb // c)
    for i in (reversed(chunks) if reverse else chunks):
        rows = slice(i * c, (i + 1) * c)
        for hd in range(GLA_HEADS):
            kc = slice(hd * GLA_HEAD_K, (hd + 1) * GLA_HEAD_K)
            vc = slice(hd * GLA_HEAD_V, (hd + 1) * GLA_HEAD_V)
            cum = cum_ref[rows, kc]
            last = cum[0:1] if reverse else cum[c - 1:c]
            q = q_ref[rows, kc].astype(F32)
            k = k_ref[rows, kc].astype(F32)
            v = v_ref[rows, vc]
            qe = (q * jnp.exp2(cum)).astype(BF16)
            kl = (k * jnp.exp2(last - cum)).astype(BF16)
            st = s_ref[hd]
            if factorised:
                ke = (k * jnp.exp2(-cum)).astype(BF16)
                scores = lax.dot_general(qe, ke, (((1,), (1,)), ((), ())), preferred_element_type=F32)
                scores = jnp.where(keep, scores, 0.0)
            else:
                scores = _gla_scores_exact(q, k, cum, reverse, kf_s, cm_s, sc_s)
            o = lax.dot_general(qe, st.astype(BF16), (((1,), (1,)), ((), ())),
                                preferred_element_type=F32)
            o = o + _dot(scores.astype(BF16), v)
            put_o(rows, vc, o)
            upd = lax.dot_general(v, kl, (((0,), (0,)), ((), ())), preferred_element_type=F32)
            s_ref[hd] = st * jnp.exp2(last) + upd


def _gla_guarded_scan(q_ref, k_ref, v_ref, cum_ref, s_ref, put_o, kf_s, cm_s, sc_s, *, reverse):
    c = GLA_CHUNK
    ends = [cum_ref[i * c:i * c + 1, :] if reverse else cum_ref[(i + 1) * c - 1:(i + 1) * c, :]
            for i in range(q_ref.shape[0] // c)]
    safe = jnp.min(functools.reduce(jnp.minimum, ends)) >= GLA_FACTORISED_MIN_CUM
    scan = functools.partial(_gla_chunks, q_ref, k_ref, v_ref, cum_ref, s_ref, put_o, kf_s, cm_s, sc_s,
                             reverse=reverse)
    pl.when(safe)(functools.partial(scan, factorised=True))
    pl.when(jnp.logical_not(safe))(functools.partial(scan, factorised=False))


def _gla_bwd_out_kernel(q_ref, k_ref, v_ref, cum_ref, s0_ref, of_ref, g_ref, x_ref, gt_ref, gn_ref, gpost_ref,
                        wout_ref, y_ref, sfin_ref, s_ref, kf_s, cm_s, sc_s, o_s):
    n = pl.program_id(1)

    @pl.when(n == 0)
    def _():
        s_ref[...] = s0_ref[...]

    def put_o(rows, vc, o):
        o_s[rows, vc] = o + of_ref[rows, vc].astype(F32)

    _gla_guarded_scan(q_ref, k_ref, v_ref, cum_ref, s_ref, put_o, kf_s, cm_s, sc_s, reverse=True)

    gn = gn_ref[...]
    parts = []
    for hd in range(GLA_HEADS):
        oh = o_s[:, hd * GLA_HEAD_V:(hd + 1) * GLA_HEAD_V]
        ms = jnp.mean(oh * oh, axis=-1, keepdims=True)
        parts.append(oh * lax.rsqrt(ms + RMS_EPS) * gn)
    on = jnp.concatenate(parts, axis=-1) * g_ref[...].astype(F32)
    y = _dot(on.astype(BF16), wout_ref[...])
    y_ref[...] = _post_norm_residual(x_ref[...], y, gpost_ref[...], gt_ref[...])

    @pl.when(n == pl.num_programs(1) - 1)
    def _():
        sfin_ref[...] = s_ref[...]


def _gla_bwd_out(q, k, v, cum, s0, o_f, g, x, gt, gn, gpost, w_out, tb):
    bsz, t, d = x.shape
    nblk = t // tb
    c = GLA_CHUNK
    row = lambda wd: pl.BlockSpec((None, tb, wd), lambda b, n: (b, nblk - 1 - n, 0))
    vec = lambda wd: pl.BlockSpec((None, 1, wd), lambda b, n: (b, 0, 0))
    full = lambda a: pl.BlockSpec(a.shape, lambda b, n: (0,) * a.ndim)
    st = pl.BlockSpec((None, GLA_HEADS, GLA_HEAD_V, GLA_HEAD_K), lambda b, n: (b, 0, 0, 0))
    return pl.pallas_call(
        _gla_bwd_out_kernel,
        grid=(bsz, nblk),
        in_specs=[row(GLA_DK), row(GLA_DK), row(GLA_DV), row(GLA_DK), st,
                  row(GLA_DV), row(GLA_DV), row(d), vec(d), full(gn), full(gpost), full(w_out)],
        out_specs=[row(d), st],
        out_shape=[jax.ShapeDtypeStruct((bsz, t, d), F32), jax.ShapeDtypeStruct(s0.shape, F32)],
        scratch_shapes=[pltpu.VMEM((GLA_HEADS, GLA_HEAD_V, GLA_HEAD_K), F32),
                        pltpu.VMEM((c, GLA_HEAD_K), F32), pltpu.VMEM((c, GLA_HEAD_K), F32), pltpu.VMEM((c, c), F32),
                        pltpu.VMEM((tb, GLA_DV), F32)],
        compiler_params=_cparams("parallel", "arbitrary"),
        name="gla_bwd_out",
    )(q, k, v, cum, s0, o_f, g, x, gt, gn, gpost, w_out)


def _gla_layer(x, sh, scl, gt, gpre, gpost, w_in, w_lr, wg, bg, gn, w_out, s0_f, s0_b):
    t = x.shape[1]
    tm, tb = min(GLA_IN_ROWS, t), min(GLA_SCAN_ROWS, t)
    assert t % tm == 0 and t % tb == 0 and tm % GLA_CHUNK == 0 and tb % GLA_CHUNK == 0, (t, tm, tb)
    q, k, v, g, cum_b, o_f, s_f = _gla_in(x, sh, scl, gpre, w_in, w_lr, wg, bg, s0_f, tm)
    x_new, s_b = _gla_bwd_out(q, k, v, cum_b, s0_b, o_f, g, x, gt, gn, gpost, w_out, tb)
    return x_new, s_f, s_b


def _lru_scan_kernel(x_ref, xp_ref, xn_ref, sh_ref, scl_ref, gpre_ref, w_ref, cw_ref, cb_ref,
                     waf_ref, baf_ref, wxf_ref, bxf_ref, lamf_ref,
                     wab_ref, bab_ref, wxb_ref, bxb_ref, lamb_ref, h0_ref,
                     hsum_ref, pb_ref, g_ref, hbfin_ref, pbfin_ref, ffin_ref,
                     zext_s, hf_s, pf_s, hb_s, pbk_s, carry_s, cin_s, *pre_s):
    g = pl.program_id(1)
    ng = pl.num_programs(1)
    r, s, d = x_ref.shape
    rows = r * s
    w = LRU_WIDTH
    halo_l = CONV_LEFT * s
    n_right = CONV_W - 1 - CONV_LEFT

    @pl.when(g == 0)
    def _():
        cin_s[...] = h0_ref[...]

    sh, scl, gpre = sh_ref[...], scl_ref[...], gpre_ref[...]
    hb16 = _pre_norm_modulate(x_ref[...].reshape(rows, d), gpre, sh, scl).astype(BF16)
    xh = jnp.concatenate([xp_ref[...].reshape(halo_l, d), xn_ref[...].reshape(n_right * s, d)], axis=0)
    zh = _dot(_pre_norm_modulate(xh, gpre, sh, scl).astype(BF16), w_ref[:, :w])
    sub = lax.broadcasted_iota(jnp.int32, (s, LRU_BLOCK_W), 0)

    def in_proj(nb):
        cs = slice(nb * LRU_BLOCK_W, (nb + 1) * LRU_BLOCK_W)
        zext_s[halo_l:halo_l + rows, cs] = _dot(hb16, w_ref[:, cs])
        for j in range(CONV_LEFT):
            src = halo_l + (r - CONV_LEFT + j) * s
            inner = pltpu.roll(zext_s[src:src + s, cs], 1, 0)
            edge = jnp.where(g == 0, 0.0, pltpu.roll(zh[j * s:(j + 1) * s, cs], 1, 0))
            zext_s[j * s:(j + 1) * s, cs] = jnp.where(sub == 0, edge, inner)
        for j in range(n_right):
            src = halo_l + j * s
            inner = pltpu.roll(zext_s[src:src + s, cs], s - 1, 0)
            edge = jnp.where(g == ng - 1, 0.0,
                             pltpu.roll(zh[halo_l + j * s:halo_l + (j + 1) * s, cs], s - 1, 0))
            dst = halo_l + rows + j * s
            zext_s[dst:dst + s, cs] = jnp.where(sub == s - 1, edge, inner)

    def gate_matmuls(nb):
        cs = slice(nb * LRU_BLOCK_W, (nb + 1) * LRU_BLOCK_W)
        hz = cb_ref[:, cs]
        for j in range(CONV_W):
            hz = hz + cw_ref[j:j + 1, cs] * zext_s[j * s:j * s + rows, cs]
        hzb = hz.astype(BF16)
        prf_s, pif_s, prb_s, pib_s, hzc_s = pre_s[5 * (nb % 2):5 * (nb % 2) + 5]
        prf_s[...] = _dot(hzb, waf_ref[nb]) + baf_ref[:, cs]
        pif_s[...] = _dot(hzb, wxf_ref[nb]) + bxf_ref[:, cs]
        prb_s[...] = _dot(hzb, wab_ref[nb]) + bab_ref[:, cs]
        pib_s[...] = _dot(hzb, wxb_ref[nb]) + bxb_ref[:, cs]
        hzc_s[...] = hz

    in_proj(0)
    gate_matmuls(0)
    for nb in range(LRU_BLOCKS):
        cs = slice(nb * LRU_BLOCK_W, (nb + 1) * LRU_BLOCK_W)
        if nb + 1 < LRU_BLOCKS:
            in_proj(nb + 1)
            gate_matmuls(nb + 1)
        else:
            g_ref[...] = _dot(hb16, w_ref[:, w:]).astype(BF16)
        prf_s, pif_s, prb_s, pib_s, hzc_s = pre_s[5 * (nb % 2):5 * (nb % 2) + 5]
        c1f = jnp.broadcast_to((-0.5 * LRU_C * LOG2_E) * _softplus(-lamf_ref[:, cs]), (s, LRU_BLOCK_W))
        c1b = jnp.broadcast_to((-0.5 * LRU_C * LOG2_E) * _softplus(-lamb_ref[:, cs]), (s, LRU_BLOCK_W))

        def gate(pr_s, pi_s, c1, rws):
            tr = jnp.tanh(pr_s[rws, :])
            ti = jnp.tanh(pi_s[rws, :])
            hz = hzc_s[rws, :]
            a = jnp.exp2(c1 * tr + c1)
            return a, _sqrt(1.0 - a * a) * (hz * ti + hz)

        hf = hb = jnp.zeros((s, LRU_BLOCK_W), F32)
        pf = pb = jnp.ones((s, LRU_BLOCK_W), F32)
        for t in range(r):
            rf = slice(t * s, (t + 1) * s)
            rb = slice((r - 1 - t) * s, (r - t) * s)
            af, uf = gate(prf_s, pif_s, c1f, rf)
            ab, ub = gate(prb_s, pib_s, c1b, rb)
            hf = af * hf + uf
            pf = af * pf
            hb = ab * hb + ub
            pb = ab * pb
            hf_s[rf, :] = hf
            pf_s[rf, :] = pf
            hb_s[rb, :] = hb
            pbk_s[rb, :] = pb

        cin = cin_s[:, cs]
        for c in range(s):
            carry_s[c:c + 1, :] = cin
            cin = hf[c:c + 1] + pf[c:c + 1] * cin
        cin_s[:, cs] = cin
        carry = carry_s[...][None]
        piece = min(r, 16)
        for i in range(r // piece):
            rws = slice(i * piece * s, (i + 1) * piece * s)
            shape3 = (piece, s, LRU_BLOCK_W)
            hsum = (hf_s[rws, :].reshape(shape3) + pf_s[rws, :].reshape(shape3) * carry
                    + hb_s[rws, :].reshape(shape3))
            hsum_ref[rws, cs] = hsum.reshape(piece * s, LRU_BLOCK_W).astype(BF16)
            pb_ref[rws, cs] = pbk_s[rws, :].astype(BF16)
        hbfin_ref[:, cs] = hb
        pbfin_ref[:, cs] = pb

    ffin_ref[...] = cin_s[...]


def _lru_scan(x5, sh, scl, gpre, w_in, conv_w, conv_b, p_f, p_b, h0_f):
    bsz, r, ng, s, d = x5.shape
    rows = r * s
    w = LRU_WIDTH
    n_right = CONV_W - 1 - CONV_LEFT
    once = pl.Buffered(1)
    cur = pl.BlockSpec((None, r, None, s, d), lambda b, g: (b, 0, g, 0, 0))
    prev = pl.BlockSpec((None, CONV_LEFT, None, s, d),
                        lambda b, g: (b, r // CONV_LEFT - 1, jnp.maximum(g - 1, 0), 0, 0))
    nxt = pl.BlockSpec((None, n_right, None, s, d), lambda b, g: (b, 0, jnp.minimum(g + 1, ng - 1), 0, 0))
    vec = lambda wd: pl.BlockSpec((None, 1, wd), lambda b, g: (b, 0, 0))
    full = lambda a: pl.BlockSpec(a.shape, lambda b, g: (0,) * a.ndim, pipeline_mode=once)
    row = pl.BlockSpec((None, rows, w), lambda b, g: (b, g, 0))
    fin = pl.BlockSpec((None, None, s, w), lambda b, g: (b, g, 0, 0))
    blk = lambda: pltpu.VMEM((rows, LRU_BLOCK_W), F32)
    return pl.pallas_call(
        _lru_scan_kernel,
        grid=(bsz, ng),
        in_specs=[cur, prev, nxt, vec(d), vec(d), full(gpre), full(w_in), full(conv_w), full(conv_b)]
                 + [full(a) for a in p_f] + [full(a) for a in p_b] + [vec(w)],
        out_specs=[row, row, row, fin, fin, vec(w)],
        out_shape=[jax.ShapeDtypeStruct((bsz, ng * rows, w), BF16),
                   jax.ShapeDtypeStruct((bsz, ng * rows, w), BF16),
                   jax.ShapeDtypeStruct((bsz, ng * rows, w), BF16),
                   jax.ShapeDtypeStruct((bsz, ng, s, w), F32),
                   jax.ShapeDtypeStruct((bsz, ng, s, w), F32),
                   jax.ShapeDtypeStruct((bsz, 1, w), F32)],
        scratch_shapes=[pltpu.VMEM((rows + (CONV_W - 1) * s, w), F32)] + [blk() for _ in range(4)]
                       + [pltpu.VMEM((s, LRU_BLOCK_W), F32), pltpu.VMEM((1, w), F32)]
                       + [blk() for _ in range(10)],
        compiler_params=_cparams("parallel", "arbitrary"),
        name="lru_scan",
    )(x5, x5, x5, sh, scl, gpre, w_in, conv_w, conv_b, *p_f, *p_b, h0_f)


def _lru_carry_kernel(h0_ref, hfin_ref, pfin_ref, carry_ref, final_ref):
    c = h0_ref[...]
    for k in range(hfin_ref.shape[0] - 1, -1, -1):
        carry_ref[k:k + 1, :] = c
        c = hfin_ref[k:k + 1, :] + pfin_ref[k:k + 1, :] * c
    final_ref[...] = c


def _lru_carry(h0, hfin, pfin):
    bsz, ng, s, w = hfin.shape
    vec = pl.BlockSpec((None, 1, w), lambda b: (b, 0, 0))
    runs = pl.BlockSpec((None, ng * s, w), lambda b: (b, 0, 0))
    carry, final = pl.pallas_call(
        _lru_carry_kernel,
        grid=(bsz,),
        in_specs=[vec, runs, runs],
        out_specs=[runs, vec],
        out_shape=[jax.ShapeDtypeStruct((bsz, ng * s, w), F32), jax.ShapeDtypeStruct((bsz, 1, w), F32)],
        compiler_params=_cparams("parallel"),
        name="lru_carry",
    )(h0, hfin.reshape(bsz, ng * s, w), pfin.reshape(bsz, ng * s, w))
    return carry.reshape(bsz, ng, s, w), final


def _lru_out_kernel(hsum_ref, pb_ref, cb_ref, g_ref, x_ref, gt_ref, gpost_ref, w_ref, o_ref):
    r, s, d = x_ref.shape
    w = hsum_ref.shape[-1]
    h = hsum_ref[...].astype(F32).reshape(r, s, w) + pb_ref[...].astype(F32).reshape(r, s, w) * cb_ref[...][None]
    y = _dot((h.reshape(r * s, w) * _silu(g_ref[...].astype(F32))).astype(BF16), w_ref[...])
    x = x_ref[...].reshape(r * s, d)
    o_ref[...] = _post_norm_residual(x, y, gpost_ref[...], gt_ref[...]).reshape(r, s, d)


def _lru_out(hsum, pb, cb, g, x5, gt, gpost, w, rb):
    bsz, r, ng, s, d = x5.shape
    nr = r // rb
    tm = rb * s
    width = hsum.shape[-1]
    xio = pl.BlockSpec((None, rb, None, s, d), lambda b, g, i: (b, i, g, 0, 0))
    row = pl.BlockSpec((None, tm, width), lambda b, g, i: (b, g * nr + i, 0))
    runs = pl.BlockSpec((None, None, s, width), lambda b, g, i: (b, g, 0, 0))
    vec = pl.BlockSpec((None, 1, d), lambda b, g, i: (b, 0, 0))
    full = lambda a: pl.BlockSpec(a.shape, lambda b, g, i: (0,) * a.ndim)
    return pl.pallas_call(
        _lru_out_kernel,
        grid=(bsz, ng, nr),
        in_specs=[row, row, runs, row, xio, vec, full(gpost), full(w)],
        out_specs=xio,
        out_shape=jax.ShapeDtypeStruct(x5.shape, F32),
        compiler_params=_cparams("parallel", "parallel", "parallel"),
        name="lru_out",
    )(hsum, pb, cb, g, x5, gt, gpost, w)


def kernel(x, c, ctx, c_ctx, ada_w, ada_b, norm_pre, norm_post, gla_w_in, gla_wg_f, gla_bg_f, gla_wg_b, gla_bg_b, gla_norm, gla_w_out, lru_w_in, lru_conv_w, lru_conv_b, lru_wa_f, lru_ba_f, lru_wx_f, lru_bx_f, lru_lam_f, lru_wa_b, lru_ba_b, lru_wx_b, lru_bx_b, lru_lam_b, lru_w_out):
    bsz, seq, d = x.shape
    ctx_len = ctx.shape[1]
    rows = seq // GRID_W
    assert seq % GRID_W == 0 and GRID_W % N_SEG == 0 and ctx_len % N_SEG == 0, (seq, ctx_len)
    assert rows % CONV_LEFT == 0 and (ctx_len // N_SEG) % CONV_LEFT == 0 and bsz < SUBLANES, (rows, ctx_len, bsz)

    cvec = jnp.concatenate([c, c_ctx[None], jnp.zeros((SUBLANES - bsz - 1, d), F32)], axis=0)
    mod = _ada_modulation(cvec, ada_w, ada_b)

    def mods(i):
        lat = [mod[i, :bsz, None, j * d:(j + 1) * d] for j in range(3)]
        con = [jnp.broadcast_to(mod[i, bsz, None, None, j * d:(j + 1) * d], (bsz, 1, d)) for j in range(3)]
        return lat, con

    (sh, scl, gt), (sh_c, scl_c, gt_c) = mods(0)
    gpre, gpost = norm_pre[0][None], norm_post[0][None]
    rk = GLA_GATE_RANK
    n_main = 2 * GLA_DK + 2 * GLA_DV
    w_in = gla_w_in[0][:, :n_main].astype(BF16)
    w_lr = jnp.pad(gla_w_in[0][:, n_main:], ((0, 0), (0, LANES - 2 * rk))).astype(BF16)
    wg = (jnp.pad(gla_wg_f[0], ((0, LANES - rk), (0, 0))).astype(BF16),
          jnp.pad(gla_wg_b[0], ((rk, LANES - 2 * rk), (0, 0))).astype(BF16))
    bg = (gla_bg_f[0][None], gla_bg_b[0][None])
    gla_args = (gpre, gpost, w_in, w_lr, wg, bg, gla_norm[0][None], gla_w_out[0].astype(BF16))
    s0 = jnp.zeros((bsz, GLA_HEADS, GLA_HEAD_V, GLA_HEAD_K), F32)
    ctx, s_f, s_b = _gla_layer(ctx, sh_c, scl_c, gt_c, *gla_args, s0, s0)
    x, _, _ = _gla_layer(x, sh, scl, gt, *gla_args, s_f, s_b)

    (sh, scl, gt), (sh_c, scl_c, _) = mods(1)
    gpre, gpost = norm_pre[1][None], norm_post[1][None]
    vec = lambda a: a[None]
    gate = lambda wgt, bias: (wgt.astype(BF16), vec(0.5 * bias))
    p_f = (*gate(lru_wa_f[0], lru_ba_f[0]), *gate(lru_wx_f[0], lru_bx_f[0]), vec(lru_lam_f[0]))
    p_b = (*gate(lru_wa_b[0], lru_ba_b[0]), *gate(lru_wx_b[0], lru_bx_b[0]), vec(lru_lam_b[0]))
    scan_args = (gpre, lru_w_in[0].astype(BF16), 0.5 * lru_conv_w[0], vec(0.5 * lru_conv_b[0]), p_f, p_b)
    h0 = jnp.zeros((bsz, 1, LRU_WIDTH), F32)
    cstep = ctx_len // N_SEG
    ctx5 = ctx.reshape(bsz, N_SEG, cstep, d).transpose(0, 2, 1, 3).reshape(bsz, cstep, 1, N_SEG, d)
    _, _, _, hbfin, pbfin, s_f = _lru_scan(ctx5, sh_c, scl_c, *scan_args, h0)
    _, s_b = _lru_carry(h0, hbfin, pbfin)
    x5 = x.reshape(bsz, rows, GRID_W // N_SEG, N_SEG, d)
    hsum, pb, g, hbfin, pbfin, _ = _lru_scan(x5, sh, scl, *scan_args, s_f)
    cb, _ = _lru_carry(s_b, hbfin, pbfin)
    out5 = _lru_out(hsum, pb, cb, g, x5, gt, gpost, lru_w_out[0].astype(BF16), rb=min(LRU_OUT_STEPS, rows))
    return out5.reshape(bsz, seq, d)
```

```python
import functools

import jax
import jax.numpy as jnp
from jax import lax
from jax.experimental import pallas as pl
from jax.experimental.pallas import tpu as pltpu

F32 = jnp.float32
BF16 = jnp.bfloat16
SUBLANES, LANES = 8, 128

RMS_EPS = 1e-6
LOG2_E = 1.4426950408889634
GRID_W = 64
GLA_HEADS = 4
GLA_HEAD_K = 128
GLA_HEAD_V = 256
GLA_DK = GLA_HEADS * GLA_HEAD_K
GLA_DV = GLA_HEADS * GLA_HEAD_V
GLA_GATE_RANK = 16
GLA_GATE_NORM = 16.0
GLA_CHUNK = 256
GLA_FACTORISED_MIN_CUM = -86.0
GLA_SUM_PIECE = 128
GLA_IN_ROWS = 1024
GLA_SCAN_ROWS = 1024
X_RING_SLOTS = 3
LRU_BLOCKS = 5
LRU_BLOCK_W = 256
LRU_WIDTH = LRU_BLOCKS * LRU_BLOCK_W
LRU_C = 8.0
CONV_W = 4
CONV_LEFT = CONV_W // 2
N_SEG = SUBLANES
LRU_OUT_STEPS = 128

VMEM_LIMIT_BYTES = 56 * 1024 * 1024


def _cparams(*sem):
    return pltpu.CompilerParams(dimension_semantics=sem, vmem_limit_bytes=VMEM_LIMIT_BYTES)


def _silu(x):
    hx = 0.5 * x
    return hx * jnp.tanh(hx) + hx


def _sqrt(x):
    return x * lax.rsqrt(jnp.maximum(x, 1e-30))


def _log_sigmoid(x):
    return jnp.minimum(x, 0.0) - jnp.log(1.0 + jnp.exp(-jnp.abs(x)))


def _softplus(x):
    return jnp.maximum(x, 0.0) + jnp.log1p(jnp.exp(-jnp.abs(x)))


def _split_bf16(x):
    hi = x.astype(BF16)
    lo = (x - hi.astype(F32)).astype(BF16)
    return hi, lo


def _dot(a, b):
    return jnp.dot(a, b, preferred_element_type=F32)


def _dot_x3(a, b):
    a_hi, a_lo = _split_bf16(a)
    b_hi, b_lo = _split_bf16(b)
    return _dot(a_hi, b_hi) + (_dot(a_hi, b_lo) + _dot(a_lo, b_hi))


def _pre_norm_modulate(x, gpre, sh, scl):
    ms = jnp.mean(x * x, axis=-1, keepdims=True)
    return (x * lax.rsqrt(ms + RMS_EPS)) * (gpre * (1.0 + scl)) + sh


def _post_norm_residual(x, y, gpost, gt):
    ms = jnp.mean(y * y, axis=-1, keepdims=True)
    return x + (y * lax.rsqrt(ms + RMS_EPS)) * (gt * gpost)


def _ada_kernel(c_ref, w_ref, b_ref, o_ref):
    sc = _silu(c_ref[...])
    o_ref[...] = _dot_x3(sc, w_ref[...]) + b_ref[...]


def _ada_modulation(cvec, ada_w, ada_b):
    depth, d, n3 = ada_w.shape
    tn = 1024
    return pl.pallas_call(
        _ada_kernel,
        grid=(depth, n3 // tn),
        in_specs=[pl.BlockSpec((SUBLANES, d), lambda i, j: (0, 0)),
                  pl.BlockSpec((None, d, tn), lambda i, j: (i, 0, j)),
                  pl.BlockSpec((None, 1, tn), lambda i, j: (i, 0, j))],
        out_specs=pl.BlockSpec((None, SUBLANES, tn), lambda i, j: (i, 0, j)),
        out_shape=jax.ShapeDtypeStruct((depth, SUBLANES, n3), F32),
        compiler_params=_cparams("parallel", "parallel"),
        name="ada_modulation",
    )(cvec, ada_w, ada_b.reshape(depth, 1, n3))


def _gla_log_decay_sums(lr, wg_ref, bg_ref, cum_ref, reverse):
    c, p = GLA_CHUNK, GLA_SUM_PIECE
    row = lax.broadcasted_iota(jnp.int32, (p, 2 * p), 0)
    col = lax.broadcasted_iota(jnp.int32, (p, 2 * p), 1) % p
    tri = ((row <= col) if reverse else (row >= col)).astype(BF16)
    z = _dot(lr, wg_ref[...]) + bg_ref[...]
    log_a = _log_sigmoid(z) * (LOG2_E / GLA_GATE_NORM)
    hi, lo = _split_bf16(log_a)
    for i in range(lr.shape[0] // c):
        carry = None
        pieces = range(c // p)
        for j in (reversed(pieces) if reverse else pieces):
            rows = slice(i * c + j * p, i * c + (j + 1) * p)
            s = _dot(tri, jnp.concatenate([hi[rows], lo[rows]], axis=0))
            if carry is not None:
                s = s + carry
            cum_ref[rows, :] = s
            carry = s[0:1] if reverse else s[p - 1:p]


def _gla_in_kernel(x_ref, sh_ref, scl_ref, gpre_ref, w_ref, wlr_ref, wgf_ref, bgf_ref, wgb_ref, bgb_ref, s0_ref,
                   q_ref, k_ref, v_ref, g_ref, cumb_ref, of_ref, sfin_ref,
                   s_ref, cumf_ref, kf_s, cm_s, sc_s):
    n = pl.program_id(1)

    @pl.when(n == 0)
    def _():
        s_ref[...] = s0_ref[...]

    hb16 = _pre_norm_modulate(x_ref[...], gpre_ref[...], sh_ref[...], scl_ref[...]).astype(BF16)
    lr = _dot(hb16, wlr_ref[...]).astype(BF16)
    proj = _dot(hb16, w_ref[...])
    _gla_log_decay_sums(lr, wgf_ref, bgf_ref, cumf_ref, reverse=False)
    _gla_log_decay_sums(lr, wgb_ref, bgb_ref, cumb_ref, reverse=True)
    q_ref[...] = (proj[:, :GLA_DK] * (GLA_HEAD_K ** -0.5)).astype(BF16)
    k_ref[...] = proj[:, GLA_DK:2 * GLA_DK].astype(BF16)
    v_ref[...] = proj[:, 2 * GLA_DK:2 * GLA_DK + GLA_DV].astype(BF16)
    g_ref[...] = _silu(proj[:, 2 * GLA_DK + GLA_DV:]).astype(BF16)

    def put_o(rows, vc, o):
        of_ref[rows, vc] = o.astype(BF16)

    _gla_guarded_scan(q_ref, k_ref, v_ref, cumf_ref, s_ref, put_o, kf_s, cm_s, sc_s, reverse=False)

    @pl.when(n == pl.num_programs(1) - 1)
    def _():
        sfin_ref[...] = s_ref[...]


def _gla_in(x, sh, scl, gpre, w, w_lr, wg, bg, s0, tm):
    bsz, t, d = x.shape
    c = GLA_CHUNK
    row = lambda wd: pl.BlockSpec((None, tm, wd), lambda b, i: (b, i, 0))
    vec = lambda wd: pl.BlockSpec((None, 1, wd), lambda b, i: (b, 0, 0))
    full = lambda a: pl.BlockSpec(a.shape, lambda b, i: (0,) * a.ndim)
    st = pl.BlockSpec((None, GLA_HEADS, GLA_HEAD_V, GLA_HEAD_K), lambda b, i: (b, 0, 0, 0))
    return pl.pallas_call(
        _gla_in_kernel,
        grid=(bsz, t // tm),
        in_specs=[row(d), vec(d), vec(d), full(gpre), full(w), full(w_lr),
                  full(wg[0]), full(bg[0]), full(wg[1]), full(bg[1]), st],
        out_specs=[row(GLA_DK), row(GLA_DK), row(GLA_DV), row(GLA_DV), row(GLA_DK), row(GLA_DV), st],
        out_shape=[jax.ShapeDtypeStruct((bsz, t, GLA_DK), BF16),
                   jax.ShapeDtypeStruct((bsz, t, GLA_DK), BF16),
                   jax.ShapeDtypeStruct((bsz, t, GLA_DV), BF16),
                   jax.ShapeDtypeStruct((bsz, t, GLA_DV), BF16),
                   jax.ShapeDtypeStruct((bsz, t, GLA_DK), F32),
                   jax.ShapeDtypeStruct((bsz, t, GLA_DV), BF16),
                   jax.ShapeDtypeStruct(s0.shape, F32)],
        scratch_shapes=[pltpu.VMEM((GLA_HEADS, GLA_HEAD_V, GLA_HEAD_K), F32), pltpu.VMEM((tm, GLA_DK), F32),
                        pltpu.VMEM((c, GLA_HEAD_K), F32), pltpu.VMEM((c, GLA_HEAD_K), F32),
                        pltpu.VMEM((c, c), F32)],
        compiler_params=_cparams("parallel", "arbitrary"),
        name="gla_in_fwd",
    )(x, sh, scl, gpre, w, w_lr, wg[0], bg[0], wg[1], bg[1], s0)


def _gla_scores_exact(q, k, cum, reverse, kf_s, cm_s, sc_s):
    c = q.shape[0]
    kf_s[...] = k
    cm_s[...] = cum
    sc_s[...] = jnp.zeros_like(sc_s)
    ridx = lax.broadcasted_iota(jnp.int32, (c, 1), 0)
    cidx = lax.broadcasted_iota(jnp.int32, (c, c), 1)

    def column(j, carry):
        kj = kf_s[pl.ds(j, 1), :]
        cj = cm_s[pl.ds(j, 1), :]
        live = (ridx <= j) if reverse else (ridx >= j)
        dec = jnp.exp2(jnp.where(live, jnp.minimum(cum - cj, 0.0), -1e30))
        sj = jnp.sum(q * kj * dec, axis=-1, keepdims=True)
        sc_s[...] += jnp.where(cidx == j, sj, 0.0)
        return carry

    lax.fori_loop(0, c, column, 0)
    return sc_s[...]


def _gla_chunks(q_ref, k_ref, v_ref, cum_ref, s_ref, put_o, kf_s, cm_s, sc_s, *, reverse, factorised):
    tb = q_ref.shape[0]
    c = GLA_CHUNK
    row = lax.broadcasted_iota(jnp.int32, (c, c), 0)
    col = lax.broadcasted_iota(jnp.int32, (c, c), 1)
    keep = (row <= col) if reverse else (row >= col)
    chunks = range(tb // c)
    for i in (reversed(chunks) if reverse else chunks):
        rows = slice(i * c, (i + 1) * c)
        for hd in range(GLA_HEADS):
            kc = slice(hd * GLA_HEAD_K, (hd + 1) * GLA_HEAD_K)
            vc = slice(hd * GLA_HEAD_V, (hd + 1) * GLA_HEAD_V)
            cum = cum_ref[rows, kc]
            last = cum[0:1] if reverse else cum[c - 1:c]
            q = q_ref[rows, kc].astype(F32)
            k = k_ref[rows, kc].astype(F32)
            v = v_ref[rows, vc]
            qe = (q * jnp.exp2(cum)).astype(BF16)
            kl = (k * jnp.exp2(last - cum)).astype(BF16)
            st = s_ref[hd]
            if factorised:
                ke = (k * jnp.exp2(-cum)).astype(BF16)
                scores = lax.dot_general(qe, ke, (((1,), (1,)), ((), ())), preferred_element_type=F32)
                scores = jnp.where(keep, scores, 0.0)
            else:
                scores = _gla_scores_exact(q, k, cum, reverse, kf_s, cm_s, sc_s)
            o = lax.dot_general(qe, st.astype(BF16), (((1,), (1,)), ((), ())),
                                preferred_element_type=F32)
            o = o + _dot(scores.astype(BF16), v)
            put_o(rows, vc, o)
            upd = lax.dot_general(v, kl, (((0,), (0,)), ((), ())), preferred_element_type=F32)
            s_ref[hd] = st * jnp.exp2(last) + upd


def _gla_guarded_scan(q_ref, k_ref, v_ref, cum_ref, s_ref, put_o, kf_s, cm_s, sc_s, *, reverse):
    c = GLA_CHUNK
    ends = [cum_ref[i * c:i * c + 1, :] if reverse else cum_ref[(i + 1) * c - 1:(i + 1) * c, :]
            for i in range(q_ref.shape[0] // c)]
    safe = jnp.min(functools.reduce(jnp.minimum, ends)) >= GLA_FACTORISED_MIN_CUM
    scan = functools.partial(_gla_chunks, q_ref, k_ref, v_ref, cum_ref, s_ref, put_o, kf_s, cm_s, sc_s,
                             reverse=reverse)
    pl.when(safe)(functools.partial(scan, factorised=True))
    pl.when(jnp.logical_not(safe))(functools.partial(scan, factorised=False))


def _gla_bwd_out_kernel(q_ref, k_ref, v_ref, cum_ref, s0_ref, of_ref, g_ref, x_hbm, gt_ref, gn_ref, gpost_ref,
                        wout_ref, y_ref, sfin_ref, s_ref, kf_s, cm_s, sc_s, o_s, x_ring, x_sem):
    n = pl.program_id(1)
    nblk = pl.num_programs(1)
    tb = x_ring.shape[1]
    step = pl.program_id(0) * nblk + n
    n_steps = pl.num_programs(0) * nblk

    def x_copy(u):
        rows = pl.ds((nblk - 1 - u % nblk) * tb, tb)
        return pltpu.make_async_copy(x_hbm.at[u // nblk, rows, :], x_ring.at[u % X_RING_SLOTS],
                                     x_sem.at[u % X_RING_SLOTS])

    @pl.when(step == 0)
    def _():
        for u in range(X_RING_SLOTS - 1):
            x_copy(u).start()

    @pl.when(step + (X_RING_SLOTS - 1) < n_steps)
    def _():
        x_copy(step + (X_RING_SLOTS - 1)).start()

    @pl.when(n == 0)
    def _():
        s_ref[...] = s0_ref[...]

    def put_o(rows, vc, o):
        o_s[rows, vc] = o + of_ref[rows, vc].astype(F32)

    _gla_guarded_scan(q_ref, k_ref, v_ref, cum_ref, s_ref, put_o, kf_s, cm_s, sc_s, reverse=True)

    gn = gn_ref[...]
    parts = []
    for hd in range(GLA_HEADS):
        oh = o_s[:, hd * GLA_HEAD_V:(hd + 1) * GLA_HEAD_V]
        ms = jnp.mean(oh * oh, axis=-1, keepdims=True)
        parts.append(oh * lax.rsqrt(ms + RMS_EPS) * gn)
    on = jnp.concatenate(parts, axis=-1) * g_ref[...].astype(F32)
    y = _dot(on.astype(BF16), wout_ref[...])
    x_copy(step).wait()
    y_ref[...] = _post_norm_residual(x_ring[step % X_RING_SLOTS], y, gpost_ref[...], gt_ref[...])

    @pl.when(n == nblk - 1)
    def _():
        sfin_ref[...] = s_ref[...]


def _gla_bwd_out(q, k, v, cum, s0, o_f, g, x, gt, gn, gpost, w_out, tb):
    bsz, t, d = x.shape
    nblk = t // tb
    assert bsz * nblk >= X_RING_SLOTS - 1, (bsz, nblk)
    c = GLA_CHUNK
    row = lambda wd: pl.BlockSpec((None, tb, wd), lambda b, n: (b, nblk - 1 - n, 0))
    vec = lambda wd: pl.BlockSpec((None, 1, wd), lambda b, n: (b, 0, 0))
    full = lambda a: pl.BlockSpec(a.shape, lambda b, n: (0,) * a.ndim)
    st = pl.BlockSpec((None, GLA_HEADS, GLA_HEAD_V, GLA_HEAD_K), lambda b, n: (b, 0, 0, 0))
    return pl.pallas_call(
        _gla_bwd_out_kernel,
        grid=(bsz, nblk),
        in_specs=[row(GLA_DK), row(GLA_DK), row(GLA_DV), row(GLA_DK), st,
                  row(GLA_DV), row(GLA_DV), pl.BlockSpec(memory_space=pl.ANY), vec(d), full(gn), full(gpost),
                  full(w_out)],
        out_specs=[row(d), st],
        out_shape=[jax.ShapeDtypeStruct((bsz, t, d), F32), jax.ShapeDtypeStruct(s0.shape, F32)],
        scratch_shapes=[pltpu.VMEM((GLA_HEADS, GLA_HEAD_V, GLA_HEAD_K), F32),
                        pltpu.VMEM((c, GLA_HEAD_K), F32), pltpu.VMEM((c, GLA_HEAD_K), F32), pltpu.VMEM((c, c), F32),
                        pltpu.VMEM((tb, GLA_DV), F32),
                        pltpu.VMEM((X_RING_SLOTS, tb, d), F32), pltpu.SemaphoreType.DMA((X_RING_SLOTS,))],
        compiler_params=_cparams("arbitrary", "arbitrary"),
        name="gla_bwd_out",
    )(q, k, v, cum, s0, o_f, g, x, gt, gn, gpost, w_out)


def _gla_layer(x, sh, scl, gt, gpre, gpost, w_in, w_lr, wg, bg, gn, w_out, s0_f, s0_b):
    t = x.shape[1]
    tm, tb = min(GLA_IN_ROWS, t), min(GLA_SCAN_ROWS, t)
    assert t % tm == 0 and t % tb == 0 and tm % GLA_CHUNK == 0 and tb % GLA_CHUNK == 0, (t, tm, tb)
    q, k, v, g, cum_b, o_f, s_f = _gla_in(x, sh, scl, gpre, w_in, w_lr, wg, bg, s0_f, tm)
    x_new, s_b = _gla_bwd_out(q, k, v, cum_b, s0_b, o_f, g, x, gt, gn, gpost, w_out, tb)
    return x_new, s_f, s_b


def _lru_scan_kernel(x_ref, xp_ref, xn_ref, sh_ref, scl_ref, gpre_ref, w_ref, cw_ref, cb_ref,
                     waf_ref, baf_ref, wxf_ref, bxf_ref, lamf_ref,
                     wab_ref, bab_ref, wxb_ref, bxb_ref, lamb_ref, h0_ref,
                     hsum_ref, pb_ref, g_ref, hbfin_ref, pbfin_ref, ffin_ref,
                     zext_s, hf_s, pf_s, hb_s, pbk_s, carry_s, cin_s, *pre_s):
    g = pl.program_id(1)
    ng = pl.num_programs(1)
    r, s, d = x_ref.shape
    rows = r * s
    w = LRU_WIDTH
    halo_l = CONV_LEFT * s
    n_right = CONV_W - 1 - CONV_LEFT

    @pl.when(g == 0)
    def _():
        cin_s[...] = h0_ref[...]

    sh, scl, gpre = sh_ref[...], scl_ref[...], gpre_ref[...]
    hb16 = _pre_norm_modulate(x_ref[...].reshape(rows, d), gpre, sh, scl).astype(BF16)
    xh = jnp.concatenate([xp_ref[...].reshape(halo_l, d), xn_ref[...].reshape(n_right * s, d)], axis=0)
    zh = _dot(_pre_norm_modulate(xh, gpre, sh, scl).astype(BF16), w_ref[:, :w])
    sub = lax.broadcasted_iota(jnp.int32, (s, LRU_BLOCK_W), 0)

    def in_proj(nb):
        cs = slice(nb * LRU_BLOCK_W, (nb + 1) * LRU_BLOCK_W)
        zext_s[halo_l:halo_l + rows, cs] = _dot(hb16, w_ref[:, cs])
        for j in range(CONV_LEFT):
            src = halo_l + (r - CONV_LEFT + j) * s
            inner = pltpu.roll(zext_s[src:src + s, cs], 1, 0)
            edge = jnp.where(g == 0, 0.0, pltpu.roll(zh[j * s:(j + 1) * s, cs], 1, 0))
            zext_s[j * s:(j + 1) * s, cs] = jnp.where(sub == 0, edge, inner)
        for j in range(n_right):
            src = halo_l + j * s
            inner = pltpu.roll(zext_s[src:src + s, cs], s - 1, 0)
            edge = jnp.where(g == ng - 1, 0.0,
                             pltpu.roll(zh[halo_l + j * s:halo_l + (j + 1) * s, cs], s - 1, 0))
            dst = halo_l + rows + j * s
            zext_s[dst:dst + s, cs] = jnp.where(sub == s - 1, edge, inner)

    def gate_matmuls(nb):
        cs = slice(nb * LRU_BLOCK_W, (nb + 1) * LRU_BLOCK_W)
        hz = cb_ref[:, cs]
        for j in range(CONV_W):
            hz = hz + cw_ref[j:j + 1, cs] * zext_s[j * s:j * s + rows, cs]
        hzb = hz.astype(BF16)
        prf_s, pif_s, prb_s, pib_s, hzc_s = pre_s[5 * (nb % 2):5 * (nb % 2) + 5]
        prf_s[...] = _dot(hzb, waf_ref[nb]) + baf_ref[:, cs]
        pif_s[...] = _dot(hzb, wxf_ref[nb]) + bxf_ref[:, cs]
        prb_s[...] = _dot(hzb, wab_ref[nb]) + bab_ref[:, cs]
        pib_s[...] = _dot(hzb, wxb_ref[nb]) + bxb_ref[:, cs]
        hzc_s[...] = hz

    in_proj(0)
    gate_matmuls(0)
    for nb in range(LRU_BLOCKS):
        cs = slice(nb * LRU_BLOCK_W, (nb + 1) * LRU_BLOCK_W)
        if nb + 1 < LRU_BLOCKS:
            in_proj(nb + 1)
            gate_matmuls(nb + 1)
        else:
            g_ref[...] = _dot(hb16, w_ref[:, w:]).astype(BF16)
        prf_s, pif_s, prb_s, pib_s, hzc_s = pre_s[5 * (nb % 2):5 * (nb % 2) + 5]
        c1f = jnp.broadcast_to((-0.5 * LRU_C * LOG2_E) * _softplus(-lamf_ref[:, cs]), (s, LRU_BLOCK_W))
        c1b = jnp.broadcast_to((-0.5 * LRU_C * LOG2_E) * _softplus(-lamb_ref[:, cs]), (s, LRU_BLOCK_W))

        def gate(pr_s, pi_s, c1, rws):
            tr = jnp.tanh(pr_s[rws, :])
            ti = jnp.tanh(pi_s[rws, :])
            hz = hzc_s[rws, :]
            a = jnp.exp2(c1 * tr + c1)
            return a, _sqrt(1.0 - a * a) * (hz * ti + hz)

        hf = hb = jnp.zeros((s, LRU_BLOCK_W), F32)
        pf = pb = jnp.ones((s, LRU_BLOCK_W), F32)
        for t in range(r):
            rf = slice(t * s, (t + 1) * s)
            rb = slice((r - 1 - t) * s, (r - t) * s)
            af, uf = gate(prf_s, pif_s, c1f, rf)
            ab, ub = gate(prb_s, pib_s, c1b, rb)
            hf = af * hf + uf
            pf = af * pf
            hb = ab * hb + ub
            pb = ab * pb
            hf_s[rf, :] = hf
            pf_s[rf, :] = pf
            hb_s[rb, :] = hb
            pbk_s[rb, :] = pb

        cin = cin_s[:, cs]
        for c in range(s):
            carry_s[c:c + 1, :] = cin
            cin = hf[c:c + 1] + pf[c:c + 1] * cin
        cin_s[:, cs] = cin
        carry = carry_s[...][None]
        piece = min(r, 16)
        for i in range(r // piece):
            rws = slice(i * piece * s, (i + 1) * piece * s)
            shape3 = (piece, s, LRU_BLOCK_W)
            hsum = (hf_s[rws, :].reshape(shape3) + pf_s[rws, :].reshape(shape3) * carry
                    + hb_s[rws, :].reshape(shape3))
            hsum_ref[rws, cs] = hsum.reshape(piece * s, LRU_BLOCK_W).astype(BF16)
            pb_ref[rws, cs] = pbk_s[rws, :].astype(BF16)
        hbfin_ref[:, cs] = hb
        pbfin_ref[:, cs] = pb

    ffin_ref[...] = cin_s[...]


def _lru_scan(x5, sh, scl, gpre, w_in, conv_w, conv_b, p_f, p_b, h0_f):
    bsz, r, ng, s, d = x5.shape
    rows = r * s
    w = LRU_WIDTH
    n_right = CONV_W - 1 - CONV_LEFT
    once = pl.Buffered(1)
    cur = pl.BlockSpec((None, r, None, s, d), lambda b, g: (b, 0, g, 0, 0))
    prev = pl.BlockSpec((None, CONV_LEFT, None, s, d),
                        lambda b, g: (b, r // CONV_LEFT - 1, jnp.maximum(g - 1, 0), 0, 0))
    nxt = pl.BlockSpec((None, n_right, None, s, d), lambda b, g: (b, 0, jnp.minimum(g + 1, ng - 1), 0, 0))
    vec = lambda wd: pl.BlockSpec((None, 1, wd), lambda b, g: (b, 0, 0))
    full = lambda a: pl.BlockSpec(a.shape, lambda b, g: (0,) * a.ndim, pipeline_mode=once)
    row = pl.BlockSpec((None, rows, w), lambda b, g: (b, g, 0))
    fin = pl.BlockSpec((None, None, s, w), lambda b, g: (b, g, 0, 0))
    blk = lambda: pltpu.VMEM((rows, LRU_BLOCK_W), F32)
    return pl.pallas_call(
        _lru_scan_kernel,
        grid=(bsz, ng),
        in_specs=[cur, prev, nxt, vec(d), vec(d), full(gpre), full(w_in), full(conv_w), full(conv_b)]
                 + [full(a) for a in p_f] + [full(a) for a in p_b] + [vec(w)],
        out_specs=[row, row, row, fin, fin, vec(w)],
        out_shape=[jax.ShapeDtypeStruct((bsz, ng * rows, w), BF16),
                   jax.ShapeDtypeStruct((bsz, ng * rows, w), BF16),
                   jax.ShapeDtypeStruct((bsz, ng * rows, w), BF16),
                   jax.ShapeDtypeStruct((bsz, ng, s, w), F32),
                   jax.ShapeDtypeStruct((bsz, ng, s, w), F32),
                   jax.ShapeDtypeStruct((bsz, 1, w), F32)],
        scratch_shapes=[pltpu.VMEM((rows + (CONV_W - 1) * s, w), F32)] + [blk() for _ in range(4)]
                       + [pltpu.VMEM((s, LRU_BLOCK_W), F32), pltpu.VMEM((1, w), F32)]
                       + [blk() for _ in range(10)],
        compiler_params=_cparams("parallel", "arbitrary"),
        name="lru_scan",
    )(x5, x5, x5, sh, scl, gpre, w_in, conv_w, conv_b, *p_f, *p_b, h0_f)


def _lru_carry_kernel(h0_ref, hfin_ref, pfin_ref, carry_ref, final_ref):
    c = h0_ref[...]
    for k in range(hfin_ref.shape[0] - 1, -1, -1):
        carry_ref[k:k + 1, :] = c
        c = hfin_ref[k:k + 1, :] + pfin_ref[k:k + 1, :] * c
    final_ref[...] = c


def _lru_carry(h0, hfin, pfin):
    bsz, ng, s, w = hfin.shape
    vec = pl.BlockSpec((None, 1, w), lambda b: (b, 0, 0))
    runs = pl.BlockSpec((None, ng * s, w), lambda b: (b, 0, 0))
    carry, final = pl.pallas_call(
        _lru_carry_kernel,
        grid=(bsz,),
        in_specs=[vec, runs, runs],
        out_specs=[runs, vec],
        out_shape=[jax.ShapeDtypeStruct((bsz, ng * s, w), F32), jax.ShapeDtypeStruct((bsz, 1, w), F32)],
        compiler_params=_cparams("parallel"),
        name="lru_carry",
    )(h0, hfin.reshape(bsz, ng * s, w), pfin.reshape(bsz, ng * s, w))
    return carry.reshape(bsz, ng, s, w), final


def _lru_out_kernel(hsum_ref, pb_ref, cb_ref, g_ref, x_ref, gt_ref, gpost_ref, w_ref, o_ref):
    r, s, d = x_ref.shape
    w = hsum_ref.shape[-1]
    h = hsum_ref[...].astype(F32).reshape(r, s, w) + pb_ref[...].astype(F32).reshape(r, s, w) * cb_ref[...][None]
    y = _dot((h.reshape(r * s, w) * _silu(g_ref[...].astype(F32))).astype(BF16), w_ref[...])
    x = x_ref[...].reshape(r * s, d)
    o_ref[...] = _post_norm_residual(x, y, gpost_ref[...], gt_ref[...]).reshape(r, s, d)


def _lru_out(hsum, pb, cb, g, x5, gt, gpost, w, rb):
    bsz, r, ng, s, d = x5.shape
    nr = r // rb
    tm = rb * s
    width = hsum.shape[-1]
    xio = pl.BlockSpec((None, rb, None, s, d), lambda b, g, i: (b, i, g, 0, 0))
    row = pl.BlockSpec((None, tm, width), lambda b, g, i: (b, g * nr + i, 0))
    runs = pl.BlockSpec((None, None, s, width), lambda b, g, i: (b, g, 0, 0))
    vec = pl.BlockSpec((None, 1, d), lambda b, g, i: (b, 0, 0))
    full = lambda a: pl.BlockSpec(a.shape, lambda b, g, i: (0,) * a.ndim)
    return pl.pallas_call(
        _lru_out_kernel,
        grid=(bsz, ng, nr),
        in_specs=[row, row, runs, row, xio, vec, full(gpost), full(w)],
        out_specs=xio,
        out_shape=jax.ShapeDtypeStruct(x5.shape, F32),
        compiler_params=_cparams("parallel", "parallel", "parallel"),
        name="lru_out",
    )(hsum, pb, cb, g, x5, gt, gpost, w)


def kernel(x, c, ctx, c_ctx, ada_w, ada_b, norm_pre, norm_post, gla_w_in, gla_wg_f, gla_bg_f, gla_wg_b, gla_bg_b, gla_norm, gla_w_out, lru_w_in, lru_conv_w, lru_conv_b, lru_wa_f, lru_ba_f, lru_wx_f, lru_bx_f, lru_lam_f, lru_wa_b, lru_ba_b, lru_wx_b, lru_bx_b, lru_lam_b, lru_w_out):
    bsz, seq, d = x.shape
    ctx_len = ctx.shape[1]
    rows = seq // GRID_W
    assert seq % GRID_W == 0 and GRID_W % N_SEG == 0 and ctx_len % N_SEG == 0, (seq, ctx_len)
    assert rows % CONV_LEFT == 0 and (ctx_len // N_SEG) % CONV_LEFT == 0 and bsz < SUBLANES, (rows, ctx_len, bsz)

    cvec = jnp.concatenate([c, c_ctx[None], jnp.zeros((SUBLANES - bsz - 1, d), F32)], axis=0)
    mod = _ada_modulation(cvec, ada_w, ada_b)

    def mods(i):
        lat = [mod[i, :bsz, None, j * d:(j + 1) * d] for j in range(3)]
        con = [jnp.broadcast_to(mod[i, bsz, None, None, j * d:(j + 1) * d], (bsz, 1, d)) for j in range(3)]
        return lat, con

    (sh, scl, gt), (sh_c, scl_c, gt_c) = mods(0)
    gpre, gpost = norm_pre[0][None], norm_post[0][None]
    rk = GLA_GATE_RANK
    n_main = 2 * GLA_DK + 2 * GLA_DV
    w_in = gla_w_in[0][:, :n_main].astype(BF16)
    w_lr = jnp.pad(gla_w_in[0][:, n_main:], ((0, 0), (0, LANES - 2 * rk))).astype(BF16)
    wg = (jnp.pad(gla_wg_f[0], ((0, LANES - rk), (0, 0))).astype(BF16),
          jnp.pad(gla_wg_b[0], ((rk, LANES - 2 * rk), (0, 0))).astype(BF16))
    bg = (gla_bg_f[0][None], gla_bg_b[0][None])
    gla_args = (gpre, gpost, w_in, w_lr, wg, bg, gla_norm[0][None], gla_w_out[0].astype(BF16))
    s0 = jnp.zeros((bsz, GLA_HEADS, GLA_HEAD_V, GLA_HEAD_K), F32)
    ctx, s_f, s_b = _gla_layer(ctx, sh_c, scl_c, gt_c, *gla_args, s0, s0)
    x, _, _ = _gla_layer(x, sh, scl, gt, *gla_args, s_f, s_b)

    (sh, scl, gt), (sh_c, scl_c, _) = mods(1)
    gpre, gpost = norm_pre[1][None], norm_post[1][None]
    vec = lambda a: a[None]
    gate = lambda wgt, bias: (wgt.astype(BF16), vec(0.5 * bias))
    p_f = (*gate(lru_wa_f[0], lru_ba_f[0]), *gate(lru_wx_f[0], lru_bx_f[0]), vec(lru_lam_f[0]))
    p_b = (*gate(lru_wa_b[0], lru_ba_b[0]), *gate(lru_wx_b[0], lru_bx_b[0]), vec(lru_lam_b[0]))
    scan_args = (gpre, lru_w_in[0].astype(BF16), 0.5 * lru_conv_w[0], vec(0.5 * lru_conv_b[0]), p_f, p_b)
    h0 = jnp.zeros((bsz, 1, LRU_WIDTH), F32)
    cstep = ctx_len // N_SEG
    ctx5 = ctx.reshape(bsz, N_SEG, cstep, d).transpose(0, 2, 1, 3).reshape(bsz, cstep, 1, N_SEG, d)
    _, _, _, hbfin, pbfin, s_f = _lru_scan(ctx5, sh_c, scl_c, *scan_args, h0)
    _, s_b = _lru_carry(h0, hbfin, pbfin)
    x5 = x.reshape(bsz, rows, GRID_W // N_SEG, N_SEG, d)
    hsum, pb, g, hbfin, pbfin, _ = _lru_scan(x5, sh, scl, *scan_args, s_f)
    cb, _ = _lru_carry(s_b, hbfin, pbfin)
    out5 = _lru_out(hsum, pb, cb, g, x5, gt, gpost, lru_w_out[0].astype(BF16), rb=min(LRU_OUT_STEPS, rows))
    return out5.reshape(bsz, seq, d)
```

```python
import functools

import jax
import jax.numpy as jnp
from jax import lax
from jax.experimental import pallas as pl
from jax.experimental.pallas import tpu as pltpu

F32 = jnp.float32
BF16 = jnp.bfloat16
SUBLANES, LANES = 8, 128

RMS_EPS = 1e-6
LOG2_E = 1.4426950408889634
GRID_W = 64
GLA_HEADS = 4
GLA_HEAD_K = 128
GLA_HEAD_V = 256
GLA_DK = GLA_HEADS * GLA_HEAD_K
GLA_DV = GLA_HEADS * GLA_HEAD_V
GLA_GATE_RANK = 16
GLA_GATE_NORM = 16.0
GLA_CHUNK = 256
GLA_FACTORISED_MIN_CUM = -86.0
GLA_SUM_PIECE = 128
GLA_IN_ROWS = 1024
GLA_SCAN_ROWS = 1024
X_RING_SLOTS = 3
LRU_BLOCKS = 5
LRU_BLOCK_W = 256
LRU_WIDTH = LRU_BLOCKS * LRU_BLOCK_W
LRU_C = 8.0
CONV_W = 4
CONV_LEFT = CONV_W // 2
N_SEG = SUBLANES
LRU_OUT_STEPS = 128

VMEM_LIMIT_BYTES = 56 * 1024 * 1024


def _cparams(*sem):
    return pltpu.CompilerParams(dimension_semantics=sem, vmem_limit_bytes=VMEM_LIMIT_BYTES)


def _silu(x):
    hx = 0.5 * x
    return hx * jnp.tanh(hx) + hx


def _sqrt(x):
    return x * lax.rsqrt(jnp.maximum(x, 1e-30))


def _log_sigmoid(x):
    return jnp.minimum(x, 0.0) - jnp.log(1.0 + jnp.exp(-jnp.abs(x)))


def _softplus(x):
    return jnp.maximum(x, 0.0) + jnp.log1p(jnp.exp(-jnp.abs(x)))


def _split_bf16(x):
    hi = x.astype(BF16)
    lo = (x - hi.astype(F32)).astype(BF16)
    return hi, lo


def _dot(a, b):
    return jnp.dot(a, b, preferred_element_type=F32)


def _dot_x3(a, b):
    a_hi, a_lo = _split_bf16(a)
    b_hi, b_lo = _split_bf16(b)
    return _dot(a_hi, b_hi) + (_dot(a_hi, b_lo) + _dot(a_lo, b_hi))


def _pre_norm_modulate(x, gpre, sh, scl):
    ms = jnp.mean(x * x, axis=-1, keepdims=True)
    return (x * lax.rsqrt(ms + RMS_EPS)) * (gpre * (1.0 + scl)) + sh


def _post_norm_residual(x, y, gpost, gt):
    ms = jnp.mean(y * y, axis=-1, keepdims=True)
    return x + (y * lax.rsqrt(ms + RMS_EPS)) * (gt * gpost)


def _ada_kernel(c_ref, w_ref, b_ref, o_ref):
    sc = _silu(c_ref[...])
    o_ref[...] = _dot_x3(sc, w_ref[...]) + b_ref[...]


def _ada_modulation(cvec, ada_w, ada_b):
    depth, d, n3 = ada_w.shape
    tn = 1024
    return pl.pallas_call(
        _ada_kernel,
        grid=(depth, n3 // tn),
        in_specs=[pl.BlockSpec((SUBLANES, d), lambda i, j: (0, 0)),
                  pl.BlockSpec((None, d, tn), lambda i, j: (i, 0, j)),
                  pl.BlockSpec((None, 1, tn), lambda i, j: (i, 0, j))],
        out_specs=pl.BlockSpec((None, SUBLANES, tn), lambda i, j: (i, 0, j)),
        out_shape=jax.ShapeDtypeStruct((depth, SUBLANES, n3), F32),
        compiler_params=_cparams("parallel", "parallel"),
        name="ada_modulation",
    )(cvec, ada_w, ada_b.reshape(depth, 1, n3))


def _gla_log_decay_sums(lr, wg_ref, bg_ref, cum_ref, reverse):
    c, p = GLA_CHUNK, GLA_SUM_PIECE
    row = lax.broadcasted_iota(jnp.int32, (p, 2 * p), 0)
    col = lax.broadcasted_iota(jnp.int32, (p, 2 * p), 1) % p
    tri = ((row <= col) if reverse else (row >= col)).astype(BF16)
    z = _dot(lr, wg_ref[...]) + bg_ref[...]
    log_a = _log_sigmoid(z) * (LOG2_E / GLA_GATE_NORM)
    hi, lo = _split_bf16(log_a)
    for i in range(lr.shape[0] // c):
        carry = None
        pieces = range(c // p)
        for j in (reversed(pieces) if reverse else pieces):
            rows = slice(i * c + j * p, i * c + (j + 1) * p)
            s = _dot(tri, jnp.concatenate([hi[rows], lo[rows]], axis=0))
            if carry is not None:
                s = s + carry
            cum_ref[rows, :] = s
            carry = s[0:1] if reverse else s[p - 1:p]


def _gla_in_kernel(x_ref, sh_ref, scl_ref, gpre_ref, w_ref, wlr_ref, wgf_ref, bgf_ref, wgb_ref, bgb_ref, s0_ref,
                   q_ref, k_ref, v_ref, g_ref, cumb_ref, of_ref, sfin_ref,
                   s_ref, cumf_ref, kf_s, cm_s, sc_s):
    n = pl.program_id(1)

    @pl.when(n == 0)
    def _():
        s_ref[...] = s0_ref[...]

    hb16 = _pre_norm_modulate(x_ref[...], gpre_ref[...], sh_ref[...], scl_ref[...]).astype(BF16)
    lr = _dot(hb16, wlr_ref[...]).astype(BF16)
    proj = _dot(hb16, w_ref[...])
    _gla_log_decay_sums(lr, wgf_ref, bgf_ref, cumf_ref, reverse=False)
    _gla_log_decay_sums(lr, wgb_ref, bgb_ref, cumb_ref, reverse=True)
    q_ref[...] = (proj[:, :GLA_DK] * (GLA_HEAD_K ** -0.5)).astype(BF16)
    k_ref[...] = proj[:, GLA_DK:2 * GLA_DK].astype(BF16)
    v_ref[...] = proj[:, 2 * GLA_DK:2 * GLA_DK + GLA_DV].astype(BF16)
    g_ref[...] = _silu(proj[:, 2 * GLA_DK + GLA_DV:]).astype(BF16)

    def put_o(rows, vc, o):
        of_ref[rows, vc] = o.astype(BF16)

    _gla_guarded_scan(q_ref, k_ref, v_ref, cumf_ref, s_ref, put_o, kf_s, cm_s, sc_s, reverse=False)

    @pl.when(n == pl.num_programs(1) - 1)
    def _():
        sfin_ref[...] = s_ref[...]


def _gla_in(x, sh, scl, gpre, w, w_lr, wg, bg, s0, tm):
    bsz, t, d = x.shape
    c = GLA_CHUNK
    row = lambda wd: pl.BlockSpec((None, tm, wd), lambda b, i: (b, i, 0))
    vec = lambda wd: pl.BlockSpec((None, 1, wd), lambda b, i: (b, 0, 0))
    full = lambda a: pl.BlockSpec(a.shape, lambda b, i: (0,) * a.ndim)
    st = pl.BlockSpec((None, GLA_HEADS, GLA_HEAD_V, GLA_HEAD_K), lambda b, i: (b, 0, 0, 0))
    return pl.pallas_call(
        _gla_in_kernel,
        grid=(bsz, t // tm),
        in_specs=[row(d), vec(d), vec(d), full(gpre), full(w), full(w_lr),
                  full(wg[0]), full(bg[0]), full(wg[1]), full(bg[1]), st],
        out_specs=[row(GLA_DK), row(GLA_DK), row(GLA_DV), row(GLA_DV), row(GLA_DK), row(GLA_DV), st],
        out_shape=[jax.ShapeDtypeStruct((bsz, t, GLA_DK), BF16),
                   jax.ShapeDtypeStruct((bsz, t, GLA_DK), BF16),
                   jax.ShapeDtypeStruct((bsz, t, GLA_DV), BF16),
                   jax.ShapeDtypeStruct((bsz, t, GLA_DV), BF16),
                   jax.ShapeDtypeStruct((bsz, t, GLA_DK), F32),
                   jax.ShapeDtypeStruct((bsz, t, GLA_DV), BF16),
                   jax.ShapeDtypeStruct(s0.shape, F32)],
        scratch_shapes=[pltpu.VMEM((GLA_HEADS, GLA_HEAD_V, GLA_HEAD_K), F32), pltpu.VMEM((tm, GLA_DK), F32),
                        pltpu.VMEM((c, GLA_HEAD_K), F32), pltpu.VMEM((c, GLA_HEAD_K), F32),
                        pltpu.VMEM((c, c), F32)],
        compiler_params=_cparams("parallel", "arbitrary"),
        name="gla_in_fwd",
    )(x, sh, scl, gpre, w, w_lr, wg[0], bg[0], wg[1], bg[1], s0)


def _gla_scores_exact(q, k, cum, reverse, kf_s, cm_s, sc_s):
    c = q.shape[0]
    kf_s[...] = k
    cm_s[...] = cum
    sc_s[...] = jnp.zeros_like(sc_s)
    ridx = lax.broadcasted_iota(jnp.int32, (c, 1), 0)
    cidx = lax.broadcasted_iota(jnp.int32, (c, c), 1)

    def column(j, carry):
        kj = kf_s[pl.ds(j, 1), :]
        cj = cm_s[pl.ds(j, 1), :]
        live = (ridx <= j) if reverse else (ridx >= j)
        dec = jnp.exp2(jnp.where(live, jnp.minimum(cum - cj, 0.0), -1e30))
        sj = jnp.sum(q * kj * dec, axis=-1, keepdims=True)
        sc_s[...] += jnp.where(cidx == j, sj, 0.0)
        return carry

    lax.fori_loop(0, c, column, 0)
    return sc_s[...]


def _gla_chunks(q_ref, k_ref, v_ref, cum_ref, s_ref, put_o, kf_s, cm_s, sc_s, *, reverse, factorised):
    tb = q_ref.shape[0]
    c = GLA_CHUNK
    row = lax.broadcasted_iota(jnp.int32, (c, c), 0)
    col = lax.broadcasted_iota(jnp.int32, (c, c), 1)
    keep = (row <= col) if reverse else (row >= col)
    chunks = range(tb // c)
    for i in (reversed(chunks) if reverse else chunks):
        rows = slice(i * c, (i + 1) * c)
        for hd in range(GLA_HEADS):
            kc = slice(hd * GLA_HEAD_K, (hd + 1) * GLA_HEAD_K)
            vc = slice(hd * GLA_HEAD_V, (hd + 1) * GLA_HEAD_V)
            cum = cum_ref[rows, kc]
            last = cum[0:1] if reverse else cum[c - 1:c]
            q = q_ref[rows, kc].astype(F32)
            k = k_ref[rows, kc].astype(F32)
            v = v_ref[rows, vc]
            qe = (q * jnp.exp2(cum)).astype(BF16)
            kl = (k * jnp.exp2(last - cum)).astype(BF16)
            st = s_ref[hd]
            if factorised:
                ke = (k * jnp.exp2(-cum)).astype(BF16)
                scores = lax.dot_general(qe, ke, (((1,), (1,)), ((), ())), preferred_element_type=F32)
                scores = jnp.where(keep, scores, 0.0)
            else:
                scores = _gla_scores_exact(q, k, cum, reverse, kf_s, cm_s, sc_s)
            o = lax.dot_general(qe, st.astype(BF16), (((1,), (1,)), ((), ())),
                                preferred_element_type=F32)
            o = o + _dot(scores.astype(BF16), v)
            put_o(rows, vc, o)
            upd = lax.dot_general(v, kl, (((0,), (0,)), ((), ())), preferred_element_type=F32)
            s_ref[hd] = st * jnp.exp2(last) + upd


def _gla_guarded_scan(q_ref, k_ref, v_ref, cum_ref, s_ref, put_o, kf_s, cm_s, sc_s, *, reverse):
    c = GLA_CHUNK
    ends = [cum_ref[i * c:i * c + 1, :] if reverse else cum_ref[(i + 1) * c - 1:(i + 1) * c, :]
            for i in range(q_ref.shape[0] // c)]
    safe = jnp.min(functools.reduce(jnp.minimum, ends)) >= GLA_FACTORISED_MIN_CUM
    scan = functools.partial(_gla_chunks, q_ref, k_ref, v_ref, cum_ref, s_ref, put_o, kf_s, cm_s, sc_s,
                             reverse=reverse)
    pl.when(safe)(functools.partial(scan, factorised=True))
    pl.when(jnp.logical_not(safe))(functools.partial(scan, factorised=False))


def _gla_bwd_out_kernel(q_ref, k_ref, v_ref, cum_ref, s0_ref, of_ref, g_ref, x_hbm, gt_ref, gn_ref, gpost_ref,
                        wout_ref, y_ref, sfin_ref, s_ref, kf_s, cm_s, sc_s, o_s, x_ring, x_sem):
    n = pl.program_id(1)
    nblk = pl.num_programs(1)
    tb = x_ring.shape[1]
    step = pl.program_id(0) * nblk + n
    n_steps = pl.num_programs(0) * nblk

    def x_copy(u):
        rows = pl.ds((nblk - 1 - u % nblk) * tb, tb)
        return pltpu.make_async_copy(x_hbm.at[u // nblk, rows, :], x_ring.at[u % X_RING_SLOTS],
                                     x_sem.at[u % X_RING_SLOTS])

    @pl.when(step == 0)
    def _():
        for u in range(X_RING_SLOTS - 1):
            x_copy(u).start()

    @pl.when(step + (X_RING_SLOTS - 1) < n_steps)
    def _():
        x_copy(step + (X_RING_SLOTS - 1)).start()

    @pl.when(n == 0)
    def _():
        s_ref[...] = s0_ref[...]

    def put_o(rows, vc, o):
        o_s[rows, vc] = o + of_ref[rows, vc].astype(F32)

    _gla_guarded_scan(q_ref, k_ref, v_ref, cum_ref, s_ref, put_o, kf_s, cm_s, sc_s, reverse=True)

    gn = gn_ref[...]
    parts = []
    for hd in range(GLA_HEADS):
        oh = o_s[:, hd * GLA_HEAD_V:(hd + 1) * GLA_HEAD_V]
        ms = jnp.mean(oh * oh, axis=-1, keepdims=True)
        parts.append(oh * lax.rsqrt(ms + RMS_EPS) * gn)
    on = jnp.concatenate(parts, axis=-1) * g_ref[...].astype(F32)
    y = _dot(on.astype(BF16), wout_ref[...])
    x_copy(step).wait()
    y_ref[...] = _post_norm_residual(x_ring[step % X_RING_SLOTS], y, gpost_ref[...], gt_ref[...])

    @pl.when(n == nblk - 1)
    def _():
        sfin_ref[...] = s_ref[...]


def _gla_bwd_out(q, k, v, cum, s0, o_f, g, x, gt, gn, gpost, w_out, tb):
    bsz, t, d = x.shape
    nblk = t // tb
    assert bsz * nblk >= X_RING_SLOTS - 1, (bsz, nblk)
    c = GLA_CHUNK
    row = lambda wd: pl.BlockSpec((None, tb, wd), lambda b, n: (b, nblk - 1 - n, 0))
    vec = lambda wd: pl.BlockSpec((None, 1, wd), lambda b, n: (b, 0, 0))
    full = lambda a: pl.BlockSpec(a.shape, lambda b, n: (0,) * a.ndim)
    st = pl.BlockSpec((None, GLA_HEADS, GLA_HEAD_V, GLA_HEAD_K), lambda b, n: (b, 0, 0, 0))
    return pl.pallas_call(
        _gla_bwd_out_kernel,
        grid=(bsz, nblk),
        in_specs=[row(GLA_DK), row(GLA_DK), row(GLA_DV), row(GLA_DK), st,
                  row(GLA_DV), row(GLA_DV), pl.BlockSpec(memory_space=pl.ANY), vec(d), full(gn), full(gpost),
                  full(w_out)],
        out_specs=[row(d), st],
        out_shape=[jax.ShapeDtypeStruct((bsz, t, d), F32), jax.ShapeDtypeStruct(s0.shape, F32)],
        scratch_shapes=[pltpu.VMEM((GLA_HEADS, GLA_HEAD_V, GLA_HEAD_K), F32),
                        pltpu.VMEM((c, GLA_HEAD_K), F32), pltpu.VMEM((c, GLA_HEAD_K), F32), pltpu.VMEM((c, c), F32),
                        pltpu.VMEM((tb, GLA_DV), F32),
                        pltpu.VMEM((X_RING_SLOTS, tb, d), F32), pltpu.SemaphoreType.DMA((X_RING_SLOTS,))],
        compiler_params=_cparams("arbitrary", "arbitrary"),
        name="gla_bwd_out",
    )(q, k, v, cum, s0, o_f, g, x, gt, gn, gpost, w_out)


def _gla_layer(x, sh, scl, gt, gpre, gpost, w_in, w_lr, wg, bg, gn, w_out, s0_f, s0_b):
    t = x.shape[1]
    tm, tb = min(GLA_IN_ROWS, t), min(GLA_SCAN_ROWS, t)
    assert t % tm == 0 and t % tb == 0 and tm % GLA_CHUNK == 0 and tb % GLA_CHUNK == 0, (t, tm, tb)
    q, k, v, g, cum_b, o_f, s_f = _gla_in(x, sh, scl, gpre, w_in, w_lr, wg, bg, s0_f, tm)
    x_new, s_b = _gla_bwd_out(q, k, v, cum_b, s0_b, o_f, g, x, gt, gn, gpost, w_out, tb)
    return x_new, s_f, s_b


def _lru_scan_kernel(x_ref, xp_ref, xn_ref, sh_ref, scl_ref, gpre_ref, w_ref, cw_ref, cb_ref,
                     waf_ref, baf_ref, wxf_ref, bxf_ref, lamf_ref,
                     wab_ref, bab_ref, wxb_ref, bxb_ref, lamb_ref, h0_ref,
                     hsum_ref, pb_ref, g_ref, hbfin_ref, pbfin_ref, ffin_ref,
                     zext_s, hf_s, pf_s, hb_s, pbk_s, carry_s, cin_s, *pre_s):
    g = pl.program_id(1)
    ng = pl.num_programs(1)
    r, s, d = x_ref.shape
    rows = r * s
    w = LRU_WIDTH
    halo_l = CONV_LEFT * s
    n_right = CONV_W - 1 - CONV_LEFT

    @pl.when(g == 0)
    def _():
        cin_s[...] = h0_ref[...]

    sh, scl, gpre = sh_ref[...], scl_ref[...], gpre_ref[...]
    hb16 = _pre_norm_modulate(x_ref[...].reshape(rows, d), gpre, sh, scl).astype(BF16)
    xh = jnp.concatenate([xp_ref[...].reshape(halo_l, d), xn_ref[...].reshape(n_right * s, d)], axis=0)
    zh = _dot(_pre_norm_modulate(xh, gpre, sh, scl).astype(BF16), w_ref[:, :w])
    sub = lax.broadcasted_iota(jnp.int32, (s, LRU_BLOCK_W), 0)

    def in_proj(nb):
        cs = slice(nb * LRU_BLOCK_W, (nb + 1) * LRU_BLOCK_W)
        zext_s[halo_l:halo_l + rows, cs] = _dot(hb16, w_ref[:, cs])
        for j in range(CONV_LEFT):
            src = halo_l + (r - CONV_LEFT + j) * s
            inner = pltpu.roll(zext_s[src:src + s, cs], 1, 0)
            edge = jnp.where(g == 0, 0.0, pltpu.roll(zh[j * s:(j + 1) * s, cs], 1, 0))
            zext_s[j * s:(j + 1) * s, cs] = jnp.where(sub == 0, edge, inner)
        for j in range(n_right):
            src = halo_l + j * s
            inner = pltpu.roll(zext_s[src:src + s, cs], s - 1, 0)
            edge = jnp.where(g == ng - 1, 0.0,
                             pltpu.roll(zh[halo_l + j * s:halo_l + (j + 1) * s, cs], s - 1, 0))
            dst = halo_l + rows + j * s
            zext_s[dst:dst + s, cs] = jnp.where(sub == s - 1, edge, inner)

    def gate_matmuls(nb):
        cs = slice(nb * LRU_BLOCK_W, (nb + 1) * LRU_BLOCK_W)
        hz = cb_ref[:, cs]
        for j in range(CONV_W):
            hz = hz + cw_ref[j:j + 1, cs] * zext_s[j * s:j * s + rows, cs]
        hzb = hz.astype(BF16)
        prf_s, pif_s, prb_s, pib_s, hzc_s = pre_s[5 * (nb % 2):5 * (nb % 2) + 5]
        prf_s[...] = _dot(hzb, waf_ref[nb]) + baf_ref[:, cs]
        pif_s[...] = _dot(hzb, wxf_ref[nb]) + bxf_ref[:, cs]
        prb_s[...] = _dot(hzb, wab_ref[nb]) + bab_ref[:, cs]
        pib_s[...] = _dot(hzb, wxb_ref[nb]) + bxb_ref[:, cs]
        hzc_s[...] = hz

    in_proj(0)
    gate_matmuls(0)
    for nb in range(LRU_BLOCKS):
        cs = slice(nb * LRU_BLOCK_W, (nb + 1) * LRU_BLOCK_W)
        if nb + 1 < LRU_BLOCKS:
            in_proj(nb + 1)
            gate_matmuls(nb + 1)
        else:
            g_ref[...] = _dot(hb16, w_ref[:, w:]).astype(BF16)
        prf_s, pif_s, prb_s, pib_s, hzc_s = pre_s[5 * (nb % 2):5 * (nb % 2) + 5]
        c1f = jnp.broadcast_to((-0.5 * LRU_C * LOG2_E) * _softplus(-lamf_ref[:, cs]), (s, LRU_BLOCK_W))
        c1b = jnp.broadcast_to((-0.5 * LRU_C * LOG2_E) * _softplus(-lamb_ref[:, cs]), (s, LRU_BLOCK_W))

        def gate(pr_s, pi_s, c1, rws):
            tr = jnp.tanh(pr_s[rws, :])
            ti = jnp.tanh(pi_s[rws, :])
            hz = hzc_s[rws, :]
            a = jnp.exp2(c1 * tr + c1)
            return a, _sqrt(1.0 - a * a) * (hz * ti + hz)

        hf = hb = jnp.zeros((s, LRU_BLOCK_W), F32)
        pf = pb = jnp.ones((s, LRU_BLOCK_W), F32)
        for t in range(r):
            rf = slice(t * s, (t + 1) * s)
            rb = slice((r - 1 - t) * s, (r - t) * s)
            af, uf = gate(prf_s, pif_s, c1f, rf)
            ab, ub = gate(prb_s, pib_s, c1b, rb)
            hf = af * hf + uf
            pf = af * pf
            hb = ab * hb + ub
            pb = ab * pb
            hf_s[rf, :] = hf
            pf_s[rf, :] = pf
            hb_s[rb, :] = hb
            pbk_s[rb, :] = pb

        cin = cin_s[:, cs]
        for c in range(s):
            carry_s[c:c + 1, :] = cin
            cin = hf[c:c + 1] + pf[c:c + 1] * cin
        cin_s[:, cs] = cin
        carry = carry_s[...][None]
        piece = min(r, 16)
        for i in range(r // piece):
            rws = slice(i * piece * s, (i + 1) * piece * s)
            shape3 = (piece, s, LRU_BLOCK_W)
            hsum = (hf_s[rws, :].reshape(shape3) + pf_s[rws, :].reshape(shape3) * carry
                    + hb_s[rws, :].reshape(shape3))
            hsum_ref[rws, cs] = hsum.reshape(piece * s, LRU_BLOCK_W).astype(BF16)
            pb_ref[rws, cs] = pbk_s[rws, :].astype(BF16)
        hbfin_ref[:, cs] = hb
        pbfin_ref[:, cs] = pb

    ffin_ref[...] = cin_s[...]


def _lru_scan(x5, sh, scl, gpre, w_in, conv_w, conv_b, p_f, p_b, h0_f):
    bsz, r, ng, s, d = x5.shape
    rows = r * s
    w = LRU_WIDTH
    n_right = CONV_W - 1 - CONV_LEFT
    once = pl.Buffered(1)
    cur = pl.BlockSpec((None, r, None, s, d), lambda b, g: (b, 0, g, 0, 0))
    prev = pl.BlockSpec((None, CONV_LEFT, None, s, d),
                        lambda b, g: (b, r // CONV_LEFT - 1, jnp.maximum(g - 1, 0), 0, 0))
    nxt = pl.BlockSpec((None, n_right, None, s, d), lambda b, g: (b, 0, jnp.minimum(g + 1, ng - 1), 0, 0))
    vec = lambda wd: pl.BlockSpec((None, 1, wd), lambda b, g: (b, 0, 0))
    full = lambda a: pl.BlockSpec(a.shape, lambda b, g: (0,) * a.ndim, pipeline_mode=once)
    row = pl.BlockSpec((None, rows, w), lambda b, g: (b, g, 0))
    fin = pl.BlockSpec((None, None, s, w), lambda b, g: (b, g, 0, 0))
    blk = lambda: pltpu.VMEM((rows, LRU_BLOCK_W), F32)
    return pl.pallas_call(
        _lru_scan_kernel,
        grid=(bsz, ng),
        in_specs=[cur, prev, nxt, vec(d), vec(d), full(gpre), full(w_in), full(conv_w), full(conv_b)]
                 + [full(a) for a in p_f] + [full(a) for a in p_b] + [vec(w)],
        out_specs=[row, row, row, fin, fin, vec(w)],
        out_shape=[jax.ShapeDtypeStruct((bsz, ng * rows, w), BF16),
                   jax.ShapeDtypeStruct((bsz, ng * rows, w), BF16),
                   jax.ShapeDtypeStruct((bsz, ng * rows, w), BF16),
                   jax.ShapeDtypeStruct((bsz, ng, s, w), F32),
                   jax.ShapeDtypeStruct((bsz, ng, s, w), F32),
                   jax.ShapeDtypeStruct((bsz, 1, w), F32)],
        scratch_shapes=[pltpu.VMEM((rows + (CONV_W - 1) * s, w), F32)] + [blk() for _ in range(4)]
                       + [pltpu.VMEM((s, LRU_BLOCK_W), F32), pltpu.VMEM((1, w), F32)]
                       + [blk() for _ in range(10)],
        compiler_params=_cparams("parallel", "arbitrary"),
        name="lru_scan",
    )(x5, x5, x5, sh, scl, gpre, w_in, conv_w, conv_b, *p_f, *p_b, h0_f)


def _lru_carry_kernel(h0_ref, hfin_ref, pfin_ref, carry_ref, final_ref):
    c = h0_ref[...]
    for k in range(hfin_ref.shape[0] - 1, -1, -1):
        carry_ref[k:k + 1, :] = c
        c = hfin_ref[k:k + 1, :] + pfin_ref[k:k + 1, :] * c
    final_ref[...] = c


def _lru_carry(h0, hfin, pfin):
    bsz, ng, s, w = hfin.shape
    vec = pl.BlockSpec((None, 1, w), lambda b: (b, 0, 0))
    runs = pl.BlockSpec((None, ng * s, w), lambda b: (b, 0, 0))
    carry, final = pl.pallas_call(
        _lru_carry_kernel,
        grid=(bsz,),
        in_specs=[vec, runs, runs],
        out_specs=[runs, vec],
        out_shape=[jax.ShapeDtypeStruct((bsz, ng * s, w), F32), jax.ShapeDtypeStruct((bsz, 1, w), F32)],
        compiler_params=_cparams("parallel"),
        name="lru_carry",
    )(h0, hfin.reshape(bsz, ng * s, w), pfin.reshape(bsz, ng * s, w))
    return carry.reshape(bsz, ng, s, w), final


def _lru_out_kernel(hsum_ref, pb_ref, cb_ref, g_ref, x_hbm, gt_ref, gpost_ref, w_ref, o_ref, x_ring, x_sem):
    r, s, d = o_ref.shape
    w = hsum_ref.shape[-1]
    ng, nr = pl.num_programs(1), pl.num_programs(2)
    step = (pl.program_id(0) * ng + pl.program_id(1)) * nr + pl.program_id(2)
    n_steps = pl.num_programs(0) * ng * nr

    def x_copy(u):
        src = x_hbm.at[u // (ng * nr), pl.ds((u % nr) * r, r), (u // nr) % ng]
        return pltpu.make_async_copy(src, x_ring.at[u % X_RING_SLOTS], x_sem.at[u % X_RING_SLOTS])

    @pl.when(step == 0)
    def _():
        for u in range(X_RING_SLOTS - 1):
            x_copy(u).start()

    @pl.when(step + (X_RING_SLOTS - 1) < n_steps)
    def _():
        x_copy(step + (X_RING_SLOTS - 1)).start()

    h = hsum_ref[...].astype(F32).reshape(r, s, w) + pb_ref[...].astype(F32).reshape(r, s, w) * cb_ref[...][None]
    y = _dot((h.reshape(r * s, w) * _silu(g_ref[...].astype(F32))).astype(BF16), w_ref[...])
    x_copy(step).wait()
    x = x_ring[step % X_RING_SLOTS].reshape(r * s, d)
    o_ref[...] = _post_norm_residual(x, y, gpost_ref[...], gt_ref[...]).reshape(r, s, d)


def _lru_out(hsum, pb, cb, g, x5, gt, gpost, w, rb):
    bsz, r, ng, s, d = x5.shape
    nr = r // rb
    tm = rb * s
    width = hsum.shape[-1]
    xio = pl.BlockSpec((None, rb, None, s, d), lambda b, g, i: (b, i, g, 0, 0))
    row = pl.BlockSpec((None, tm, width), lambda b, g, i: (b, g * nr + i, 0))
    runs = pl.BlockSpec((None, None, s, width), lambda b, g, i: (b, g, 0, 0))
    vec = pl.BlockSpec((None, 1, d), lambda b, g, i: (b, 0, 0))
    full = lambda a: pl.BlockSpec(a.shape, lambda b, g, i: (0,) * a.ndim)
    return pl.pallas_call(
        _lru_out_kernel,
        grid=(bsz, ng, nr),
        in_specs=[row, row, runs, row, pl.BlockSpec(memory_space=pl.ANY), vec, full(gpost), full(w)],
        out_specs=xio,
        out_shape=jax.ShapeDtypeStruct(x5.shape, F32),
        scratch_shapes=[pltpu.VMEM((X_RING_SLOTS, rb, s, d), F32), pltpu.SemaphoreType.DMA((X_RING_SLOTS,))],
        compiler_params=_cparams("arbitrary", "arbitrary", "arbitrary"),
        name="lru_out",
    )(hsum, pb, cb, g, x5, gt, gpost, w)


def kernel(x, c, ctx, c_ctx, ada_w, ada_b, norm_pre, norm_post, gla_w_in, gla_wg_f, gla_bg_f, gla_wg_b, gla_bg_b, gla_norm, gla_w_out, lru_w_in, lru_conv_w, lru_conv_b, lru_wa_f, lru_ba_f, lru_wx_f, lru_bx_f, lru_lam_f, lru_wa_b, lru_ba_b, lru_wx_b, lru_bx_b, lru_lam_b, lru_w_out):
    bsz, seq, d = x.shape
    ctx_len = ctx.shape[1]
    rows = seq // GRID_W
    assert seq % GRID_W == 0 and GRID_W % N_SEG == 0 and ctx_len % N_SEG == 0, (seq, ctx_len)
    assert rows % CONV_LEFT == 0 and (ctx_len // N_SEG) % CONV_LEFT == 0 and bsz < SUBLANES, (rows, ctx_len, bsz)

    cvec = jnp.concatenate([c, c_ctx[None], jnp.zeros((SUBLANES - bsz - 1, d), F32)], axis=0)
    mod = _ada_modulation(cvec, ada_w, ada_b)

    def mods(i):
        lat = [mod[i, :bsz, None, j * d:(j + 1) * d] for j in range(3)]
        con = [jnp.broadcast_to(mod[i, bsz, None, None, j * d:(j + 1) * d], (bsz, 1, d)) for j in range(3)]
        return lat, con

    (sh, scl, gt), (sh_c, scl_c, gt_c) = mods(0)
    gpre, gpost = norm_pre[0][None], norm_post[0][None]
    rk = GLA_GATE_RANK
    n_main = 2 * GLA_DK + 2 * GLA_DV
    w_in = gla_w_in[0][:, :n_main].astype(BF16)
    w_lr = jnp.pad(gla_w_in[0][:, n_main:], ((0, 0), (0, LANES - 2 * rk))).astype(BF16)
    wg = (jnp.pad(gla_wg_f[0], ((0, LANES - rk), (0, 0))).astype(BF16),
          jnp.pad(gla_wg_b[0], ((rk, LANES - 2 * rk), (0, 0))).astype(BF16))
    bg = (gla_bg_f[0][None], gla_bg_b[0][None])
    gla_args = (gpre, gpost, w_in, w_lr, wg, bg, gla_norm[0][None], gla_w_out[0].astype(BF16))
    s0 = jnp.zeros((bsz, GLA_HEADS, GLA_HEAD_V, GLA_HEAD_K), F32)
    ctx, s_f, s_b = _gla_layer(ctx, sh_c, scl_c, gt_c, *gla_args, s0, s0)
    x, _, _ = _gla_layer(x, sh, scl, gt, *gla_args, s_f, s_b)

    (sh, scl, gt), (sh_c, scl_c, _) = mods(1)
    gpre, gpost = norm_pre[1][None], norm_post[1][None]
    vec = lambda a: a[None]
    gate = lambda wgt, bias: (wgt.astype(BF16), vec(0.5 * bias))
    p_f = (*gate(lru_wa_f[0], lru_ba_f[0]), *gate(lru_wx_f[0], lru_bx_f[0]), vec(lru_lam_f[0]))
    p_b = (*gate(lru_wa_b[0], lru_ba_b[0]), *gate(lru_wx_b[0], lru_bx_b[0]), vec(lru_lam_b[0]))
    scan_args = (gpre, lru_w_in[0].astype(BF16), 0.5 * lru_conv_w[0], vec(0.5 * lru_conv_b[0]), p_f, p_b)
    h0 = jnp.zeros((bsz, 1, LRU_WIDTH), F32)
    cstep = ctx_len // N_SEG
    ctx5 = ctx.reshape(bsz, N_SEG, cstep, d).transpose(0, 2, 1, 3).reshape(bsz, cstep, 1, N_SEG, d)
    _, _, _, hbfin, pbfin, s_f = _lru_scan(ctx5, sh_c, scl_c, *scan_args, h0)
    _, s_b = _lru_carry(h0, hbfin, pbfin)
    x5 = x.reshape(bsz, rows, GRID_W // N_SEG, N_SEG, d)
    hsum, pb, g, hbfin, pbfin, _ = _lru_scan(x5, sh, scl, *scan_args, s_f)
    cb, _ = _lru_carry(s_b, hbfin, pbfin)
    out5 = _lru_out(hsum, pb, cb, g, x5, gt, gpost, lru_w_out[0].astype(BF16), rb=min(LRU_OUT_STEPS, rows))
    return out5.reshape(bsz, seq, d)
```
